```python
import math
import jax, jax.numpy as jnp
from jax import lax
import numpy as np

D_MODEL = 1024
BATCH = 2
SEQ = 8192
DEPTH = 1
DEC_BATCH = 128
DEC_SEQ = 4
PAST_LEN = 8192
PAGE_SIZE = 128

N_HEADS = 8
N_KV = 2
GROUP = N_HEADS // N_KV
HEAD_DIM = 64
ATTN_DIM = N_HEADS * HEAD_DIM
CMP_BLOCK = 32
CMP_STRIDE = 16
SEL_BLOCK = 64
SEL_TOP = 16
WINDOW = 512
Q_BLOCK = 128
N_BUCKETS = 32
MAX_DISTANCE = 128
CONV_DIM = 512
CONV_W = 3
PEER_HEADS = 8
PEER_NKEYS = 128
PEER_EXPERTS = PEER_NKEYS * PEER_NKEYS
PEER_KEY_DIM = 256
PEER_HALF = PEER_KEY_DIM // 2
PEER_TOPK = 16
PEER_CHUNK = 128
PLE_DIM = 256
EPS = 1e-6
NEG = -1e30
FORCE = 1e4

KV_W = 2 * N_KV * HEAD_DIM
SPLIT_SIZES = (ATTN_DIM, KV_W, KV_W, KV_W, 3 * N_HEADS, CONV_DIM, CONV_DIM, CONV_DIM, 2 * D_MODEL)
IN_COLS = ATTN_DIM + 3 * KV_W + 3 * N_HEADS + 3 * CONV_DIM + 2 * D_MODEL

kernel_name = 'nsa_shortconv_peer_hybrid_step'


def rmsnorm(x, g):
    x32 = x.astype(jnp.float32)
    y = x32 * lax.rsqrt(jnp.mean(x32 * x32, axis=-1, keepdims=True) + EPS)
    return (y * g.astype(jnp.float32)).astype(x.dtype)


def rel_bucket(dist):
    n = jnp.maximum(dist, 0)
    max_exact = N_BUCKETS // 2
    nf = jnp.maximum(n, 1).astype(jnp.float32)
    large = max_exact + (jnp.log(nf / max_exact) / math.log(MAX_DISTANCE / max_exact)
                         * (N_BUCKETS - max_exact)).astype(jnp.int32)
    large = jnp.minimum(large, N_BUCKETS - 1)
    return jnp.where(n < max_exact, n, large)


def head_bias(rel_bias, bucket):
    Q, K = bucket.shape
    b = rel_bias[bucket].reshape(Q, K, N_KV, GROUP)
    return jnp.transpose(b, (2, 3, 0, 1)).astype(jnp.float32)


def masked_softmax(logits, mask, axis):
    p = jax.nn.softmax(jnp.where(mask, logits, NEG), axis=axis)
    return jnp.where(mask, p, 0.0)


def overlap_matrix(n_c, n_sel):
    i = jnp.arange(n_c)[:, None]
    j = jnp.arange(n_sel)[None, :]
    hit = (i * CMP_STRIDE < (j + 1) * SEL_BLOCK) & (i * CMP_STRIDE + CMP_BLOCK > j * SEL_BLOCK)
    return hit.astype(jnp.float32)


def split_cols(z):
    parts, start = [], 0
    for size in SPLIT_SIZES:
        parts.append(z[..., start:start + size])
        start += size
    return parts


def chunk_proj(kv, w_cmp):
    B, T = kv.shape[:2]
    n_ch = T // CMP_STRIDE
    ch = kv[:, :n_ch * CMP_STRIDE].reshape(B, n_ch, CMP_STRIDE, 2, N_KV, HEAD_DIM)
    first = jnp.einsum('bnlcgd,lcde->bncge', ch, w_cmp[:CMP_STRIDE])
    second = jnp.einsum('bnlcgd,lcde->bncge', ch, w_cmp[CMP_STRIDE:])
    return first, second


def compress(first, second, cmp_pos, w_cmp):
    bias = jnp.einsum('lcd,lcde->ce', cmp_pos, w_cmp)
    kv_c = first[:, :-1] + second[:, 1:] + bias[None, None, :, None, :]
    cmp_end = jnp.arange(kv_c.shape[1]) * CMP_STRIDE + CMP_BLOCK - 1
    return kv_c, cmp_end


def to_blocks(parts):
    B = parts[0].shape[0]
    T = sum(p.shape[1] for p in parts)
    n_sel = -(-T // SEL_BLOCK)
    pad = jnp.zeros((B, n_sel * SEL_BLOCK - T) + parts[0].shape[2:], parts[0].dtype)
    rows = jnp.concatenate(list(parts) + [pad], axis=1)
    return rows.reshape(B, n_sel, SEL_BLOCK, 2, N_KV, HEAD_DIM)


def gather_pages(cache, layer, page_table):
    pages = cache[layer, page_table]
    B, n_pages = page_table.shape
    return pages.reshape(B, n_pages * PAGE_SIZE, 2, N_KV, HEAD_DIM)


def nsa_attend(q, qpos, gate, kv_c, cmp_end, slc_blocks, kv_w, win_pos, rel_bias):
    f32 = jnp.float32
    B, Q = q.shape[:2]
    qg = (q.astype(f32) * HEAD_DIM ** -0.5).reshape(B, Q, N_KV, GROUP, HEAD_DIM)
    k_c = kv_c[:, :, 0].astype(f32)
    v_c = kv_c[:, :, 1].astype(f32)
    dist_c = qpos[:, None] - cmp_end[None, :]
    lc = jnp.einsum('bqghd,bngd->bghqn', qg, k_c) + head_bias(rel_bias, rel_bucket(dist_c))[None]
    p_c = masked_softmax(lc, (dist_c >= 0)[None, None, None], -1)
    o_c = jnp.einsum('bghqn,bngd->bqghd', p_c, v_c)
    n_sel = slc_blocks.shape[1]
    imp = jnp.einsum('bghqn,nj->bgqj', p_c, overlap_matrix(kv_c.shape[1], n_sel))
    blk = jnp.arange(n_sel)[None, :]
    cur = (qpos // SEL_BLOCK)[:, None]
    forced = (blk == 0) | (blk == cur) | (blk == cur - 1)
    future = blk * SEL_BLOCK > qpos[:, None]
    imp = jnp.where(forced, FORCE, jnp.where(future, -FORCE, imp))
    _, sel = lax.top_k(imp, min(SEL_TOP, n_sel))
    bi = jnp.arange(B)[:, None, None, None]
    gi = jnp.arange(N_KV)[None, :, None, None]
    kv_s = slc_blocks[bi, sel, :, :, gi, :]
    k_s = kv_s[..., 0, :].astype(f32)
    v_s = kv_s[..., 1, :].astype(f32)
    kpos = sel[..., None] * SEL_BLOCK + jnp.arange(SEL_BLOCK)
    dist_s = qpos[None, None, :, None, None] - kpos
    tbl = rel_bias.T.reshape(N_KV, GROUP, N_BUCKETS)
    bias_s = jnp.moveaxis(tbl[gi[..., None], :, rel_bucket(dist_s)], -1, 2).astype(f32)
    ls = jnp.einsum('bqghd,bgqnld->bghqnl', qg, k_s) + bias_s
    p_s = masked_softmax(ls, (dist_s >= 0)[:, :, None], (-2, -1))
    o_s = jnp.einsum('bghqnl,bgqnld->bqghd', p_s, v_s)
    k_w = kv_w[:, :, 0].astype(f32)
    v_w = kv_w[:, :, 1].astype(f32)
    dist_w = qpos[:, None] - win_pos[None, :]
    mask_w = (dist_w >= 0) & (dist_w < WINDOW) & (win_pos[None, :] >= 0)
    lw = jnp.einsum('bqghd,bkgd->bghqk', qg, k_w) + head_bias(rel_bias, rel_bucket(dist_w))[None]
    p_w = masked_softmax(lw, mask_w[None, None, None], -1)
    o_w = jnp.einsum('bghqk,bkgd->bqghd', p_w, v_w)
    g = gate.astype(f32).reshape(B, Q, N_KV, GROUP, 3)
    o = g[..., 0:1] * o_c + g[..., 1:2] * o_s + g[..., 2:3] * o_w
    return o.reshape(B, Q, ATTN_DIM).astype(q.dtype)


def nsa_prompt(q, gate, kv_cmp, kv_slc, kv_win, w_cmp, cmp_pos, rel_bias):
    B, T = q.shape[:2]
    first, second = chunk_proj(kv_cmp, w_cmp)
    kv_c, cmp_end = compress(first, second, cmp_pos, w_cmp)
    blocks = to_blocks([kv_slc])
    kv_w_pad = jnp.pad(kv_win, ((0, 0), (WINDOW, 0), (0, 0), (0, 0), (0, 0)))

    def one_block(i):
        start = i * Q_BLOCK
        qb = lax.dynamic_slice_in_dim(q, start, Q_BLOCK, axis=1)
        gb = lax.dynamic_slice_in_dim(gate, start, Q_BLOCK, axis=1)
        wb = lax.dynamic_slice_in_dim(kv_w_pad, start, WINDOW + Q_BLOCK, axis=1)
        qpos = start + jnp.arange(Q_BLOCK)
        wpos = start - WINDOW + jnp.arange(WINDOW + Q_BLOCK)
        return nsa_attend(qb, qpos, gb, kv_c, cmp_end, blocks, wb, wpos, rel_bias)

    out = lax.map(one_block, jnp.arange(T // Q_BLOCK))
    return jnp.moveaxis(out, 0, 1).reshape(B, T, ATTN_DIM)


def nsa_sample(q, gate, kv_cmp_new, kv_slc_new, kv_win_new, cache_cmp_kv, cache_slc_kv, layer,
               page_table, win_buf, w_cmp, cmp_pos, rel_bias):
    B, T_new = q.shape[:2]
    past_len = page_table.shape[1] * PAGE_SIZE
    f_p, s_p = chunk_proj(gather_pages(cache_cmp_kv, layer, page_table), w_cmp)
    f_n, s_n = chunk_proj(kv_cmp_new, w_cmp)
    kv_c, cmp_end = compress(jnp.concatenate([f_p, f_n], axis=1),
                             jnp.concatenate([s_p, s_n], axis=1), cmp_pos, w_cmp)
    blocks = to_blocks([gather_pages(cache_slc_kv, layer, page_table), kv_slc_new])
    n_buf = win_buf.shape[1]
    kv_w = jnp.concatenate([win_buf, kv_win_new], axis=1)
    wpos = past_len - n_buf + jnp.arange(n_buf + T_new)
    qpos = past_len + jnp.arange(T_new)
    out = nsa_attend(q, qpos, gate, kv_c, cmp_end, blocks, kv_w, wpos, rel_bias)
    return out, kv_w[:, kv_w.shape[1] - n_buf:]


def short_conv(u, buf, conv_w, conv_b):
    xp = jnp.concatenate([buf, u], axis=1)
    T = u.shape[1]
    y = conv_b + conv_w[0] * xp[:, 0:T]
    for j in range(1, CONV_W):
        y = y + conv_w[j] * xp[:, j:j + T]
    return y, xp[:, xp.shape[1] - (CONV_W - 1):]


def peer_ffn(h, wq, k1, k2, u_tab, v_tab):
    f32 = jnp.float32
    N = h.shape[0]
    n_pad = -(-N // PEER_CHUNK) * PEER_CHUNK
    hp = jnp.pad(h, ((0, n_pad - N), (0, 0))).reshape(n_pad // PEER_CHUNK, PEER_CHUNK, D_MODEL)

    def one_chunk(hc):
        q = (hc @ wq).astype(f32).reshape(PEER_CHUNK, PEER_HEADS, 2, PEER_HALF)
        s1 = jnp.einsum('nhd,kd->nhk', q[:, :, 0], k1.astype(f32))
        s2 = jnp.einsum('nhd,kd->nhk', q[:, :, 1], k2.astype(f32))
        v1, i1 = lax.top_k(s1, PEER_TOPK)
        v2, i2 = lax.top_k(s2, PEER_TOPK)
        cand = (v1[..., :, None] + v2[..., None, :]).reshape(PEER_CHUNK, PEER_HEADS, PEER_TOPK * PEER_TOPK)
        sc, ci = lax.top_k(cand, PEER_TOPK)
        e1 = jnp.take_along_axis(i1, ci // PEER_TOPK, axis=-1)
        e2 = jnp.take_along_axis(i2, ci % PEER_TOPK, axis=-1)
        idx = e1 * PEER_NKEYS + e2
        g = jax.nn.softmax(sc, axis=-1)
        act = jax.nn.gelu(jnp.einsum('nd,nhkd->nhk', hc, u_tab[idx]).astype(f32), approximate=False)
        return jnp.einsum('nhk,nhkd->nd', (g * act).astype(hc.dtype), v_tab[idx])

    out = lax.map(one_chunk, hp).reshape(n_pad, D_MODEL)
    return out[:N]


def mixer_inputs(x, norm_g, w_in):
    B, T = x.shape[:2]
    h = rmsnorm(x, norm_g)
    q, kc, ks, kw, gn, cv, cb, cc, mg = split_cols(h @ w_in)
    kv_shape = (B, T, 2, N_KV, HEAD_DIM)
    return (q.reshape(B, T, N_HEADS, HEAD_DIM), kc.reshape(kv_shape), ks.reshape(kv_shape),
            kw.reshape(kv_shape), jax.nn.sigmoid(gn).reshape(B, T, N_HEADS, 3), cc * cv, cb,
            jax.nn.sigmoid(mg))


def merge_and_channel_mix(x, attn, conv_y, conv_b_gate, merge_gate, p, w_attn_up, w_conv_up, w_out,
                          norm2_g, peer_wq, peer_k1, peer_k2, peer_u, peer_v, ple_g, w_ple_gate, w_ple_proj):
    B, T = x.shape[:2]
    mixed = (merge_gate[..., :D_MODEL] * (attn @ w_attn_up)
             + merge_gate[..., D_MODEL:] * ((conv_b_gate * conv_y) @ w_conv_up))
    x = x + mixed @ w_out
    h2 = rmsnorm(x, norm2_g).reshape(B * T, D_MODEL)
    x = x + peer_ffn(h2, peer_wq, peer_k1, peer_k2, peer_u, peer_v).reshape(B, T, D_MODEL)
    ple_gate = jax.nn.sigmoid(rmsnorm(x, ple_g) @ w_ple_gate)
    return x + ple_gate * (p @ w_ple_proj)


def setup_inputs(seed: int = 0) -> dict:
    key = jax.random.key(seed)
    ks = jax.random.split(key, 32)
    f32 = jnp.float32
    n_pages = PAST_LEN // PAGE_SIZE
    n_used = DEC_BATCH * n_pages
    n_pool = n_used + n_used // 4
    w_buf = min(WINDOW, PAST_LEN)

    def nrm(k, shape, scale):
        return jax.random.normal(k, shape, f32) * scale

    page_table = jax.random.permutation(ks[0], n_pool)[:n_used].reshape(DEC_BATCH, n_pages).astype(jnp.int32)
    return {
        'x_prompt': nrm(ks[1], (BATCH, SEQ, D_MODEL), 1.0),
        'x_sample': nrm(ks[2], (DEC_BATCH, DEC_SEQ, D_MODEL), 1.0),
        'p_prompt': nrm(ks[3], (DEPTH, BATCH, SEQ, PLE_DIM), 1.0),
        'p_sample': nrm(ks[4], (DEPTH, DEC_BATCH, DEC_SEQ, PLE_DIM), 1.0),
        'cache_cmp_kv': nrm(ks[5], (DEPTH, n_pool, PAGE_SIZE, 2, N_KV, HEAD_DIM), 1.0),
        'cache_slc_kv': nrm(ks[6], (DEPTH, n_pool, PAGE_SIZE, 2, N_KV, HEAD_DIM), 1.0),
        'page_table': page_table,
        'state_win_kv': nrm(ks[7], (DEPTH, DEC_BATCH, w_buf, 2, N_KV, HEAD_DIM), 1.0),
        'state_conv': nrm(ks[8], (DEPTH, DEC_BATCH, CONV_W - 1, CONV_DIM), 1.0),
        'norm1_g': 1.0 + nrm(ks[9], (DEPTH, D_MODEL), 0.02),
        'w_in': nrm(ks[10], (DEPTH, D_MODEL, IN_COLS), D_MODEL ** -0.5),
        'w_cmp': nrm(ks[11], (DEPTH, CMP_BLOCK, 2, HEAD_DIM, HEAD_DIM), (CMP_BLOCK * HEAD_DIM) ** -0.5),
        'cmp_pos': nrm(ks[12], (DEPTH, CMP_BLOCK, 2, HEAD_DIM), 0.1),
        'conv_w': nrm(ks[13], (DEPTH, CONV_W, CONV_DIM), CONV_W ** -0.5),
        'conv_b': nrm(ks[14], (DEPTH, CONV_DIM), 0.01),
        'w_attn_up': nrm(ks[15], (DEPTH, ATTN_DIM, D_MODEL), ATTN_DIM ** -0.5),
        'w_conv_up': nrm(ks[16], (DEPTH, CONV_DIM, D_MODEL), CONV_DIM ** -0.5),
        'w_out': nrm(ks[17], (DEPTH, D_MODEL, D_MODEL), D_MODEL ** -0.5),
        'norm2_g': 1.0 + nrm(ks[18], (DEPTH, D_MODEL), 0.02),
        'peer_wq': nrm(ks[19], (DEPTH, D_MODEL, PEER_HEADS * PEER_KEY_DIM), D_MODEL ** -0.5),
        'peer_k1': nrm(ks[20], (DEPTH, PEER_NKEYS, PEER_HALF), PEER_HALF ** -0.5),
        'peer_k2': nrm(ks[21], (DEPTH, PEER_NKEYS, PEER_HALF), PEER_HALF ** -0.5),
        'peer_u': nrm(ks[22], (DEPTH, PEER_EXPERTS, D_MODEL), D_MODEL ** -0.5),
        'peer_v': nrm(ks[23], (DEPTH, PEER_EXPERTS, D_MODEL), PEER_HEADS ** -0.5),
        'ple_g': 1.0 + nrm(ks[24], (DEPTH, D_MODEL), 0.02),
        'w_ple_gate': nrm(ks[25], (DEPTH, D_MODEL, D_MODEL), D_MODEL ** -0.5),
        'w_ple_proj': nrm(ks[26], (DEPTH, PLE_DIM, D_MODEL), PLE_DIM ** -0.5),
        'rel_bias': nrm(ks[27], (N_BUCKETS, N_HEADS), 0.5),
        'final_g': 1.0 + nrm(ks[28], (D_MODEL,), 0.02),
    }


def reference(x_prompt, x_sample, p_prompt, p_sample, cache_cmp_kv, cache_slc_kv, page_table,
              state_win_kv, state_conv, norm1_g, w_in, w_cmp, cmp_pos, conv_w, conv_b, w_attn_up,
              w_conv_up, w_out, norm2_g, peer_wq, peer_k1, peer_k2, peer_u, peer_v, ple_g,
              w_ple_gate, w_ple_proj, rel_bias, final_g):
    xp, xs = x_prompt, x_sample
    cmp_p, cmp_s, slc_p, slc_s, win_p, win_s, conv_p, conv_s = [], [], [], [], [], [], [], []
    for i in range(DEPTH):
        tail = (w_attn_up[i], w_conv_up[i], w_out[i], norm2_g[i], peer_wq[i], peer_k1[i], peer_k2[i],
                peer_u[i], peer_v[i], ple_g[i], w_ple_gate[i], w_ple_proj[i])
        q, kvc, kvs, kvw, gn, cin, cbg, mg = mixer_inputs(xp, norm1_g[i], w_in[i])
        attn = nsa_prompt(q, gn, kvc, kvs, kvw, w_cmp[i], cmp_pos[i], rel_bias)
        buf0 = jnp.zeros((cin.shape[0], CONV_W - 1, CONV_DIM), cin.dtype)
        conv_y, conv_state = short_conv(cin, buf0, conv_w[i], conv_b[i])
        xp = merge_and_channel_mix(xp, attn, conv_y, cbg, mg, p_prompt[i], *tail)
        cmp_p.append(kvc)
        slc_p.append(kvs)
        win_p.append(kvw[:, kvw.shape[1] - min(WINDOW, kvw.shape[1]):])
        conv_p.append(conv_state)
        q, kvc, kvs, kvw, gn, cin, cbg, mg = mixer_inputs(xs, norm1_g[i], w_in[i])
        attn, new_win = nsa_sample(q, gn, kvc, kvs, kvw, cache_cmp_kv, cache_slc_kv, i, page_table,
                                   state_win_kv[i], w_cmp[i], cmp_pos[i], rel_bias)
        conv_y, conv_state = short_conv(cin, state_conv[i], conv_w[i], conv_b[i])
        xs = merge_and_channel_mix(xs, attn, conv_y, cbg, mg, p_sample[i], *tail)
        cmp_s.append(kvc)
        slc_s.append(kvs)
        win_s.append(new_win)
        conv_s.append(conv_state)
    y_prompt = rmsnorm(xp, final_g)
    y_sample = rmsnorm(xs, final_g)
    return (y_prompt, y_sample, jnp.stack(cmp_p), jnp.stack(cmp_s), jnp.stack(slc_p), jnp.stack(slc_s),
            jnp.stack(win_p), jnp.stack(win_s), jnp.stack(conv_p), jnp.stack(conv_s))
```

```python
import functools
import math

import jax
import jax.numpy as jnp
import numpy as np
from jax import lax
from jax.experimental import pallas as pl
from jax.experimental.pallas import tpu as pltpu

f32 = jnp.float32
bf16 = jnp.bfloat16
i32 = jnp.int32

N_HEADS = 8
N_KV = 2
GROUP = N_HEADS // N_KV
HEAD_DIM = 64
ATTN_DIM = N_HEADS * HEAD_DIM
KV_W = 2 * N_KV * HEAD_DIM
CMP_BLOCK = 32
CMP_STRIDE = 16
SEL_BLOCK = 64
SEL_SHIFT = 6
SEL_TOP = 16
WINDOW = 512
N_BUCKETS = 32
MAX_EXACT = N_BUCKETS // 2
MAX_DISTANCE = 128
CONV_DIM = 512
CONV_W = 3
PEER_HEADS = 8
PEER_NKEYS = 128
PEER_HALF = 128
PEER_TOPK = 16
PAGE_SIZE = 128
EPS = 1e-6
NEG = -1e30
FORCE = 1e4

LANES = 128
SUBLANES = 8
VMEM_LIMIT_BYTES = 56 * 1024 * 1024

Q_TILE = 128
FAR_CHUNK = 512
CMP_PAD = 16
CMP_WIN = 24
ROW_TILE = 256
EXPERT_CHUNK = 1024
G_PITCH_PAD = 8


def _cparams(sem):
    return pltpu.CompilerParams(dimension_semantics=sem, vmem_limit_bytes=VMEM_LIMIT_BYTES)


def _dot(a, b):
    return jnp.dot(a, b, preferred_element_type=f32)


def _dot_nt(a, b):
    return lax.dot_general(a, b, (((1,), (1,)), ((), ())), preferred_element_type=f32)


def _dot_tn(a, b):
    return lax.dot_general(a, b, (((0,), (0,)), ((), ())), preferred_element_type=f32)


def _split_bf16(x):
    hi = x.astype(bf16)
    lo = (x - hi.astype(f32)).astype(bf16)
    return hi, lo


def _resident(shape):
    nd = len(shape)
    return pl.BlockSpec(shape, lambda *_: (0,) * nd)


def _rmsnorm(x, g):
    ms = jnp.mean(x * x, axis=-1, keepdims=True)
    return x * lax.rsqrt(ms + EPS) * g


def _in_proj_kernel(x_ref, g_ref, w_ref, q_ref, kc_ref, ks_ref, kw_ref, ksb_ref, kwb_ref,
                    gn_ref, cin_ref, cb_ref, mg_ref, *, d_model):
    qw = N_HEADS * LANES
    h = _rmsnorm(x_ref[...], g_ref[...]).astype(bf16)
    o = 0
    q_ref[...] = (_dot(h, w_ref[:, o:o + qw]) * (HEAD_DIM ** -0.5)).astype(bf16)
    o += qw
    kc_ref[...] = _dot(h, w_ref[:, o:o + KV_W])
    o += KV_W
    ks = _dot(h, w_ref[:, o:o + KV_W])
    ks_ref[...] = ks
    ksb_ref[...] = ks.astype(bf16)
    o += KV_W
    kw = _dot(h, w_ref[:, o:o + KV_W])
    kw_ref[...] = kw
    kwb_ref[...] = kw.astype(bf16)
    o += KV_W
    gn_ref[...] = jax.nn.sigmoid(_dot(h, w_ref[:, o:o + LANES]))
    o += LANES
    cv = _dot(h, w_ref[:, o:o + CONV_DIM])
    o += CONV_DIM
    cb_ref[...] = _dot(h, w_ref[:, o:o + CONV_DIM])
    o += CONV_DIM
    cin_ref[...] = _dot(h, w_ref[:, o:o + CONV_DIM]) * cv
    o += CONV_DIM
    mg_ref[...] = jax.nn.sigmoid(_dot(h, w_ref[:, o:o + 2 * d_model]))


def _pack_w_in(w_in, d_model):
    sizes = (ATTN_DIM, KV_W, KV_W, KV_W, 3 * N_HEADS, CONV_DIM, CONV_DIM, CONV_DIM, 2 * d_model)
    parts, s = [], 0
    for n in sizes:
        parts.append(w_in[:, s:s + n])
        s += n
    wq = parts[0].reshape(d_model, N_KV, GROUP, HEAD_DIM)
    z = jnp.zeros_like(wq)
    slabs = [jnp.concatenate([wq[:, 0], z[:, 0]], axis=-1), jnp.concatenate([z[:, 1], wq[:, 1]], axis=-1)]
    wq = jnp.stack(slabs, axis=1).reshape(d_model, N_HEADS * LANES)
    gn = jnp.pad(parts[4], ((0, 0), (0, LANES - 3 * N_HEADS)))
    return jnp.concatenate([wq, parts[1], parts[2], parts[3], gn] + parts[5:], axis=1).astype(bf16)


def _in_proj(x, norm_g, w_pack):
    n, d = x.shape
    tm = ROW_TILE
    assert n % tm == 0
    row = lambda w: pl.BlockSpec((tm, w), lambda i: (i, 0))
    widths = (N_HEADS * LANES, KV_W, KV_W, KV_W, KV_W, KV_W, LANES, CONV_DIM, CONV_DIM, 2 * d)
    dtypes = (bf16, f32, f32, f32, bf16, bf16, f32, f32, f32, f32)
    return pl.pallas_call(
        functools.partial(_in_proj_kernel, d_model=d),
        grid=(n // tm,),
        in_specs=[row(d), _resident((1, d)), _resident(w_pack.shape)],
        out_specs=[row(w) for w in widths],
        out_shape=[jax.ShapeDtypeStruct((n, w), t) for w, t in zip(widths, dtypes)],
        compiler_params=_cparams(("parallel",)),
        name="in_proj",
    )(x, norm_g.reshape(1, d), w_pack)


def _bias_of_dist(dist, valid, rb_ref, head):
    n = jnp.maximum(dist, 0)
    nf = jnp.maximum(n, 1).astype(f32)
    large = MAX_EXACT + (jnp.log(nf / MAX_EXACT) / math.log(MAX_DISTANCE / MAX_EXACT)
                         * (N_BUCKETS - MAX_EXACT)).astype(i32)
    large = jnp.minimum(large, N_BUCKETS - 1)
    bucket = jnp.where(n < MAX_EXACT, n, large)
    out = jnp.zeros(dist.shape, f32)
    for b in range(N_BUCKETS):
        out = jnp.where(bucket == b, rb_ref[b, head], out)
    return jnp.where(valid, out, NEG)


def _tables_kernel(rb_ref, wt_ref, ct_ref, c31_ref, sk_ref, snew_ref, swin_ref, scmp_ref, c31s_ref,
                   *, past_len, ncp_s, n_cmp_s):
    ik = lax.broadcasted_iota(i32, (Q_TILE, Q_TILE), 0)
    iq = lax.broadcasted_iota(i32, (Q_TILE, Q_TILE), 1)
    n_w = WINDOW // Q_TILE + 1
    for g in range(N_KV):
        for hh in range(GROUP):
            head = g * GROUP + hh
            cols = slice(hh * Q_TILE, (hh + 1) * Q_TILE)
            for w in range(n_w):
                dist = Q_TILE * (n_w - 1 - w) + iq - ik
                wt_ref[g, w, :, cols] = _bias_of_dist(dist, (dist >= 0) & (dist < WINDOW), rb_ref, head)
            mm = lax.broadcasted_iota(i32, (CMP_WIN, Q_TILE), 0)
            jq = lax.broadcasted_iota(i32, (CMP_WIN, Q_TILE), 1)
            dist = jq - CMP_STRIDE * (mm - CMP_PAD) - (CMP_BLOCK - 1)
            ct_ref[g, :, cols] = _bias_of_dist(dist, dist >= 0, rb_ref, head)
            c31_ref[g, :, cols] = jnp.full((SUBLANES, Q_TILE), rb_ref[N_BUCKETS - 1, head], f32)
    def col_tables(rows, dist_fn, valid_fn, out_ref):
        r = lax.broadcasted_iota(i32, (rows, LANES), 0)
        c = lax.broadcasted_iota(i32, (rows, LANES), 1)
        t = c & 3
        hd = jnp.minimum(lax.shift_right_logical(c, 2), N_HEADS - 1)
        dist = dist_fn(r, t)
        valid = valid_fn(r, t, dist)
        acc = jnp.zeros((rows, LANES), f32)
        for head in range(N_HEADS):
            acc = jnp.where(hd == head, _bias_of_dist(dist, valid, rb_ref, head), acc)
        out_ref[...] = acc

    col_tables(PAGE_SIZE, lambda r, t: PAGE_SIZE + t - r, lambda r, t, d: d >= 0, sk_ref)
    col_tables(16, lambda r, t: t - r, lambda r, t, d: (d >= 0) & (r < 4), snew_ref)
    col_tables(WINDOW, lambda r, t: WINDOW + t - r, lambda r, t, d: d < WINDOW, swin_ref)
    col_tables(ncp_s, lambda r, t: past_len + t - CMP_STRIDE * (r - CMP_PAD) - (CMP_BLOCK - 1),
               lambda r, t, d: (r >= CMP_PAD) & (r < CMP_PAD + n_cmp_s) & (d >= 0), scmp_ref)
    c = lax.broadcasted_iota(i32, (SUBLANES, LANES), 1)
    hd = jnp.minimum(lax.shift_right_logical(c, 2), N_HEADS - 1)
    acc = jnp.zeros((SUBLANES, LANES), f32)
    for head in range(N_HEADS):
        acc = jnp.where(hd == head, rb_ref[N_BUCKETS - 1, head], acc)
    c31s_ref[...] = acc


def _tables(rel_bias, past_len, ncp_s, n_cmp_s):
    n_w = WINDOW // Q_TILE + 1
    cols = GROUP * Q_TILE
    shapes = [
        (N_KV, n_w, Q_TILE, cols), (N_KV, CMP_WIN, cols), (N_KV, SUBLANES, cols),
        (PAGE_SIZE, LANES), (16, LANES), (WINDOW, LANES), (ncp_s, LANES), (SUBLANES, LANES),
    ]
    return pl.pallas_call(
        functools.partial(_tables_kernel, past_len=past_len, ncp_s=ncp_s, n_cmp_s=n_cmp_s),
        in_specs=[pl.BlockSpec(memory_space=pltpu.SMEM)],
        out_shape=[jax.ShapeDtypeStruct(s, f32) for s in shapes],
        compiler_params=pltpu.CompilerParams(vmem_limit_bytes=VMEM_LIMIT_BYTES),
        name="bias_tables",
    )(rel_bias)


def _ncp(n_chunks):
    return -(-(n_chunks + CMP_PAD + SUBLANES) // LANES) * LANES


def _compress_kernel(*refs, n_rows_in, n_chunks):
    refs = refs[len(refs) - n_rows_in - 3:]
    chunk_refs = refs[:n_rows_in]
    w_ref, pos_ref, out_ref = refs[n_rows_in:]
    if n_rows_in == 1:
        x = chunk_refs[0][0]
    else:
        x = jnp.concatenate([r[0] for r in chunk_refs], axis=0)
    fs = _dot(x.astype(bf16), w_ref[...])
    pb = _dot(pos_ref[...], w_ref[...])
    bias = pb[0:1, :KV_W] + pb[1:2, KV_W:]
    nxt = pltpu.roll(fs[:, KV_W:], n_chunks - 1, axis=0)
    kc = fs[:, :KV_W] + nxt + bias
    r = lax.broadcasted_iota(i32, kc.shape, 0)
    kc = jnp.where(r < n_chunks - 1, kc, 0.0)
    out_ref[...] = jnp.zeros(out_ref.shape, out_ref.dtype)
    out_ref[0, CMP_PAD:CMP_PAD + n_chunks, :] = kc.astype(out_ref.dtype)


def _pack_w_cmp(w_cmp, cmp_pos):
    eye_c = jnp.eye(2, dtype=f32)
    eye_g = jnp.eye(N_KV, dtype=f32)

    def half(w):
        big = jnp.einsum("lcde,cC,gG->lcgdCGe", w, eye_c, eye_g)
        return big.reshape(CMP_STRIDE * KV_W, KV_W)

    w_big = jnp.concatenate([half(w_cmp[:CMP_STRIDE]), half(w_cmp[CMP_STRIDE:])], axis=1)

    def pos_row(p):
        return jnp.broadcast_to(p[:, :, None, :], (CMP_STRIDE, 2, N_KV, HEAD_DIM)).reshape(1, -1)

    pos = jnp.concatenate([pos_row(cmp_pos[:CMP_STRIDE]), pos_row(cmp_pos[CMP_STRIDE:]),
                           jnp.zeros((SUBLANES - 2, CMP_STRIDE * KV_W), f32)], axis=0)
    return w_big.astype(bf16), pos.astype(bf16)


def _compress_prompt(kv_cmp, w_big, pos):
    b, t, _ = kv_cmp.shape
    n_chunks = t // CMP_STRIDE
    cw = CMP_STRIDE * KV_W
    chunks = kv_cmp.reshape(b, n_chunks, cw)
    ncp = _ncp(n_chunks)
    return pl.pallas_call(
        functools.partial(_compress_kernel, n_rows_in=1, n_chunks=n_chunks),
        grid=(b,),
        in_specs=[pl.BlockSpec((1, n_chunks, cw), lambda i: (i, 0, 0)), _resident(w_big.shape), _resident(pos.shape)],
        out_specs=pl.BlockSpec((1, ncp, KV_W), lambda i: (i, 0, 0)),
        out_shape=jax.ShapeDtypeStruct((b, ncp, KV_W), bf16),
        compiler_params=_cparams(("parallel",)),
        name="compress_prompt",
    )(chunks, w_big, pos)


def _compress_sample(cache, page_table, w_big, pos):
    n_pool = cache.shape[0]
    bd, n_pages = page_table.shape
    per_page = PAGE_SIZE // CMP_STRIDE
    cw = CMP_STRIDE * KV_W
    pages = cache.reshape(n_pool, per_page, cw)
    n_chunks = n_pages * per_page
    ncp = _ncp(n_chunks)

    def page_spec(k):
        return pl.BlockSpec((1, per_page, cw), lambda i, pt: (pt[i, k], 0, 0))

    grid_spec = pltpu.PrefetchScalarGridSpec(
        num_scalar_prefetch=1,
        grid=(bd,),
        in_specs=[page_spec(k) for k in range(n_pages)]
        + [pl.BlockSpec(w_big.shape, lambda i, pt: (0, 0)), pl.BlockSpec(pos.shape, lambda i, pt: (0, 0))],
        out_specs=pl.BlockSpec((1, ncp, KV_W), lambda i, pt: (i, 0, 0)),
    )
    return pl.pallas_call(
        functools.partial(_compress_kernel, n_rows_in=n_pages, n_chunks=n_chunks),
        grid_spec=grid_spec,
        out_shape=jax.ShapeDtypeStruct((bd, ncp, KV_W), bf16),
        compiler_params=_cparams(("parallel",)),
        name="compress_sample",
    )(page_table, *([pages] * n_pages), w_big, pos)


def _overlap_t(n_sel_rows, n_sel, ncp, n_cmp):
    j = np.arange(n_sel_rows)[:, None]
    n = np.arange(ncp)[None, :] - CMP_PAD
    hit = (n * CMP_STRIDE < (j + 1) * SEL_BLOCK) & (n * CMP_STRIDE + CMP_BLOCK > j * SEL_BLOCK)
    hit &= (n >= 0) & (n < n_cmp) & (j < n_sel)
    return jnp.asarray(hit, dtype=bf16)


def _masked_softmax_cols(s):
    m = jnp.max(s, axis=0, keepdims=True)
    e = jnp.exp(s - m)
    z = jnp.sum(e, axis=0, keepdims=True)
    return jnp.where(s > 0.5 * NEG, e / z, 0.0)


def _top_rows(imp, k):
    rows = lax.broadcasted_iota(i32, imp.shape, 0).astype(f32)
    sel = jnp.zeros(imp.shape, jnp.bool_)
    v = imp
    for _ in range(k):
        m = jnp.max(v, axis=0, keepdims=True)
        first = jnp.min(jnp.where(v == m, rows, 1e9), axis=0, keepdims=True)
        pick = rows == first
        sel = sel | pick
        v = jnp.where(pick, -jnp.inf, v)
    return sel


def _flash_step(carry, s, v_t):
    m, l, acc = carry
    m_new = jnp.maximum(m, jnp.max(s, axis=0, keepdims=True))
    alpha = jnp.exp(m - m_new)
    p = jnp.exp(s - m_new)
    l = alpha * l + jnp.sum(p, axis=0, keepdims=True)
    acc = alpha * acc + _dot(v_t, p.astype(bf16))
    return m_new, l, acc


def _attn_prompt_kernel(q_ref, gn_ref, kc_ref, vct_ref, ks_ref, vst_ref, kw_ref, vwt_ref, ovt_ref,
                        wt_ref, ct_ref, c31_ref, out_ref, cadd_ref, sb0_ref, sbc_ref,
                        *, n_sel, ncp):
    qt = pl.program_id(1)
    cols = GROUP * Q_TILE
    n_w = WINDOW // Q_TILE + 1
    gates_t = gn_ref[0].T
    n_far = jnp.maximum(qt - 1, 0) // (FAR_CHUNK // Q_TILE)
    blocks_per_chunk = FAR_CHUNK // SEL_BLOCK
    blocks_per_tile = Q_TILE // SEL_BLOCK

    for g in range(N_KV):
        qp = jnp.concatenate(
            [q_ref[0, :, (g * GROUP + hh) * LANES:(g * GROUP + hh + 1) * LANES] for hh in range(GROUP)], axis=0)
        c31 = c31_ref[g, 0:1, :]

        ri = lax.broadcasted_iota(i32, (ncp, cols), 0)
        cadd_ref[...] = jnp.where(ri < SUBLANES * qt, c31, NEG)
        cadd_ref[pl.ds(pl.multiple_of(SUBLANES * qt, SUBLANES), CMP_WIN), :] = ct_ref[g]
        lc = _dot_nt(kc_ref[0], qp) + cadd_ref[...]
        lc = jnp.where(ri >= CMP_PAD, lc, NEG)
        pc = _masked_softmax_cols(lc)
        o_c = _dot(vct_ref[0], pc.astype(bf16))
        psum = pc[:, 0:Q_TILE]
        for hh in range(1, GROUP):
            psum = psum + pc[:, hh * Q_TILE:(hh + 1) * Q_TILE]
        p_hi, p_lo = _split_bf16(psum)
        imp = _dot(ovt_ref[...], p_hi) + _dot(ovt_ref[...], p_lo)

        blk = lax.broadcasted_iota(i32, (n_sel, Q_TILE), 0)
        qpos = qt * Q_TILE + lax.broadcasted_iota(i32, (n_sel, Q_TILE), 1)
        cur = lax.shift_right_logical(qpos, SEL_SHIFT)
        forced = (blk == 0) | (blk == cur) | (blk == cur - 1)
        future = blk * SEL_BLOCK > qpos
        imp = jnp.where(forced, FORCE, jnp.where(future, -FORCE, imp))
        sel = _top_rows(imp, min(SEL_TOP, n_sel))
        sb = jnp.where(sel, 0.0, NEG)
        sb = jnp.concatenate([sb] * GROUP, axis=1)
        sb0_ref[...] = sb
        sbc_ref[...] = sb + c31

        def far_step(i, carry):
            ks = pl.multiple_of(i * FAR_CHUNK, FAR_CHUNK)
            s = _dot_nt(ks_ref[0, pl.ds(ks, FAR_CHUNK), :], qp)
            add = sbc_ref[pl.ds(pl.multiple_of(i * blocks_per_chunk, SUBLANES), blocks_per_chunk), :]
            s = (s.reshape(blocks_per_chunk, SEL_BLOCK, cols) + add[:, None, :]).reshape(FAR_CHUNK, cols)
            return _flash_step(carry, s, vst_ref[0, :, pl.ds(ks, FAR_CHUNK)])

        def near_step(kt, carry):
            ks = pl.multiple_of(kt * Q_TILE, Q_TILE)
            s = _dot_nt(ks_ref[0, pl.ds(ks, Q_TILE), :], qp)
            w = jnp.where(kt >= qt - 1, kt - qt + n_w - 1, 1)
            rows = []
            for jb in range(blocks_per_tile):
                r = sb0_ref[pl.ds(kt * blocks_per_tile + jb, 1), :]
                rows.append(jnp.broadcast_to(r, (SEL_BLOCK, cols)))
            s = s + wt_ref[g, w] + jnp.concatenate(rows, axis=0)
            return _flash_step(carry, s, vst_ref[0, :, pl.ds(ks, Q_TILE)])

        init = (jnp.full((1, cols), NEG, f32), jnp.zeros((1, cols), f32), jnp.zeros((LANES, cols), f32))
        carry = lax.fori_loop(0, n_far, far_step, init)
        _, l_s, acc_s = lax.fori_loop(n_far * (FAR_CHUNK // Q_TILE), qt + 1, near_step, carry)
        o_s = acc_s / l_s

        wk = n_w * Q_TILE
        start = pl.multiple_of(qt * Q_TILE, Q_TILE)
        sw = _dot_nt(kw_ref[0, pl.ds(start, wk), :], qp) + wt_ref[g].reshape(wk, cols)
        rw = lax.broadcasted_iota(i32, (wk, cols), 0)
        sw = jnp.where(rw >= WINDOW - qt * Q_TILE, sw, NEG)
        pw = _masked_softmax_cols(sw)
        o_w = _dot(vwt_ref[0, :, pl.ds(start, wk)], pw.astype(bf16))

        for hh in range(GROUP):
            head = g * GROUP + hh
            cs = slice(hh * Q_TILE, (hh + 1) * Q_TILE)
            ds = slice(g * HEAD_DIM, (g + 1) * HEAD_DIM)
            o = (gates_t[3 * head:3 * head + 1, :] * o_c[ds, cs]
                 + gates_t[3 * head + 1:3 * head + 2, :] * o_s[ds, cs]
                 + gates_t[3 * head + 2:3 * head + 3, :] * o_w[ds, cs])
            out_ref[0, head * HEAD_DIM:(head + 1) * HEAD_DIM, :] = o


def _attn_prompt(q_pk, gates, kcp, ks_bf, kw_bf, wt, ct, c31):
    b, t, _ = q_pk.shape
    ncp = kcp.shape[1]
    n_sel = t // SEL_BLOCK
    n_cmp = t // CMP_STRIDE - 1
    assert t % FAR_CHUNK == 0 and n_sel % SUBLANES == 0
    n_w = WINDOW // Q_TILE + 1
    cols = GROUP * Q_TILE
    half = KV_W // 2
    vct = jnp.swapaxes(kcp[:, :, half:], 1, 2)
    vst = jnp.swapaxes(ks_bf[:, :, half:], 1, 2)
    kw_pad = jnp.pad(kw_bf, ((0, 0), (WINDOW, 0), (0, 0)))
    vwt = jnp.swapaxes(kw_pad[:, :, half:], 1, 2)
    ovt = _overlap_t(n_sel, n_sel, ncp, n_cmp)
    tp = t + WINDOW
    per_b = lambda shape: pl.BlockSpec((1,) + shape, lambda i, j: (i, 0, 0))
    attn_t = pl.pallas_call(
        functools.partial(_attn_prompt_kernel, n_sel=n_sel, ncp=ncp),
        grid=(b, t // Q_TILE),
        in_specs=[
            pl.BlockSpec((1, Q_TILE, N_HEADS * LANES), lambda i, j: (i, j, 0)),
            pl.BlockSpec((1, Q_TILE, LANES), lambda i, j: (i, j, 0)),
            per_b((ncp, half)), per_b((half, ncp)),
            per_b((t, half)), per_b((half, t)),
            per_b((tp, half)), per_b((half, tp)),
            _resident(ovt.shape), _resident(wt.shape), _resident(ct.shape), _resident(c31.shape),
        ],
        out_specs=pl.BlockSpec((1, ATTN_DIM, Q_TILE), lambda i, j: (i, 0, j)),
        out_shape=jax.ShapeDtypeStruct((b, ATTN_DIM, t), f32),
        scratch_shapes=[pltpu.VMEM((ncp, cols), f32), pltpu.VMEM((n_sel, cols), f32), pltpu.VMEM((n_sel, cols), f32)],
        compiler_params=_cparams(("parallel", "arbitrary")),
        name="attn_prompt",
    )(q_pk, gates, kcp, vct, ks_bf, vst, kw_pad, vwt, ovt, wt, ct, c31)
    return jnp.swapaxes(attn_t, 1, 2)


def _attn_sample_kernel(*refs, n_pages, n_sel, past_len):
    pt_ref = refs[0]
    page_refs = refs[1:1 + n_pages]
    (q_ref, g_ref, kcp_ref, snew_kv_ref, win_ref, wnew_ref, ovt_ref, rmat_ref,
     sk_ref, snew_ref, swin_ref, scmp_ref, c31_ref, out_ref, kv_ref) = refs[1 + n_pages:]
    del pt_ref
    half = KV_W // 2
    q = q_ref[0]
    c31 = c31_ref[0:1, :]

    kcp = kcp_ref[0]
    pc = _masked_softmax_cols(_dot_nt(kcp[:, :half], q) + scmp_ref[...])
    o_c = _dot_tn(pc.astype(bf16), kcp[:, half:])
    p_hi, p_lo = _split_bf16(pc)
    imp = _dot(ovt_ref[...], p_hi) + _dot(ovt_ref[...], p_lo)
    i_hi, i_lo = _split_bf16(imp)
    imp = _dot(i_hi, rmat_ref[...]) + _dot(i_lo, rmat_ref[...])
    rows = imp.shape[0]
    blk = lax.broadcasted_iota(i32, (rows, LANES), 0)
    qpos = past_len + (lax.broadcasted_iota(i32, (rows, LANES), 1) & 3)
    cur = lax.shift_right_logical(qpos, SEL_SHIFT)
    forced = (blk == 0) | (blk == cur) | (blk == cur - 1)
    future = blk * SEL_BLOCK > qpos
    imp = jnp.where(forced, FORCE, jnp.where(future, -FORCE, imp))
    imp = jnp.where(blk < n_sel, imp, -jnp.inf)
    sel = _top_rows(imp, min(SEL_TOP, n_sel))
    sb = jnp.where(sel, 0.0, NEG)

    for k in range(n_pages):
        kv_ref[k * PAGE_SIZE:(k + 1) * PAGE_SIZE, :] = page_refs[k][0].astype(bf16)
    n_far = past_len - PAGE_SIZE
    far_blocks = n_far // SEL_BLOCK
    s_a = _dot_nt(kv_ref[0:n_far, :half], q)
    s_a = (s_a.reshape(far_blocks, SEL_BLOCK, LANES) + (sb[0:far_blocks] + c31)[:, None, :]).reshape(n_far, LANES)
    s_b = _dot_nt(kv_ref[n_far:past_len, :half], q) + sk_ref[...]
    s_b = (s_b.reshape(2, SEL_BLOCK, LANES) + sb[far_blocks:far_blocks + 2][:, None, :]).reshape(PAGE_SIZE, LANES)
    new_kv = snew_kv_ref[0].astype(bf16)
    s_n = _dot_nt(new_kv[:, :half], q) + snew_ref[...]
    m = jnp.maximum(jnp.maximum(jnp.max(s_a, axis=0, keepdims=True), jnp.max(s_b, axis=0, keepdims=True)),
                    jnp.max(s_n, axis=0, keepdims=True))
    e_a, e_b, e_n = jnp.exp(s_a - m), jnp.exp(s_b - m), jnp.exp(s_n - m)
    inv = 1.0 / (jnp.sum(e_a, axis=0, keepdims=True) + jnp.sum(e_b, axis=0, keepdims=True)
                 + jnp.sum(e_n, axis=0, keepdims=True))
    o_s = (_dot_tn((e_a * inv).astype(bf16), kv_ref[0:n_far, half:])
           + _dot_tn((e_b * inv).astype(bf16), kv_ref[n_far:past_len, half:])
           + _dot_tn((e_n * inv).astype(bf16), new_kv[:, half:]))

    win = win_ref[0].astype(bf16)
    wnew = wnew_ref[0].astype(bf16)
    s_w = _dot_nt(win[:, :half], q) + swin_ref[...]
    s_x = _dot_nt(wnew[:, :half], q) + snew_ref[...]
    m = jnp.maximum(jnp.max(s_w, axis=0, keepdims=True), jnp.max(s_x, axis=0, keepdims=True))
    e_w, e_x = jnp.exp(s_w - m), jnp.exp(s_x - m)
    inv = 1.0 / (jnp.sum(e_w, axis=0, keepdims=True) + jnp.sum(e_x, axis=0, keepdims=True))
    o_w = _dot_tn((e_w * inv).astype(bf16), win[:, half:]) + _dot_tn((e_x * inv).astype(bf16), wnew[:, half:])

    gt = g_ref[0]
    out_ref[0] = gt[:, 0:1] * o_c + gt[:, 1:2] * o_s + gt[:, 2:3] * o_w


def _attn_sample(q_pk, gates, kcp, cache_slc, page_table, ks_new, state_win, kw_new, tabs):
    sk, snew, swin, scmp, c31s = tabs
    bd, t_new, _ = q_pk.shape
    n_pages = page_table.shape[1]
    past_len = n_pages * PAGE_SIZE
    ncp = kcp.shape[1]
    n_cmp = past_len // CMP_STRIDE - 1
    n_sel = -(-(past_len + t_new) // SEL_BLOCK)
    n_sel_rows = -(-n_sel // SUBLANES) * SUBLANES
    assert t_new == 4 and state_win.shape[1] == WINDOW and past_len >= 2 * PAGE_SIZE
    n_cols = N_KV * GROUP * t_new
    qc = q_pk.reshape(bd, t_new, N_HEADS, LANES).transpose(0, 2, 1, 3).reshape(bd, n_cols, LANES)
    qc = jnp.pad(qc, ((0, 0), (0, LANES - n_cols), (0, 0)))
    gc = gates[:, :, :3 * N_HEADS].reshape(bd, t_new, N_HEADS, 3).transpose(0, 2, 1, 3).reshape(bd, n_cols, 3)
    gc = jnp.pad(gc, ((0, 0), (0, LANES - n_cols), (0, SUBLANES - 3)))
    pad_rows = lambda a: jnp.pad(a, ((0, 0), (0, 16 - t_new), (0, 0)))
    ovt = _overlap_t(n_sel_rows, n_sel, ncp, n_cmp)
    c = np.arange(LANES)
    same = (c[:, None] // (GROUP * t_new) == c[None, :] // (GROUP * t_new)) & (c[:, None] % t_new == c[None, :] % t_new)
    rmat = jnp.asarray(same & (c[:, None] < n_cols) & (c[None, :] < n_cols), dtype=bf16)

    def page_spec(k):
        return pl.BlockSpec((1, PAGE_SIZE, KV_W), lambda i, pt: (pt[i, k], 0, 0))

    per_b = lambda shape: pl.BlockSpec((1,) + shape, lambda i, pt: (i, 0, 0))
    res = lambda a: pl.BlockSpec(a.shape, lambda i, pt: (0,) * a.ndim)
    grid_spec = pltpu.PrefetchScalarGridSpec(
        num_scalar_prefetch=1,
        grid=(bd,),
        in_specs=[page_spec(k) for k in range(n_pages)] + [
            per_b((LANES, LANES)), per_b((LANES, SUBLANES)), per_b((ncp, KV_W)), per_b((16, KV_W)),
            per_b((WINDOW, KV_W)), per_b((16, KV_W)),
            res(ovt), res(rmat), res(sk), res(snew), res(swin), res(scmp), res(c31s)],
        out_specs=per_b((LANES, LANES)),
        scratch_shapes=[pltpu.VMEM((past_len, KV_W), bf16)],
    )
    o = pl.pallas_call(
        functools.partial(_attn_sample_kernel, n_pages=n_pages, n_sel=n_sel, past_len=past_len),
        grid_spec=grid_spec,
        out_shape=jax.ShapeDtypeStruct((bd, LANES, LANES), f32),
        compiler_params=_cparams(("parallel",)),
        name="attn_sample",
    )(page_table, *([cache_slc] * n_pages), qc, gc, kcp, pad_rows(ks_new), state_win, pad_rows(kw_new),
      ovt, rmat, sk, snew, swin, scmp, c31s)
    o = o[:, :n_cols].reshape(bd, N_KV, GROUP, t_new, N_KV, HEAD_DIM)
    o = jnp.stack([o[:, g, :, :, g, :] for g in range(N_KV)], axis=1)
    return o.transpose(0, 3, 1, 2, 4).reshape(bd, t_new, ATTN_DIM)


def _merge_kernel(x_ref, a_ref, u0_ref, u1_ref, u2_ref, cb_ref, mg_ref, cw_ref, cbias_ref,
                  wau_ref, wcu_ref, wo_ref, g2_ref, x2_ref, h2_ref, *, d_model):
    y = cbias_ref[...] + cw_ref[0:1, :] * u2_ref[...]
    y = y + cw_ref[1:2, :] * u1_ref[...]
    y = y + cw_ref[2:3, :] * u0_ref[...]
    up_a = _dot(a_ref[...].astype(bf16), wau_ref[...])
    up_c = _dot((cb_ref[...] * y).astype(bf16), wcu_ref[...])
    mixed = mg_ref[:, :d_model] * up_a + mg_ref[:, d_model:] * up_c
    x2 = x_ref[...] + _dot(mixed.astype(bf16), wo_ref[...])
    x2_ref[...] = x2
    h2_ref[...] = _rmsnorm(x2, g2_ref[...]).astype(bf16)


def _merge(x, attn, u0, u1, u2, cb, mg, conv_w, conv_b, w_attn_up, w_conv_up, w_out, norm2_g):
    n, d = x.shape
    tm = ROW_TILE
    row = lambda w: pl.BlockSpec((tm, w), lambda i: (i, 0))
    cw = jnp.pad(conv_w, ((0, SUBLANES - CONV_W), (0, 0)))
    return pl.pallas_call(
        functools.partial(_merge_kernel, d_model=d),
        grid=(n // tm,),
        in_specs=[row(d), row(ATTN_DIM), row(CONV_DIM), row(CONV_DIM), row(CONV_DIM), row(CONV_DIM), row(2 * d),
                  _resident(cw.shape), _resident((1, CONV_DIM)), _resident(w_attn_up.shape),
                  _resident(w_conv_up.shape), _resident(w_out.shape), _resident((1, d))],
        out_specs=[row(d), row(d)],
        out_shape=[jax.ShapeDtypeStruct((n, d), f32), jax.ShapeDtypeStruct((n, d), bf16)],
        compiler_params=_cparams(("parallel",)),
        name="merge",
    )(x, attn, u0, u1, u2, cb, mg, cw, conv_b.reshape(1, CONV_DIM), w_attn_up.astype(bf16),
      w_conv_up.astype(bf16), w_out.astype(bf16), norm2_g.reshape(1, d))


def _top_rows_sorted(s, k):
    rows = lax.broadcasted_iota(i32, s.shape, 0).astype(f32)
    vals, idxs = [], []
    for _ in range(k):
        m = jnp.max(s, axis=0, keepdims=True)
        first = jnp.min(jnp.where(s == m, rows, 1e9), axis=0, keepdims=True)
        vals.append(m)
        idxs.append(first)
        s = jnp.where(rows == first, -jnp.inf, s)
    return jnp.concatenate(vals, axis=0), jnp.concatenate(idxs, axis=0)


def _peer_route_kernel(h_ref, wq_ref, k1_ref, k2_ref, a_ref, b_ref, g_ref):
    qp = _dot(h_ref[...], wq_ref[...]).astype(bf16)
    key_dim = 2 * PEER_HALF
    a_rows, b_rows, g_rows = [], [], []
    for h in range(PEER_HEADS):
        q1 = qp[:, h * key_dim:h * key_dim + PEER_HALF]
        q2 = qp[:, h * key_dim + PEER_HALF:(h + 1) * key_dim]
        v1, i1 = _top_rows_sorted(_dot_nt(k1_ref[...], q1), PEER_TOPK)
        v2, i2 = _top_rows_sorted(_dot_nt(k2_ref[...], q2), PEER_TOPK)
        cand = jnp.concatenate([v1[r:r + 1, :] + v2 for r in range(PEER_TOPK)], axis=0)
        sc, ci = _top_rows_sorted(cand, PEER_TOPK)
        c1 = jnp.floor(ci * (1.0 / PEER_TOPK))
        c2 = ci - c1 * PEER_TOPK
        e1 = jnp.zeros_like(ci)
        e2 = jnp.zeros_like(ci)
        for r in range(PEER_TOPK):
            e1 = jnp.where(c1 == r, i1[r:r + 1, :], e1)
            e2 = jnp.where(c2 == r, i2[r:r + 1, :], e2)
        e = jnp.exp(sc - sc[0:1, :])
        a_rows.append(e1)
        b_rows.append(e2)
        g_rows.append(e / jnp.sum(e, axis=0, keepdims=True))
    tn = h_ref.shape[0]
    for src, dst in ((a_rows, a_ref), (b_rows, b_ref), (g_rows, g_ref)):
        full = jnp.concatenate(src, axis=0)
        for c in range(tn // LANES):
            dst[c * LANES:(c + 1) * LANES, :] = full[:, c * LANES:(c + 1) * LANES].T


def _peer_route(h2, wq, k1, k2):
    n, d = h2.shape
    tn = ROW_TILE
    assert PEER_HEADS * PEER_TOPK == LANES
    row = lambda w: pl.BlockSpec((tn, w), lambda i: (i, 0))
    return pl.pallas_call(
        _peer_route_kernel,
        grid=(n // tn,),
        in_specs=[row(d), _resident(wq.shape), _resident(k1.shape), _resident(k2.shape)],
        out_specs=[row(LANES)] * 3,
        out_shape=[jax.ShapeDtypeStruct((n, LANES), f32)] * 3,
        compiler_params=_cparams(("parallel",)),
        name="peer_route",
    )(h2, wq.astype(bf16), k1.astype(bf16), k2.astype(bf16))


def _peer_expert_kernel(h_ref, a_ref, b_ref, w_ref, x_ref, u_ref, v_ref, out_ref, g_ref, acc_ref, *, pitch):
    e_step = pl.program_id(1)
    tn = h_ref.shape[0]
    keys_per_chunk = EXPERT_CHUNK // PEER_NKEYS

    @pl.when(e_step == 0)
    def _build_gate_matrix():
        key_row = lax.broadcasted_iota(i32, (PEER_NKEYS, LANES), 0).astype(f32)

        def token(n, c):
            a = jnp.broadcast_to(a_ref[pl.ds(n, 1), :], (PEER_NKEYS, LANES))
            b = jnp.broadcast_to(b_ref[pl.ds(n, 1), :], (PEER_NKEYS, LANES))
            w = jnp.broadcast_to(w_ref[pl.ds(n, 1), :], (PEER_NKEYS, LANES))
            pa = (a == key_row).astype(bf16)
            wb = jnp.where(b == key_row, w, 0.0)
            w_hi, w_lo = _split_bf16(wb)
            g = _dot_nt(jnp.concatenate([pa, pa], axis=1), jnp.concatenate([w_hi, w_lo], axis=1))
            g_ref[pl.ds(n, PEER_NKEYS, stride=pitch), :] = g
            return c

        lax.fori_loop(0, tn, token, 0)
        acc_ref[...] = jnp.zeros_like(acc_ref)

    s = _dot_nt(h_ref[...], u_ref[...])
    gates = jnp.concatenate(
        [g_ref[pl.ds(pl.multiple_of((e_step * keys_per_chunk + j) * pitch, SUBLANES), tn), :]
         for j in range(keys_per_chunk)], axis=1)
    act = 0.5 * s * (1.0 + lax.erf(s * math.sqrt(0.5)))
    acc_ref[...] += _dot((gates * act).astype(bf16), v_ref[...])

    @pl.when(e_step == pl.num_programs(1) - 1)
    def _finish():
        out_ref[...] = x_ref[...] + acc_ref[...]


def _peer_expert(h2, a_idx, b_idx, gate_w, x2, u_tab, v_tab):
    n, d = h2.shape
    n_exp = u_tab.shape[0]
    tn = ROW_TILE
    pitch = tn + G_PITCH_PAD
    assert n_exp == PEER_NKEYS * PEER_NKEYS and n_exp % EXPERT_CHUNK == 0
    row = lambda w: pl.BlockSpec((tn, w), lambda i, e: (i, 0))
    tab = pl.BlockSpec((EXPERT_CHUNK, d), lambda i, e: (e, 0))
    return pl.pallas_call(
        functools.partial(_peer_expert_kernel, pitch=pitch),
        grid=(n // tn, n_exp // EXPERT_CHUNK),
        in_specs=[row(d), row(LANES), row(LANES), row(LANES), row(d), tab, tab],
        out_specs=row(d),
        out_shape=jax.ShapeDtypeStruct((n, d), f32),
        scratch_shapes=[pltpu.VMEM((PEER_NKEYS * pitch, LANES), f32), pltpu.VMEM((tn, d), f32)],
        compiler_params=_cparams(("parallel", "arbitrary")),
        name="peer_expert",
    )(h2, a_idx, b_idx, gate_w, x2, u_tab.astype(bf16), v_tab.astype(bf16))


def _ple_kernel(x_ref, p_ref, pg_ref, wg_ref, wp_ref, fg_ref, y_ref):
    x = x_ref[...]
    gate = jax.nn.sigmoid(_dot(_rmsnorm(x, pg_ref[...]).astype(bf16), wg_ref[...]))
    x = x + gate * _dot(p_ref[...].astype(bf16), wp_ref[...])
    y_ref[...] = _rmsnorm(x, fg_ref[...])


def _ple(x3, p, ple_g, w_ple_gate, w_ple_proj, final_g):
    n, d = x3.shape
    tm = ROW_TILE
    row = lambda w: pl.BlockSpec((tm, w), lambda i: (i, 0))
    return pl.pallas_call(
        _ple_kernel,
        grid=(n // tm,),
        in_specs=[row(d), row(p.shape[1]), _resident((1, d)), _resident(w_ple_gate.shape),
                  _resident(w_ple_proj.shape), _resident((1, d))],
        out_specs=row(d),
        out_shape=jax.ShapeDtypeStruct((n, d), f32),
        compiler_params=_cparams(("parallel",)),
        name="ple_final",
    )(x3, p, ple_g.reshape(1, d), w_ple_gate.astype(bf16), w_ple_proj.astype(bf16), final_g.reshape(1, d))


def kernel(x_prompt, x_sample, p_prompt, p_sample, cache_cmp_kv, cache_slc_kv, page_table, state_win_kv, state_conv, norm1_g, w_in, w_cmp, cmp_pos, conv_w, conv_b, w_attn_up, w_conv_up, w_out, norm2_g, peer_wq, peer_k1, peer_k2, peer_u, peer_v, ple_g, w_ple_gate, w_ple_proj, rel_bias, final_g):
    depth = norm1_g.shape[0]
    assert depth == 1, "single-layer trunk"
    b, t, d = x_prompt.shape
    bd, t_new, _ = x_sample.shape
    n_pool = cache_cmp_kv.shape[1]
    n_pages = page_table.shape[1]
    past_len = n_pages * PAGE_SIZE
    n_p, n_s = b * t, bd * t_new
    kv_shape = lambda lead: lead + (2, N_KV, HEAD_DIM)

    x_all = jnp.concatenate([x_prompt.reshape(n_p, d), x_sample.reshape(n_s, d)], axis=0)
    w_pack = _pack_w_in(w_in[0], d)
    q_pk, kvc, kvs, kvw, kvs_bf, kvw_bf, gates, cin, cbg, mg = _in_proj(x_all, norm1_g[0], w_pack)
    split = lambda a: (a[:n_p].reshape(b, t, -1), a[n_p:].reshape(bd, t_new, -1))

    w_big, pos_rows = _pack_w_cmp(w_cmp[0], cmp_pos[0])
    kvc_p, kvc_s = split(kvc)
    kcp_p = _compress_prompt(kvc_p, w_big, pos_rows)
    kcp_s = _compress_sample(cache_cmp_kv[0].reshape(n_pool, PAGE_SIZE, KV_W), page_table, w_big, pos_rows)
    n_cmp_s = past_len // CMP_STRIDE - 1
    wt, ct, c31, *sample_tabs = _tables(rel_bias, past_len, kcp_s.shape[1], n_cmp_s)

    q_p, q_s = split(q_pk)
    g_p, g_s = split(gates)
    kvs_p, kvs_s = split(kvs)
    kvw_p, kvw_s = split(kvw)
    ksb_p, _ = split(kvs_bf)
    kwb_p, _ = split(kvw_bf)
    attn_p = _attn_prompt(q_p, g_p, kcp_p, ksb_p, kwb_p, wt, ct, c31)
    win_buf = state_win_kv[0].reshape(bd, -1, KV_W)
    attn_s = _attn_sample(q_s, g_s, kcp_s, cache_slc_kv[0].reshape(n_pool, PAGE_SIZE, KV_W), page_table,
                          kvs_s, win_buf, kvw_s, sample_tabs)
    attn = jnp.concatenate([attn_p.reshape(n_p, ATTN_DIM), attn_s.reshape(n_s, ATTN_DIM)], axis=0)

    cin_p, cin_s = split(cin)
    buf_p = jnp.zeros((b, CONV_W - 1, CONV_DIM), f32)
    buf_s = state_conv[0]

    def shifted(u, buf, k):
        return jnp.concatenate([buf[:, CONV_W - 1 - k:], u[:, :u.shape[1] - k]], axis=1).reshape(-1, CONV_DIM)

    u1 = jnp.concatenate([shifted(cin_p, buf_p, 1), shifted(cin_s, buf_s, 1)], axis=0)
    u2 = jnp.concatenate([shifted(cin_p, buf_p, 2), shifted(cin_s, buf_s, 2)], axis=0)
    x2, h2 = _merge(x_all, attn, cin, u1, u2, cbg, mg, conv_w[0], conv_b[0], w_attn_up[0], w_conv_up[0],
                    w_out[0], norm2_g[0])

    a_idx, b_idx, gate_w = _peer_route(h2, peer_wq[0], peer_k1[0], peer_k2[0])
    x3 = _peer_expert(h2, a_idx, b_idx, gate_w, x2, peer_u[0], peer_v[0])

    p_all = jnp.concatenate([p_prompt[0].reshape(n_p, -1), p_sample[0].reshape(n_s, -1)], axis=0)
    y = _ple(x3, p_all, ple_g[0], w_ple_gate[0], w_ple_proj[0], final_g)

    conv_tail = lambda u, buf: jnp.concatenate([buf, u], axis=1)[:, -(CONV_W - 1):]
    n_win = min(WINDOW, t)
    new_win_s = jnp.concatenate([win_buf, kvw_s], axis=1)[:, t_new:]
    return (
        y[:n_p].reshape(b, t, d),
        y[n_p:].reshape(bd, t_new, d),
        kvc_p.reshape(kv_shape((1, b, t))),
        kvc_s.reshape(kv_shape((1, bd, t_new))),
        kvs_p.reshape(kv_shape((1, b, t))),
        kvs_s.reshape(kv_shape((1, bd, t_new))),
        kvw_p[:, t - n_win:].reshape(kv_shape((1, b, n_win))),
        new_win_s.reshape(kv_shape((1, bd, win_buf.shape[1]))),
        conv_tail(cin_p, buf_p)[None],
        conv_tail(cin_s, buf_s)[None],
    )
```

```python
import functools
import math

import jax
import jax.numpy as jnp
import numpy as np
from jax import lax
from jax.experimental import pallas as pl
from jax.experimental.pallas import tpu as pltpu

f32 = jnp.float32
bf16 = jnp.bfloat16
i32 = jnp.int32

N_HEADS = 8
N_KV = 2
GROUP = N_HEADS // N_KV
HEAD_DIM = 64
ATTN_DIM = N_HEADS * HEAD_DIM
KV_W = 2 * N_KV * HEAD_DIM
CMP_BLOCK = 32
CMP_STRIDE = 16
SEL_BLOCK = 64
SEL_SHIFT = 6
SEL_TOP = 16
WINDOW = 512
N_BUCKETS = 32
MAX_EXACT = N_BUCKETS // 2
MAX_DISTANCE = 128
CONV_DIM = 512
CONV_W = 3
PEER_HEADS = 8
PEER_NKEYS = 128
PEER_HALF = 128
PEER_TOPK = 16
PAGE_SIZE = 128
EPS = 1e-6
NEG = -1e30
FORCE = 1e4

LANES = 128
SUBLANES = 8
VMEM_LIMIT_BYTES = 56 * 1024 * 1024

Q_TILE = 128
SAMPLE_COLS = 32
FAR_CHUNK = 512
CMP_PAD = 16
CMP_WIN = 24
ROW_TILE = 256
PEER_TILE = 512
EXPERT_CHUNK = 1024
TOKEN_UNROLL = 16
G_PITCH_PAD = 8


def _cparams(sem):
    return pltpu.CompilerParams(dimension_semantics=sem, vmem_limit_bytes=VMEM_LIMIT_BYTES)


def _dot(a, b):
    return jnp.dot(a, b, preferred_element_type=f32)


def _dot_nt(a, b):
    return lax.dot_general(a, b, (((1,), (1,)), ((), ())), preferred_element_type=f32)


def _dot_tn(a, b):
    return lax.dot_general(a, b, (((0,), (0,)), ((), ())), preferred_element_type=f32)


def _split_bf16(x):
    hi = x.astype(bf16)
    lo = (x - hi.astype(f32)).astype(bf16)
    return hi, lo


def _resident(shape):
    nd = len(shape)
    return pl.BlockSpec(shape, lambda *_: (0,) * nd)


def _rmsnorm(x, g):
    ms = jnp.mean(x * x, axis=-1, keepdims=True)
    return x * lax.rsqrt(ms + EPS) * g


def _in_proj_kernel(x_ref, g_ref, w_ref, q_ref, kc_ref, ks_ref, kw_ref, ksb_ref, kwb_ref,
                    gn_ref, cin_ref, cb_ref, mg_ref, *, d_model):
    qw = N_HEADS * LANES
    h = _rmsnorm(x_ref[...], g_ref[...]).astype(bf16)
    o = 0
    q_ref[...] = (_dot(h, w_ref[:, o:o + qw]) * (HEAD_DIM ** -0.5)).astype(bf16)
    o += qw
    kc_ref[...] = _dot(h, w_ref[:, o:o + KV_W])
    o += KV_W
    ks = _dot(h, w_ref[:, o:o + KV_W])
    ks_ref[...] = ks
    ksb_ref[...] = ks.astype(bf16)
    o += KV_W
    kw = _dot(h, w_ref[:, o:o + KV_W])
    kw_ref[...] = kw
    kwb_ref[...] = kw.astype(bf16)
    o += KV_W
    gn_ref[...] = jax.nn.sigmoid(_dot(h, w_ref[:, o:o + LANES]))
    o += LANES
    cv = _dot(h, w_ref[:, o:o + CONV_DIM])
    o += CONV_DIM
    cb_ref[...] = _dot(h, w_ref[:, o:o + CONV_DIM])
    o += CONV_DIM
    cin_ref[...] = _dot(h, w_ref[:, o:o + CONV_DIM]) * cv
    o += CONV_DIM
    mg_ref[...] = jax.nn.sigmoid(_dot(h, w_ref[:, o:o + 2 * d_model]))


def _pack_w_in(w_in, d_model):
    sizes = (ATTN_DIM, KV_W, KV_W, KV_W, 3 * N_HEADS, CONV_DIM, CONV_DIM, CONV_DIM, 2 * d_model)
    parts, s = [], 0
    for n in sizes:
        parts.append(w_in[:, s:s + n])
        s += n
    wq = parts[0].reshape(d_model, N_KV, GROUP, HEAD_DIM)
    z = jnp.zeros_like(wq)
    slabs = [jnp.concatenate([wq[:, 0], z[:, 0]], axis=-1), jnp.concatenate([z[:, 1], wq[:, 1]], axis=-1)]
    wq = jnp.stack(slabs, axis=1).reshape(d_model, N_HEADS * LANES)
    gn = jnp.pad(parts[4], ((0, 0), (0, LANES - 3 * N_HEADS)))
    return jnp.concatenate([wq, parts[1], parts[2], parts[3], gn] + parts[5:], axis=1).astype(bf16)


def _in_proj(x, norm_g, w_pack):
    n, d = x.shape
    tm = ROW_TILE
    assert n % tm == 0
    row = lambda w: pl.BlockSpec((tm, w), lambda i: (i, 0))
    widths = (N_HEADS * LANES, KV_W, KV_W, KV_W, KV_W, KV_W, LANES, CONV_DIM, CONV_DIM, 2 * d)
    dtypes = (bf16, f32, f32, f32, bf16, bf16, f32, f32, f32, f32)
    return pl.pallas_call(
        functools.partial(_in_proj_kernel, d_model=d),
        grid=(n // tm,),
        in_specs=[row(d), _resident((1, d)), _resident(w_pack.shape)],
        out_specs=[row(w) for w in widths],
        out_shape=[jax.ShapeDtypeStruct((n, w), t) for w, t in zip(widths, dtypes)],
        compiler_params=_cparams(("parallel",)),
        name="in_proj",
    )(x, norm_g.reshape(1, d), w_pack)


def _bias_of_dist(dist, valid, rb_ref, head):
    n = jnp.maximum(dist, 0)
    nf = jnp.maximum(n, 1).astype(f32)
    large = MAX_EXACT + (jnp.log(nf / MAX_EXACT) / math.log(MAX_DISTANCE / MAX_EXACT)
                         * (N_BUCKETS - MAX_EXACT)).astype(i32)
    large = jnp.minimum(large, N_BUCKETS - 1)
    bucket = jnp.where(n < MAX_EXACT, n, large)
    out = jnp.zeros(dist.shape, f32)
    for b in range(N_BUCKETS):
        out = jnp.where(bucket == b, rb_ref[b, head], out)
    return jnp.where(valid, out, NEG)


def _tables_kernel(rb_ref, wt_ref, ct_ref, c31_ref, scmp_ref, sk_ref, snew_ref, swin_ref, c31s_ref,
                   *, past_len, ncp_s, n_cmp_s):
    ik = lax.broadcasted_iota(i32, (Q_TILE, Q_TILE), 0)
    iq = lax.broadcasted_iota(i32, (Q_TILE, Q_TILE), 1)
    n_w = WINDOW // Q_TILE + 1
    for g in range(N_KV):
        for hh in range(GROUP):
            head = g * GROUP + hh
            cols = slice(hh * Q_TILE, (hh + 1) * Q_TILE)
            for w in range(n_w):
                dist = Q_TILE * (n_w - 1 - w) + iq - ik
                wt_ref[g, w, :, cols] = _bias_of_dist(dist, (dist >= 0) & (dist < WINDOW), rb_ref, head)
            mm = lax.broadcasted_iota(i32, (CMP_WIN, Q_TILE), 0)
            jq = lax.broadcasted_iota(i32, (CMP_WIN, Q_TILE), 1)
            dist = jq - CMP_STRIDE * (mm - CMP_PAD) - (CMP_BLOCK - 1)
            ct_ref[g, :, cols] = _bias_of_dist(dist, dist >= 0, rb_ref, head)
            c31_ref[g, :, cols] = jnp.full((SUBLANES, Q_TILE), rb_ref[N_BUCKETS - 1, head], f32)
    def sample_table(shape, q_axis, dist_fn, valid_fn, out_ref):
        c = lax.broadcasted_iota(i32, shape, q_axis)
        r = lax.broadcasted_iota(i32, shape, 1 - q_axis)
        t = c & 3
        hd = jnp.minimum(lax.shift_right_logical(c, 2), N_HEADS - 1)
        dist = dist_fn(r, t)
        valid = valid_fn(r, t, dist)
        acc = jnp.zeros(shape, f32)
        for head in range(N_HEADS):
            acc = jnp.where(hd == head, _bias_of_dist(dist, valid, rb_ref, head), acc)
        out_ref[...] = acc

    sample_table((ncp_s, LANES), 1, lambda r, t: past_len + t - CMP_STRIDE * (r - CMP_PAD) - (CMP_BLOCK - 1),
                 lambda r, t, d: (r >= CMP_PAD) & (r < CMP_PAD + n_cmp_s) & (d >= 0), scmp_ref)
    sample_table((SAMPLE_COLS, PAGE_SIZE), 0, lambda r, t: PAGE_SIZE + t - r, lambda r, t, d: d >= 0, sk_ref)
    sample_table((SAMPLE_COLS, LANES), 0, lambda r, t: t - r, lambda r, t, d: (d >= 0) & (r < 4), snew_ref)
    sample_table((SAMPLE_COLS, WINDOW), 0, lambda r, t: WINDOW + t - r, lambda r, t, d: d < WINDOW, swin_ref)
    sample_table((SAMPLE_COLS, LANES), 0, lambda r, t: jnp.full_like(r, MAX_DISTANCE), lambda r, t, d: d > 0, c31s_ref)


def _tables(rel_bias, past_len, ncp_s, n_cmp_s):
    n_w = WINDOW // Q_TILE + 1
    cols = GROUP * Q_TILE
    shapes = [
        (N_KV, n_w, Q_TILE, cols), (N_KV, CMP_WIN, cols), (N_KV, SUBLANES, cols),
        (ncp_s, LANES), (SAMPLE_COLS, PAGE_SIZE), (SAMPLE_COLS, LANES), (SAMPLE_COLS, WINDOW), (SAMPLE_COLS, LANES),
    ]
    return pl.pallas_call(
        functools.partial(_tables_kernel, past_len=past_len, ncp_s=ncp_s, n_cmp_s=n_cmp_s),
        in_specs=[pl.BlockSpec(memory_space=pltpu.SMEM)],
        out_shape=[jax.ShapeDtypeStruct(s, f32) for s in shapes],
        compiler_params=pltpu.CompilerParams(vmem_limit_bytes=VMEM_LIMIT_BYTES),
        name="bias_tables",
    )(rel_bias)


def _ncp(n_chunks):
    return -(-(n_chunks + CMP_PAD + SUBLANES) // LANES) * LANES


def _compress_core(chunk_rows, w_ref, pos_ref, out_ref, n_chunks):
    half = KV_W // 2
    out_ref[...] = jnp.zeros(out_ref.shape, out_ref.dtype)
    for c in range(2):
        fs = None
        pb = None
        for p in range(CMP_STRIDE // 2):
            x = jnp.concatenate([chunk_rows(2 * p, c), chunk_rows(2 * p + 1, c)], axis=1).astype(bf16)
            lanes = [slice((2 * p + k) * KV_W + c * half, (2 * p + k) * KV_W + (c + 1) * half) for k in range(2)]
            px = jnp.concatenate([pos_ref[:, lanes[0]], pos_ref[:, lanes[1]]], axis=1)
            fs = _dot(x, w_ref[p, c]) if fs is None else fs + _dot(x, w_ref[p, c])
            pb = _dot(px, w_ref[p, c]) if pb is None else pb + _dot(px, w_ref[p, c])
        bias = pb[0:1, :half] + pb[1:2, half:]
        nxt = pltpu.roll(fs[:, half:], n_chunks - 1, axis=0)
        kc = fs[:, :half] + nxt + bias
        r = lax.broadcasted_iota(i32, kc.shape, 0)
        kc = jnp.where(r < n_chunks - 1, kc, 0.0)
        out_ref[0, CMP_PAD:CMP_PAD + n_chunks, c * half:(c + 1) * half] = kc.astype(out_ref.dtype)


def _compress_prompt_kernel(x_ref, w_ref, pos_ref, out_ref, *, n_chunks):
    half = KV_W // 2
    rows = lambda l, c: x_ref[0, :, l * KV_W + c * half:l * KV_W + (c + 1) * half]
    _compress_core(rows, w_ref, pos_ref, out_ref, n_chunks)


def _compress_paged_kernel(*refs, n_pages, n_chunks):
    page_refs = refs[1:1 + n_pages]
    perm_ref, w_ref, pos_ref, out_ref, xp_ref = refs[1 + n_pages:]
    half = KV_W // 2
    per_page = PAGE_SIZE // CMP_STRIDE
    for k in range(n_pages):
        xp_ref[k] = _dot_nt(perm_ref[...], page_refs[k][0].astype(bf16))

    def rows(l, c):
        x = xp_ref[:, l * per_page:(l + 1) * per_page, c * half:(c + 1) * half]
        return x.reshape(n_pages * per_page, half)

    _compress_core(rows, w_ref, pos_ref, out_ref, n_chunks)


def _pack_w_cmp(w_cmp, cmp_pos):
    w = w_cmp.reshape(2, CMP_STRIDE // 2, 2, 2, HEAD_DIM, HEAD_DIM)
    w4 = jnp.einsum("hplcde,gG->pclgdhGe", w, jnp.eye(N_KV, dtype=f32))
    w4 = w4.reshape(CMP_STRIDE // 2, 2, KV_W, KV_W)

    def pos_row(p):
        return jnp.broadcast_to(p[:, :, None, :], (CMP_STRIDE, 2, N_KV, HEAD_DIM)).reshape(1, -1)

    pos = jnp.concatenate([pos_row(cmp_pos[:CMP_STRIDE]), pos_row(cmp_pos[CMP_STRIDE:]),
                           jnp.zeros((SUBLANES - 2, CMP_STRIDE * KV_W), f32)], axis=0)
    return w4.astype(bf16), pos.astype(bf16)


def _compress_prompt(kv_cmp, w4, pos):
    b, t, _ = kv_cmp.shape
    n_chunks = t // CMP_STRIDE
    cw = CMP_STRIDE * KV_W
    chunks = kv_cmp.reshape(b, n_chunks, cw)
    ncp = _ncp(n_chunks)
    return pl.pallas_call(
        functools.partial(_compress_prompt_kernel, n_chunks=n_chunks),
        grid=(b,),
        in_specs=[pl.BlockSpec((1, n_chunks, cw), lambda i: (i, 0, 0)), _resident(w4.shape), _resident(pos.shape)],
        out_specs=pl.BlockSpec((1, ncp, KV_W), lambda i: (i, 0, 0)),
        out_shape=jax.ShapeDtypeStruct((b, ncp, KV_W), bf16),
        compiler_params=_cparams(("parallel",)),
        name="compress_prompt",
    )(chunks, w4, pos)


def _pages_t(cache):
    n_pool = cache.shape[0]
    return jnp.transpose(cache, (0, 2, 3, 4, 1)).reshape(n_pool, KV_W, cache.shape[1])


def _compress_sample(pages_t, page_table, w4, pos):
    bd, n_pages = page_table.shape
    per_page = PAGE_SIZE // CMP_STRIDE
    n_chunks = n_pages * per_page
    ncp = _ncp(n_chunks)
    tok = np.arange(PAGE_SIZE)
    perm = jnp.asarray(tok[None, :] == (tok[:, None] % per_page) * CMP_STRIDE + tok[:, None] // per_page, dtype=bf16)

    def page_spec(k):
        return pl.BlockSpec((1, KV_W, PAGE_SIZE), lambda i, pt: (pt[i, k], 0, 0))

    res = lambda a: pl.BlockSpec(a.shape, lambda i, pt: (0,) * a.ndim)
    grid_spec = pltpu.PrefetchScalarGridSpec(
        num_scalar_prefetch=1,
        grid=(bd,),
        in_specs=[page_spec(k) for k in range(n_pages)] + [res(perm), res(w4), res(pos)],
        out_specs=pl.BlockSpec((1, ncp, KV_W), lambda i, pt: (i, 0, 0)),
        scratch_shapes=[pltpu.VMEM((n_pages, PAGE_SIZE, KV_W), f32)],
    )
    return pl.pallas_call(
        functools.partial(_compress_paged_kernel, n_pages=n_pages, n_chunks=n_chunks),
        grid_spec=grid_spec,
        out_shape=jax.ShapeDtypeStruct((bd, ncp, KV_W), bf16),
        compiler_params=_cparams(("parallel",)),
        name="compress_sample",
    )(page_table, *([pages_t] * n_pages), perm, w4, pos)


def _overlap_t(n_sel_rows, n_sel, ncp, n_cmp):
    j = np.arange(n_sel_rows)[:, None]
    n = np.arange(ncp)[None, :] - CMP_PAD
    hit = (n * CMP_STRIDE < (j + 1) * SEL_BLOCK) & (n * CMP_STRIDE + CMP_BLOCK > j * SEL_BLOCK)
    hit &= (n >= 0) & (n < n_cmp) & (j < n_sel)
    return jnp.asarray(hit, dtype=bf16)


def _masked_softmax_cols(s):
    m = jnp.max(s, axis=0, keepdims=True)
    e = jnp.exp(s - m)
    z = jnp.sum(e, axis=0, keepdims=True)
    return jnp.where(s > 0.5 * NEG, e / z, 0.0)


def _top_rows(imp, k):
    rows = lax.broadcasted_iota(i32, imp.shape, 0).astype(f32)
    sel = jnp.zeros(imp.shape, jnp.bool_)
    v = imp
    for _ in range(k):
        m = jnp.max(v, axis=0, keepdims=True)
        first = jnp.min(jnp.where(v == m, rows, 1e9), axis=0, keepdims=True)
        pick = rows == first
        sel = sel | pick
        v = jnp.where(pick, -jnp.inf, v)
    return sel


def _flash_step(carry, s, v_t):
    m, l, acc = carry
    m_new = jnp.maximum(m, jnp.max(s, axis=0, keepdims=True))
    alpha = jnp.exp(m - m_new)
    p = jnp.exp(s - m_new)
    l = alpha * l + jnp.sum(p, axis=0, keepdims=True)
    acc = alpha * acc + _dot(v_t, p.astype(bf16))
    return m_new, l, acc


def _attn_prompt_kernel(q_ref, gn_ref, kc_ref, vct_ref, ks_ref, vst_ref, kw_ref, vwt_ref, ovt_ref,
                        wt_ref, ct_ref, c31_ref, out_ref, cadd_ref, sb0_ref, sbc_ref,
                        *, n_sel, ncp):
    qt = pl.program_id(1)
    cols = GROUP * Q_TILE
    n_w = WINDOW // Q_TILE + 1
    gates_t = gn_ref[0].T
    n_far = jnp.maximum(qt - 1, 0) // (FAR_CHUNK // Q_TILE)
    blocks_per_chunk = FAR_CHUNK // SEL_BLOCK
    blocks_per_tile = Q_TILE // SEL_BLOCK

    for g in range(N_KV):
        qp = jnp.concatenate(
            [q_ref[0, :, (g * GROUP + hh) * LANES:(g * GROUP + hh + 1) * LANES] for hh in range(GROUP)], axis=0)
        c31 = c31_ref[g, 0:1, :]

        ri = lax.broadcasted_iota(i32, (ncp, cols), 0)
        cadd_ref[...] = jnp.where(ri < SUBLANES * qt, c31, NEG)
        cadd_ref[pl.ds(pl.multiple_of(SUBLANES * qt, SUBLANES), CMP_WIN), :] = ct_ref[g]
        lc = _dot_nt(kc_ref[0], qp) + cadd_ref[...]
        lc = jnp.where(ri >= CMP_PAD, lc, NEG)
        pc = _masked_softmax_cols(lc)
        o_c = _dot(vct_ref[0], pc.astype(bf16))
        psum = pc[:, 0:Q_TILE]
        for hh in range(1, GROUP):
            psum = psum + pc[:, hh * Q_TILE:(hh + 1) * Q_TILE]
        p_hi, p_lo = _split_bf16(psum)
        imp = _dot(ovt_ref[...], p_hi) + _dot(ovt_ref[...], p_lo)

        blk = lax.broadcasted_iota(i32, (n_sel, Q_TILE), 0)
        qpos = qt * Q_TILE + lax.broadcasted_iota(i32, (n_sel, Q_TILE), 1)
        cur = lax.shift_right_logical(qpos, SEL_SHIFT)
        forced = (blk == 0) | (blk == cur) | (blk == cur - 1)
        future = blk * SEL_BLOCK > qpos
        imp = jnp.where(forced, FORCE, jnp.where(future, -FORCE, imp))
        sel = _top_rows(imp, min(SEL_TOP, n_sel))
        sb = jnp.where(sel, 0.0, NEG)
        sb = jnp.concatenate([sb] * GROUP, axis=1)
        sb0_ref[...] = sb
        sbc_ref[...] = sb + c31

        def far_step(i, carry):
            ks = pl.multiple_of(i * FAR_CHUNK, FAR_CHUNK)
            s = _dot_nt(ks_ref[0, pl.ds(ks, FAR_CHUNK), :], qp)
            add = sbc_ref[pl.ds(pl.multiple_of(i * blocks_per_chunk, SUBLANES), blocks_per_chunk), :]
            s = (s.reshape(blocks_per_chunk, SEL_BLOCK, cols) + add[:, None, :]).reshape(FAR_CHUNK, cols)
            return _flash_step(carry, s, vst_ref[0, :, pl.ds(ks, FAR_CHUNK)])

        def near_step(kt, carry):
            ks = pl.multiple_of(kt * Q_TILE, Q_TILE)
            s = _dot_nt(ks_ref[0, pl.ds(ks, Q_TILE), :], qp)
            w = jnp.where(kt >= qt - 1, kt - qt + n_w - 1, 1)
            rows = []
            for jb in range(blocks_per_tile):
                r = sb0_ref[pl.ds(kt * blocks_per_tile + jb, 1), :]
                rows.append(jnp.broadcast_to(r, (SEL_BLOCK, cols)))
            s = s + wt_ref[g, w] + jnp.concatenate(rows, axis=0)
            return _flash_step(carry, s, vst_ref[0, :, pl.ds(ks, Q_TILE)])

        init = (jnp.full((1, cols), NEG, f32), jnp.zeros((1, cols), f32), jnp.zeros((LANES, cols), f32))
        carry = lax.fori_loop(0, n_far, far_step, init)
        _, l_s, acc_s = lax.fori_loop(n_far * (FAR_CHUNK // Q_TILE), qt + 1, near_step, carry)
        o_s = acc_s / l_s

        wk = n_w * Q_TILE
        start = pl.multiple_of(qt * Q_TILE, Q_TILE)
        sw = _dot_nt(kw_ref[0, pl.ds(start, wk), :], qp) + wt_ref[g].reshape(wk, cols)
        rw = lax.broadcasted_iota(i32, (wk, cols), 0)
        sw = jnp.where(rw >= WINDOW - qt * Q_TILE, sw, NEG)
        pw = _masked_softmax_cols(sw)
        o_w = _dot(vwt_ref[0, :, pl.ds(start, wk)], pw.astype(bf16))

        for hh in range(GROUP):
            head = g * GROUP + hh
            cs = slice(hh * Q_TILE, (hh + 1) * Q_TILE)
            ds = slice(g * HEAD_DIM, (g + 1) * HEAD_DIM)
            o = (gates_t[3 * head:3 * head + 1, :] * o_c[ds, cs]
                 + gates_t[3 * head + 1:3 * head + 2, :] * o_s[ds, cs]
                 + gates_t[3 * head + 2:3 * head + 3, :] * o_w[ds, cs])
            out_ref[0, head * HEAD_DIM:(head + 1) * HEAD_DIM, :] = o


def _attn_prompt(q_pk, gates, kcp, ks_bf, kw_bf, wt, ct, c31):
    b, t, _ = q_pk.shape
    ncp = kcp.shape[1]
    n_sel = t // SEL_BLOCK
    n_cmp = t // CMP_STRIDE - 1
    assert t % FAR_CHUNK == 0 and n_sel % SUBLANES == 0
    n_w = WINDOW // Q_TILE + 1
    cols = GROUP * Q_TILE
    half = KV_W // 2
    vct = jnp.swapaxes(kcp[:, :, half:], 1, 2)
    vst = jnp.swapaxes(ks_bf[:, :, half:], 1, 2)
    kw_pad = jnp.pad(kw_bf, ((0, 0), (WINDOW, 0), (0, 0)))
    vwt = jnp.swapaxes(kw_pad[:, :, half:], 1, 2)
    ovt = _overlap_t(n_sel, n_sel, ncp, n_cmp)
    tp = t + WINDOW
    per_b = lambda shape: pl.BlockSpec((1,) + shape, lambda i, j: (i, 0, 0))
    attn_t = pl.pallas_call(
        functools.partial(_attn_prompt_kernel, n_sel=n_sel, ncp=ncp),
        grid=(b, t // Q_TILE),
        in_specs=[
            pl.BlockSpec((1, Q_TILE, N_HEADS * LANES), lambda i, j: (i, j, 0)),
            pl.BlockSpec((1, Q_TILE, LANES), lambda i, j: (i, j, 0)),
            per_b((ncp, half)), per_b((half, ncp)),
            per_b((t, half)), per_b((half, t)),
            per_b((tp, half)), per_b((half, tp)),
            _resident(ovt.shape), _resident(wt.shape), _resident(ct.shape), _resident(c31.shape),
        ],
        out_specs=pl.BlockSpec((1, ATTN_DIM, Q_TILE), lambda i, j: (i, 0, j)),
        out_shape=jax.ShapeDtypeStruct((b, ATTN_DIM, t), f32),
        scratch_shapes=[pltpu.VMEM((ncp, cols), f32), pltpu.VMEM((n_sel, cols), f32), pltpu.VMEM((n_sel, cols), f32)],
        compiler_params=_cparams(("parallel", "arbitrary")),
        name="attn_prompt",
    )(q_pk, gates, kcp, vct, ks_bf, vst, kw_pad, vwt, ovt, wt, ct, c31)
    return jnp.swapaxes(attn_t, 1, 2)


def _softmax_rows2(s_main, s_new):
    m = jnp.maximum(jnp.max(s_main, axis=1, keepdims=True), jnp.max(s_new, axis=1, keepdims=True))
    e_main, e_new = jnp.exp(s_main - m), jnp.exp(s_new - m)
    inv = 1.0 / (jnp.sum(e_main, axis=1, keepdims=True) + jnp.sum(e_new, axis=1, keepdims=True))
    return (e_main * inv).astype(bf16), (e_new * inv).astype(bf16)


def _attn_sample_kernel(*refs, n_pages, n_sel, past_len):
    page_refs = refs[1:1 + n_pages]
    (q_ref, g_ref, kcp_ref, knew_ref, win_ref, wnew_ref, ovt_ref, rmat_ref, emat_ref,
     scmp_ref, sk_ref, snew_ref, swin_ref, c31_ref, out_ref, ke_ref, vt_ref) = refs[1 + n_pages:]
    half = KV_W // 2
    q = q_ref[0]
    q_rows = q[0:SAMPLE_COLS]

    kcp = kcp_ref[0]
    pc = _masked_softmax_cols(_dot_nt(kcp[:, :half], q) + scmp_ref[...])
    o_c = _dot_tn(pc.astype(bf16), kcp[:, half:])[0:SAMPLE_COLS]
    p_hi, p_lo = _split_bf16(pc)
    imp = _dot(ovt_ref[...], p_hi) + _dot(ovt_ref[...], p_lo)
    i_hi, i_lo = _split_bf16(imp)
    imp = _dot(i_hi, rmat_ref[...]) + _dot(i_lo, rmat_ref[...])
    rows = imp.shape[0]
    blk = lax.broadcasted_iota(i32, (rows, LANES), 0)
    qpos = past_len + (lax.broadcasted_iota(i32, (rows, LANES), 1) & 3)
    cur = lax.shift_right_logical(qpos, SEL_SHIFT)
    forced = (blk == 0) | (blk == cur) | (blk == cur - 1)
    future = blk * SEL_BLOCK > qpos
    imp = jnp.where(forced, FORCE, jnp.where(future, -FORCE, imp))
    imp = jnp.where(blk < n_sel, imp, -jnp.inf)
    sel = _top_rows(imp, min(SEL_TOP, n_sel))
    sb = jnp.where(sel, 0.0, NEG)
    past_blocks = past_len // SEL_BLOCK
    sb_rows = sb[0:past_blocks].T[0:SAMPLE_COLS]

    for k in range(n_pages):
        lanes = slice(k * PAGE_SIZE, (k + 1) * PAGE_SIZE)
        ke_ref[0:half, lanes] = page_refs[k][0, 0:half, :].astype(bf16)
        vt_ref[:, lanes] = page_refs[k][0, half:, :].astype(bf16)
    ke_ref[half:, :] = emat_ref[...]
    n_far = past_len - PAGE_SIZE
    s = _dot(jnp.concatenate([q_rows, sb_rows.astype(bf16)], axis=1), ke_ref[...])
    s = jnp.concatenate([s[:, :n_far] + c31_ref[:, 0:1], s[:, n_far:] + sk_ref[...]], axis=1)
    knew = knew_ref[0].astype(bf16)
    s_n = _dot(q_rows, knew[0:half]) + snew_ref[...]
    p, p_n = _softmax_rows2(s, s_n)
    o_s = _dot_nt(p, vt_ref[...]) + _dot_nt(p_n, knew[half:])

    win = win_ref[0].astype(bf16)
    wnew = wnew_ref[0].astype(bf16)
    p, p_n = _softmax_rows2(_dot(q_rows, win[0:half]) + swin_ref[...], _dot(q_rows, wnew[0:half]) + snew_ref[...])
    o_w = _dot_nt(p, win[half:]) + _dot_nt(p_n, wnew[half:])

    gt = g_ref[0]
    out_ref[0] = gt[:, 0:1] * o_c + gt[:, 1:2] * o_s + gt[:, 2:3] * o_w


def _attn_sample(q_pk, gates, kcp, pages_t, page_table, ks_new, win_t, kw_new, tabs):
    scmp, sk, snew, swin, c31s = tabs
    bd, t_new, _ = q_pk.shape
    n_pages = page_table.shape[1]
    past_len = n_pages * PAGE_SIZE
    ncp = kcp.shape[1]
    n_cmp = past_len // CMP_STRIDE - 1
    n_sel = -(-(past_len + t_new) // SEL_BLOCK)
    n_sel_rows = -(-n_sel // SUBLANES) * SUBLANES
    past_blocks = past_len // SEL_BLOCK
    n_cols = N_KV * GROUP * t_new
    assert n_cols == SAMPLE_COLS and win_t.shape[2] == WINDOW and past_blocks == LANES
    qc = q_pk.reshape(bd, t_new, N_HEADS, LANES).transpose(0, 2, 1, 3).reshape(bd, n_cols, LANES)
    qc = jnp.pad(qc, ((0, 0), (0, LANES - n_cols), (0, 0)))
    gc = gates[:, :, :3 * N_HEADS].reshape(bd, t_new, N_HEADS, 3).transpose(0, 2, 1, 3).reshape(bd, n_cols, 3)
    gc = jnp.pad(gc, ((0, 0), (0, 0), (0, SUBLANES - 3)))
    new_t = lambda a: jnp.pad(jnp.swapaxes(a, 1, 2), ((0, 0), (0, 0), (0, LANES - t_new)))
    ovt = _overlap_t(n_sel_rows, n_sel, ncp, n_cmp)
    c = np.arange(LANES)
    same = (c[:, None] // (GROUP * t_new) == c[None, :] // (GROUP * t_new)) & (c[:, None] % t_new == c[None, :] % t_new)
    rmat = jnp.asarray(same & (c[:, None] < n_cols) & (c[None, :] < n_cols), dtype=bf16)
    emat = jnp.asarray(np.arange(past_blocks)[:, None] == np.arange(past_len)[None, :] // SEL_BLOCK, dtype=bf16)

    def page_spec(k):
        return pl.BlockSpec((1, KV_W, PAGE_SIZE), lambda i, pt: (pt[i, k], 0, 0))

    per_b = lambda shape: pl.BlockSpec((1,) + shape, lambda i, pt: (i, 0, 0))
    res = lambda a: pl.BlockSpec(a.shape, lambda i, pt: (0,) * a.ndim)
    half = KV_W // 2
    grid_spec = pltpu.PrefetchScalarGridSpec(
        num_scalar_prefetch=1,
        grid=(bd,),
        in_specs=[page_spec(k) for k in range(n_pages)] + [
            per_b((LANES, LANES)), per_b((n_cols, SUBLANES)), per_b((ncp, KV_W)), per_b((KV_W, LANES)),
            per_b((KV_W, WINDOW)), per_b((KV_W, LANES)),
            res(ovt), res(rmat), res(emat), res(scmp), res(sk), res(snew), res(swin), res(c31s)],
        out_specs=per_b((n_cols, LANES)),
        scratch_shapes=[pltpu.VMEM((half + past_blocks, past_len), bf16), pltpu.VMEM((half, past_len), bf16)],
    )
    o = pl.pallas_call(
        functools.partial(_attn_sample_kernel, n_pages=n_pages, n_sel=n_sel, past_len=past_len),
        grid_spec=grid_spec,
        out_shape=jax.ShapeDtypeStruct((bd, n_cols, LANES), f32),
        compiler_params=_cparams(("parallel",)),
        name="attn_sample",
    )(page_table, *([pages_t] * n_pages), qc, gc, kcp, new_t(ks_new), win_t, new_t(kw_new),
      ovt, rmat, emat, scmp, sk, snew, swin, c31s)
    o = o.reshape(bd, N_KV, GROUP, t_new, N_KV, HEAD_DIM)
    o = jnp.stack([o[:, g, :, :, g, :] for g in range(N_KV)], axis=1)
    return o.transpose(0, 3, 1, 2, 4).reshape(bd, t_new, ATTN_DIM)


def _merge_kernel(x_ref, a_ref, u0_ref, u1_ref, u2_ref, cb_ref, mg_ref, cw_ref, cbias_ref,
                  wau_ref, wcu_ref, wo_ref, g2_ref, x2_ref, h2_ref, *, d_model):
    y = cbias_ref[...] + cw_ref[0:1, :] * u2_ref[...]
    y = y + cw_ref[1:2, :] * u1_ref[...]
    y = y + cw_ref[2:3, :] * u0_ref[...]
    up_a = _dot(a_ref[...].astype(bf16), wau_ref[...])
    up_c = _dot((cb_ref[...] * y).astype(bf16), wcu_ref[...])
    mixed = mg_ref[:, :d_model] * up_a + mg_ref[:, d_model:] * up_c
    x2 = x_ref[...] + _dot(mixed.astype(bf16), wo_ref[...])
    x2_ref[...] = x2
    h2_ref[...] = _rmsnorm(x2, g2_ref[...]).astype(bf16)


def _merge(x, attn, u0, u1, u2, cb, mg, conv_w, conv_b, w_attn_up, w_conv_up, w_out, norm2_g):
    n, d = x.shape
    tm = ROW_TILE
    row = lambda w: pl.BlockSpec((tm, w), lambda i: (i, 0))
    cw = jnp.pad(conv_w, ((0, SUBLANES - CONV_W), (0, 0)))
    return pl.pallas_call(
        functools.partial(_merge_kernel, d_model=d),
        grid=(n // tm,),
        in_specs=[row(d), row(ATTN_DIM), row(CONV_DIM), row(CONV_DIM), row(CONV_DIM), row(CONV_DIM), row(2 * d),
                  _resident(cw.shape), _resident((1, CONV_DIM)), _resident(w_attn_up.shape),
                  _resident(w_conv_up.shape), _resident(w_out.shape), _resident((1, d))],
        out_specs=[row(d), row(d)],
        out_shape=[jax.ShapeDtypeStruct((n, d), f32), jax.ShapeDtypeStruct((n, d), bf16)],
        compiler_params=_cparams(("parallel",)),
        name="merge",
    )(x, attn, u0, u1, u2, cb, mg, cw, conv_b.reshape(1, CONV_DIM), w_attn_up.astype(bf16),
      w_conv_up.astype(bf16), w_out.astype(bf16), norm2_g.reshape(1, d))


def _top_rows_sorted(s, k):
    rows = lax.broadcasted_iota(i32, s.shape, 0).astype(f32)
    vals, idxs = [], []
    for _ in range(k):
        m = jnp.max(s, axis=0, keepdims=True)
        first = jnp.min(jnp.where(s == m, rows, 1e9), axis=0, keepdims=True)
        vals.append(m)
        idxs.append(first)
        s = jnp.where(rows == first, -jnp.inf, s)
    return jnp.concatenate(vals, axis=0), jnp.concatenate(idxs, axis=0)


def _peer_route_kernel(h_ref, wq_ref, k1_ref, k2_ref, a_ref, b_ref, g_ref):
    qp = _dot(h_ref[...], wq_ref[...]).astype(bf16)
    key_dim = 2 * PEER_HALF
    a_rows, b_rows, g_rows = [], [], []
    for h in range(PEER_HEADS):
        q1 = qp[:, h * key_dim:h * key_dim + PEER_HALF]
        q2 = qp[:, h * key_dim + PEER_HALF:(h + 1) * key_dim]
        v1, i1 = _top_rows_sorted(_dot_nt(k1_ref[...], q1), PEER_TOPK)
        v2, i2 = _top_rows_sorted(_dot_nt(k2_ref[...], q2), PEER_TOPK)
        cand = jnp.concatenate([v1[r:r + 1, :] + v2 for r in range(PEER_TOPK)], axis=0)
        sc, ci = _top_rows_sorted(cand, PEER_TOPK)
        c1 = jnp.floor(ci * (1.0 / PEER_TOPK))
        c2 = ci - c1 * PEER_TOPK
        e1 = jnp.zeros_like(ci)
        e2 = jnp.zeros_like(ci)
        for r in range(PEER_TOPK):
            e1 = jnp.where(c1 == r, i1[r:r + 1, :], e1)
            e2 = jnp.where(c2 == r, i2[r:r + 1, :], e2)
        e = jnp.exp(sc - sc[0:1, :])
        a_rows.append(e1)
        b_rows.append(e2)
        g_rows.append(e / jnp.sum(e, axis=0, keepdims=True))
    tn = h_ref.shape[0]
    for src, dst in ((a_rows, a_ref), (b_rows, b_ref), (g_rows, g_ref)):
        full = jnp.concatenate(src, axis=0)
        for c in range(tn // LANES):
            dst[c * LANES:(c + 1) * LANES, :] = full[:, c * LANES:(c + 1) * LANES].T


def _peer_route(h2, wq, k1, k2):
    n, d = h2.shape
    tn = ROW_TILE
    assert PEER_HEADS * PEER_TOPK == LANES
    row = lambda w: pl.BlockSpec((tn, w), lambda i: (i, 0))
    return pl.pallas_call(
        _peer_route_kernel,
        grid=(n // tn,),
        in_specs=[row(d), _resident(wq.shape), _resident(k1.shape), _resident(k2.shape)],
        out_specs=[row(LANES)] * 3,
        out_shape=[jax.ShapeDtypeStruct((n, LANES), f32)] * 3,
        compiler_params=_cparams(("parallel",)),
        name="peer_route",
    )(h2, wq.astype(bf16), k1.astype(bf16), k2.astype(bf16))


def _peer_expert_kernel(h_ref, a_ref, b_ref, w_ref, x_ref, u_ref, v_ref, out_ref, g_ref, acc_ref, *, pitch):
    e_step = pl.program_id(1)
    tn = h_ref.shape[0]
    keys_per_chunk = EXPERT_CHUNK // PEER_NKEYS
    half_keys = PEER_NKEYS // 2
    hi_mask = jnp.uint32(0xFFFF0000)

    @pl.when(e_step == 0)
    def _build_gate_matrix():
        key_row = lax.broadcasted_iota(i32, (PEER_NKEYS, LANES), 0).astype(f32)

        def token(n, c):
            a = jnp.broadcast_to(a_ref[pl.ds(n, 1), :], (PEER_NKEYS, LANES))
            b = jnp.broadcast_to(b_ref[pl.ds(n, 1), :], (PEER_NKEYS, LANES))
            w = jnp.broadcast_to(w_ref[pl.ds(n, 1), :], (PEER_NKEYS, LANES))
            pa = (a == key_row).astype(bf16)
            wb = jnp.where(b == key_row, w, 0.0).astype(bf16)
            g = _dot_nt(pa, wb).astype(bf16).astype(f32)
            bits = lax.bitcast_convert_type(g, jnp.uint32)
            word = lax.shift_right_logical(bits[:half_keys], jnp.uint32(16)) | (bits[half_keys:] & hi_mask)
            g_ref[pl.ds(n, half_keys, stride=pitch), :] = word
            return c

        lax.fori_loop(0, tn, token, 0, unroll=TOKEN_UNROLL)
        acc_ref[...] = jnp.zeros_like(acc_ref)

    s = _dot_nt(h_ref[...], u_ref[...])
    key0 = (e_step * keys_per_chunk) % half_keys
    words = jnp.concatenate(
        [g_ref[pl.ds(pl.multiple_of((key0 + j) * pitch, SUBLANES), tn), :] for j in range(keys_per_chunk)], axis=1)
    shift = jnp.where(e_step * keys_per_chunk < half_keys, 16, 0).astype(jnp.uint32)
    gates = lax.bitcast_convert_type(lax.shift_left(words, shift) & hi_mask, f32)
    act = 0.5 * s * (1.0 + lax.erf(s * math.sqrt(0.5)))
    acc_ref[...] += _dot((gates * act).astype(bf16), v_ref[...])

    @pl.when(e_step == pl.num_programs(1) - 1)
    def _finish():
        out_ref[...] = x_ref[...] + acc_ref[...]


def _peer_expert(h2, a_idx, b_idx, gate_w, x2, u_tab, v_tab):
    n, d = h2.shape
    n_exp = u_tab.shape[0]
    tn = PEER_TILE
    pitch = tn + G_PITCH_PAD
    assert n % tn == 0 and n_exp == PEER_NKEYS * PEER_NKEYS and (PEER_NKEYS // 2) % (EXPERT_CHUNK // PEER_NKEYS) == 0
    row = lambda w: pl.BlockSpec((tn, w), lambda i, e: (i, 0))
    tab = pl.BlockSpec((EXPERT_CHUNK, d), lambda i, e: (e, 0))
    return pl.pallas_call(
        functools.partial(_peer_expert_kernel, pitch=pitch),
        grid=(n // tn, n_exp // EXPERT_CHUNK),
        in_specs=[row(d), row(LANES), row(LANES), row(LANES), row(d), tab, tab],
        out_specs=row(d),
        out_shape=jax.ShapeDtypeStruct((n, d), f32),
        scratch_shapes=[pltpu.VMEM((PEER_NKEYS // 2 * pitch, LANES), jnp.uint32), pltpu.VMEM((tn, d), f32)],
        compiler_params=_cparams(("parallel", "arbitrary")),
        name="peer_expert",
    )(h2, a_idx, b_idx, gate_w, x2, u_tab.astype(bf16), v_tab.astype(bf16))


def _ple_kernel(x_ref, p_ref, pg_ref, wg_ref, wp_ref, fg_ref, y_ref):
    x = x_ref[...]
    gate = jax.nn.sigmoid(_dot(_rmsnorm(x, pg_ref[...]).astype(bf16), wg_ref[...]))
    x = x + gate * _dot(p_ref[...].astype(bf16), wp_ref[...])
    y_ref[...] = _rmsnorm(x, fg_ref[...])


def _ple(x3, p, ple_g, w_ple_gate, w_ple_proj, final_g):
    n, d = x3.shape
    tm = ROW_TILE
    row = lambda w: pl.BlockSpec((tm, w), lambda i: (i, 0))
    return pl.pallas_call(
        _ple_kernel,
        grid=(n // tm,),
        in_specs=[row(d), row(p.shape[1]), _resident((1, d)), _resident(w_ple_gate.shape),
                  _resident(w_ple_proj.shape), _resident((1, d))],
        out_specs=row(d),
        out_shape=jax.ShapeDtypeStruct((n, d), f32),
        compiler_params=_cparams(("parallel",)),
        name="ple_final",
    )(x3, p, ple_g.reshape(1, d), w_ple_gate.astype(bf16), w_ple_proj.astype(bf16), final_g.reshape(1, d))


def kernel(x_prompt, x_sample, p_prompt, p_sample, cache_cmp_kv, cache_slc_kv, page_table, state_win_kv, state_conv, norm1_g, w_in, w_cmp, cmp_pos, conv_w, conv_b, w_attn_up, w_conv_up, w_out, norm2_g, peer_wq, peer_k1, peer_k2, peer_u, peer_v, ple_g, w_ple_gate, w_ple_proj, rel_bias, final_g):
    depth = norm1_g.shape[0]
    assert depth == 1, "single-layer trunk"
    b, t, d = x_prompt.shape
    bd, t_new, _ = x_sample.shape
    n_pages = page_table.shape[1]
    past_len = n_pages * PAGE_SIZE
    n_p, n_s = b * t, bd * t_new
    kv_shape = lambda lead: lead + (2, N_KV, HEAD_DIM)

    x_all = jnp.concatenate([x_prompt.reshape(n_p, d), x_sample.reshape(n_s, d)], axis=0)
    w_pack = _pack_w_in(w_in[0], d)
    q_pk, kvc, kvs, kvw, kvs_bf, kvw_bf, gates, cin, cbg, mg = _in_proj(x_all, norm1_g[0], w_pack)
    split = lambda a: (a[:n_p].reshape(b, t, -1), a[n_p:].reshape(bd, t_new, -1))

    w4, pos_rows = _pack_w_cmp(w_cmp[0], cmp_pos[0])
    kvc_p, kvc_s = split(kvc)
    kcp_p = _compress_prompt(kvc_p, w4, pos_rows)
    kcp_s = _compress_sample(_pages_t(cache_cmp_kv[0]), page_table, w4, pos_rows)
    n_cmp_s = past_len // CMP_STRIDE - 1
    wt, ct, c31, *sample_tabs = _tables(rel_bias, past_len, kcp_s.shape[1], n_cmp_s)

    q_p, q_s = split(q_pk)
    g_p, g_s = split(gates)
    kvs_p, kvs_s = split(kvs)
    kvw_p, kvw_s = split(kvw)
    ksb_p, _ = split(kvs_bf)
    kwb_p, _ = split(kvw_bf)
    attn_p = _attn_prompt(q_p, g_p, kcp_p, ksb_p, kwb_p, wt, ct, c31)
    win_t = jnp.transpose(state_win_kv[0], (0, 2, 3, 4, 1)).reshape(bd, KV_W, -1)
    attn_s = _attn_sample(q_s, g_s, kcp_s, _pages_t(cache_slc_kv[0]), page_table, kvs_s, win_t, kvw_s, sample_tabs)
    attn = jnp.concatenate([attn_p.reshape(n_p, ATTN_DIM), attn_s.reshape(n_s, ATTN_DIM)], axis=0)

    cin_p, cin_s = split(cin)
    buf_p = jnp.zeros((b, CONV_W - 1, CONV_DIM), f32)
    buf_s = state_conv[0]

    def shifted(u, buf, k):
        return jnp.concatenate([buf[:, CONV_W - 1 - k:], u[:, :u.shape[1] - k]], axis=1).reshape(-1, CONV_DIM)

    u1 = jnp.concatenate([shifted(cin_p, buf_p, 1), shifted(cin_s, buf_s, 1)], axis=0)
    u2 = jnp.concatenate([shifted(cin_p, buf_p, 2), shifted(cin_s, buf_s, 2)], axis=0)
    x2, h2 = _merge(x_all, attn, cin, u1, u2, cbg, mg, conv_w[0], conv_b[0], w_attn_up[0], w_conv_up[0],
                    w_out[0], norm2_g[0])

    a_idx, b_idx, gate_w = _peer_route(h2, peer_wq[0], peer_k1[0], peer_k2[0])
    x3 = _peer_expert(h2, a_idx, b_idx, gate_w, x2, peer_u[0], peer_v[0])

    p_all = jnp.concatenate([p_prompt[0].reshape(n_p, -1), p_sample[0].reshape(n_s, -1)], axis=0)
    y = _ple(x3, p_all, ple_g[0], w_ple_gate[0], w_ple_proj[0], final_g)

    conv_tail = lambda u, buf: jnp.concatenate([buf, u], axis=1)[:, -(CONV_W - 1):]
    n_win = min(WINDOW, t)
    new_win_s = jnp.concatenate([state_win_kv[:, :, t_new:], kvw_s.reshape(kv_shape((1, bd, t_new)))], axis=2)
    return (
        y[:n_p].reshape(b, t, d),
        y[n_p:].reshape(bd, t_new, d),
        kvc_p.reshape(kv_shape((1, b, t))),
        kvc_s.reshape(kv_shape((1, bd, t_new))),
        kvs_p.reshape(kv_shape((1, b, t))),
        kvs_s.reshape(kv_shape((1, bd, t_new))),
        kvw_p[:, t - n_win:].reshape(kv_shape((1, b, n_win))),
        new_win_s,
        conv_tail(cin_p, buf_p)[None],
        conv_tail(cin_s, buf_s)[None],
    )
```

```python
import functools
import math

import jax
import jax.numpy as jnp
import numpy as np
from jax import lax
from jax.experimental import pallas as pl
from jax.experimental.pallas import tpu as pltpu

f32 = jnp.float32
bf16 = jnp.bfloat16
i32 = jnp.int32

N_HEADS = 8
N_KV = 2
GROUP = N_HEADS // N_KV
HEAD_DIM = 64
ATTN_DIM = N_HEADS * HEAD_DIM
KV_W = 2 * N_KV * HEAD_DIM
CMP_BLOCK = 32
CMP_STRIDE = 16
SEL_BLOCK = 64
SEL_SHIFT = 6
SEL_TOP = 16
WINDOW = 512
N_BUCKETS = 32
MAX_EXACT = N_BUCKETS // 2
MAX_DISTANCE = 128
CONV_DIM = 512
CONV_W = 3
PEER_HEADS = 8
PEER_NKEYS = 128
PEER_HALF = 128
PEER_TOPK = 16
PAGE_SIZE = 128
EPS = 1e-6
NEG = -1e30
FORCE = 1e4

LANES = 128
SUBLANES = 8
VMEM_LIMIT_BYTES = 56 * 1024 * 1024

Q_TILE = 128
SAMPLE_COLS = 32
FAR_CHUNK = 512
CMP_PAD = 16
CMP_WIN = 24
ONES_ROWS = 16
ROW_TILE = 256
PEER_TILE = 512
EXPERT_CHUNK = 1024
TOKEN_UNROLL = 16
G_PITCH_PAD = 8


def _cparams(sem):
    return pltpu.CompilerParams(dimension_semantics=sem, vmem_limit_bytes=VMEM_LIMIT_BYTES)


def _dot(a, b):
    return jnp.dot(a, b, preferred_element_type=f32)


def _dot_nt(a, b):
    return lax.dot_general(a, b, (((1,), (1,)), ((), ())), preferred_element_type=f32)


def _dot_tn(a, b):
    return lax.dot_general(a, b, (((0,), (0,)), ((), ())), preferred_element_type=f32)


def _split_bf16(x):
    hi = x.astype(bf16)
    lo = (x - hi.astype(f32)).astype(bf16)
    return hi, lo


def _resident(shape):
    nd = len(shape)
    return pl.BlockSpec(shape, lambda *_: (0,) * nd)


def _rmsnorm(x, g):
    ms = jnp.mean(x * x, axis=-1, keepdims=True)
    return x * lax.rsqrt(ms + EPS) * g


def _in_proj_kernel(x_ref, g_ref, w_ref, q_ref, kc_ref, ks_ref, kw_ref, ksb_ref, kwb_ref,
                    gn_ref, cin_ref, cb_ref, mg_ref, *, d_model):
    qw = N_HEADS * LANES
    h = _rmsnorm(x_ref[...], g_ref[...]).astype(bf16)
    o = 0
    q_ref[...] = (_dot(h, w_ref[:, o:o + qw]) * (HEAD_DIM ** -0.5)).astype(bf16)
    o += qw
    kc_ref[...] = _dot(h, w_ref[:, o:o + KV_W])
    o += KV_W
    ks = _dot(h, w_ref[:, o:o + KV_W])
    ks_ref[...] = ks
    ksb_ref[...] = ks.astype(bf16)
    o += KV_W
    kw = _dot(h, w_ref[:, o:o + KV_W])
    kw_ref[...] = kw
    kwb_ref[...] = kw.astype(bf16)
    o += KV_W
    gn_ref[...] = jax.nn.sigmoid(_dot(h, w_ref[:, o:o + LANES]))
    o += LANES
    cv = _dot(h, w_ref[:, o:o + CONV_DIM])
    o += CONV_DIM
    cb_ref[...] = _dot(h, w_ref[:, o:o + CONV_DIM])
    o += CONV_DIM
    cin_ref[...] = _dot(h, w_ref[:, o:o + CONV_DIM]) * cv
    o += CONV_DIM
    mg_ref[...] = jax.nn.sigmoid(_dot(h, w_ref[:, o:o + 2 * d_model]))


def _pack_w_in(w_in, d_model):
    sizes = (ATTN_DIM, KV_W, KV_W, KV_W, 3 * N_HEADS, CONV_DIM, CONV_DIM, CONV_DIM, 2 * d_model)
    parts, s = [], 0
    for n in sizes:
        parts.append(w_in[:, s:s + n])
        s += n
    wq = parts[0].reshape(d_model, N_KV, GROUP, HEAD_DIM)
    z = jnp.zeros_like(wq)
    slabs = [jnp.concatenate([wq[:, 0], z[:, 0]], axis=-1), jnp.concatenate([z[:, 1], wq[:, 1]], axis=-1)]
    wq = jnp.stack(slabs, axis=1).reshape(d_model, N_HEADS * LANES)
    gn = jnp.pad(parts[4], ((0, 0), (0, LANES - 3 * N_HEADS)))
    return jnp.concatenate([wq, parts[1], parts[2], parts[3], gn] + parts[5:], axis=1).astype(bf16)


def _in_proj(x, norm_g, w_pack):
    n, d = x.shape
    tm = ROW_TILE
    assert n % tm == 0
    row = lambda w: pl.BlockSpec((tm, w), lambda i: (i, 0))
    widths = (N_HEADS * LANES, KV_W, KV_W, KV_W, KV_W, KV_W, LANES, CONV_DIM, CONV_DIM, 2 * d)
    dtypes = (bf16, f32, f32, f32, bf16, bf16, f32, f32, f32, f32)
    return pl.pallas_call(
        functools.partial(_in_proj_kernel, d_model=d),
        grid=(n // tm,),
        in_specs=[row(d), _resident((1, d)), _resident(w_pack.shape)],
        out_specs=[row(w) for w in widths],
        out_shape=[jax.ShapeDtypeStruct((n, w), t) for w, t in zip(widths, dtypes)],
        compiler_params=_cparams(("parallel",)),
        name="in_proj",
    )(x, norm_g.reshape(1, d), w_pack)


def _bias_of_dist(dist, valid, rb_ref, head, shift=0.0):
    n = jnp.maximum(dist, 0)
    nf = jnp.maximum(n, 1).astype(f32)
    large = MAX_EXACT + (jnp.log(nf / MAX_EXACT) / math.log(MAX_DISTANCE / MAX_EXACT)
                         * (N_BUCKETS - MAX_EXACT)).astype(i32)
    large = jnp.minimum(large, N_BUCKETS - 1)
    bucket = jnp.where(n < MAX_EXACT, n, large)
    out = jnp.zeros(dist.shape, f32)
    for b in range(N_BUCKETS):
        out = jnp.where(bucket == b, rb_ref[b, head], out)
    return jnp.where(valid, out - shift, NEG)


def _tables_kernel(rb_ref, wt_ref, ct_ref, scmp_ref, sk_ref, snew_ref, swin_ref, c31s_ref,
                   *, past_len, ncp_s, n_cmp_s):
    ik = lax.broadcasted_iota(i32, (Q_TILE, Q_TILE), 0)
    iq = lax.broadcasted_iota(i32, (Q_TILE, Q_TILE), 1)
    n_w = WINDOW // Q_TILE + 1
    for g in range(N_KV):
        wt_ref[g, 0] = jnp.full(wt_ref.shape[2:], NEG, f32)
        for hh in range(GROUP):
            head = g * GROUP + hh
            far = rb_ref[N_BUCKETS - 1, head]
            cols = slice(hh * Q_TILE, (hh + 1) * Q_TILE)
            for w in range(n_w):
                dist = Q_TILE * (n_w - 1 - w) + iq - ik
                wt_ref[g, w + 1, :, cols] = _bias_of_dist(dist, (dist >= 0) & (dist < WINDOW), rb_ref, head, far)
            mm = lax.broadcasted_iota(i32, (CMP_WIN, Q_TILE), 0)
            jq = lax.broadcasted_iota(i32, (CMP_WIN, Q_TILE), 1)
            dist = jq - CMP_STRIDE * (mm - CMP_PAD) - (CMP_BLOCK - 1)
            ct_ref[g, :, cols] = _bias_of_dist(dist, dist >= 0, rb_ref, head, far)
    def sample_table(shape, q_axis, dist_fn, valid_fn, out_ref):
        c = lax.broadcasted_iota(i32, shape, q_axis)
        r = lax.broadcasted_iota(i32, shape, 1 - q_axis)
        t = c & 3
        hd = jnp.minimum(lax.shift_right_logical(c, 2), N_HEADS - 1)
        dist = dist_fn(r, t)
        valid = valid_fn(r, t, dist)
        acc = jnp.zeros(shape, f32)
        for head in range(N_HEADS):
            acc = jnp.where(hd == head, _bias_of_dist(dist, valid, rb_ref, head), acc)
        out_ref[...] = acc

    sample_table((ncp_s, LANES), 1, lambda r, t: past_len + t - CMP_STRIDE * (r - CMP_PAD) - (CMP_BLOCK - 1),
                 lambda r, t, d: (r >= CMP_PAD) & (r < CMP_PAD + n_cmp_s) & (d >= 0), scmp_ref)
    sample_table((SAMPLE_COLS, PAGE_SIZE), 0, lambda r, t: PAGE_SIZE + t - r, lambda r, t, d: d >= 0, sk_ref)
    sample_table((SAMPLE_COLS, LANES), 0, lambda r, t: t - r, lambda r, t, d: (d >= 0) & (r < 4), snew_ref)
    sample_table((SAMPLE_COLS, WINDOW), 0, lambda r, t: WINDOW + t - r, lambda r, t, d: d < WINDOW, swin_ref)
    sample_table((SAMPLE_COLS, LANES), 0, lambda r, t: jnp.full_like(r, MAX_DISTANCE), lambda r, t, d: d > 0, c31s_ref)


def _tables(rel_bias, past_len, ncp_s, n_cmp_s):
    n_w = WINDOW // Q_TILE + 1
    cols = GROUP * Q_TILE
    shapes = [
        (N_KV, n_w + 1, Q_TILE, cols), (N_KV, CMP_WIN, cols),
        (ncp_s, LANES), (SAMPLE_COLS, PAGE_SIZE), (SAMPLE_COLS, LANES), (SAMPLE_COLS, WINDOW), (SAMPLE_COLS, LANES),
    ]
    return pl.pallas_call(
        functools.partial(_tables_kernel, past_len=past_len, ncp_s=ncp_s, n_cmp_s=n_cmp_s),
        in_specs=[pl.BlockSpec(memory_space=pltpu.SMEM)],
        out_shape=[jax.ShapeDtypeStruct(s, f32) for s in shapes],
        compiler_params=pltpu.CompilerParams(vmem_limit_bytes=VMEM_LIMIT_BYTES),
        name="bias_tables",
    )(rel_bias)


def _ncp(n_chunks):
    return -(-(n_chunks + CMP_PAD + SUBLANES) // LANES) * LANES


def _compress_core(chunk_rows, w_ref, pos_ref, out_ref, n_chunks):
    half = KV_W // 2
    out_ref[...] = jnp.zeros(out_ref.shape, out_ref.dtype)
    for c in range(2):
        fs = None
        pb = None
        for p in range(CMP_STRIDE // 2):
            x = jnp.concatenate([chunk_rows(2 * p, c), chunk_rows(2 * p + 1, c)], axis=1).astype(bf16)
            lanes = [slice((2 * p + k) * KV_W + c * half, (2 * p + k) * KV_W + (c + 1) * half) for k in range(2)]
            px = jnp.concatenate([pos_ref[:, lanes[0]], pos_ref[:, lanes[1]]], axis=1)
            fs = _dot(x, w_ref[p, c]) if fs is None else fs + _dot(x, w_ref[p, c])
            pb = _dot(px, w_ref[p, c]) if pb is None else pb + _dot(px, w_ref[p, c])
        bias = pb[0:1, :half] + pb[1:2, half:]
        nxt = pltpu.roll(fs[:, half:], n_chunks - 1, axis=0)
        kc = fs[:, :half] + nxt + bias
        r = lax.broadcasted_iota(i32, kc.shape, 0)
        kc = jnp.where(r < n_chunks - 1, kc, 0.0)
        out_ref[0, CMP_PAD:CMP_PAD + n_chunks, c * half:(c + 1) * half] = kc.astype(out_ref.dtype)


def _compress_prompt_kernel(x_ref, w_ref, pos_ref, out_ref, *, n_chunks):
    half = KV_W // 2
    rows = lambda l, c: x_ref[0, :, l * KV_W + c * half:l * KV_W + (c + 1) * half]
    _compress_core(rows, w_ref, pos_ref, out_ref, n_chunks)


def _compress_paged_kernel(*refs, n_pages, n_chunks):
    page_refs = refs[1:1 + n_pages]
    perm_ref, w_ref, pos_ref, out_ref, xp_ref = refs[1 + n_pages:]
    half = KV_W // 2
    per_page = PAGE_SIZE // CMP_STRIDE
    for k in range(n_pages):
        xp_ref[k] = _dot_nt(perm_ref[...], page_refs[k][0].astype(bf16))

    def rows(l, c):
        x = xp_ref[:, l * per_page:(l + 1) * per_page, c * half:(c + 1) * half]
        return x.reshape(n_pages * per_page, half)

    _compress_core(rows, w_ref, pos_ref, out_ref, n_chunks)


def _pack_w_cmp(w_cmp, cmp_pos):
    w = w_cmp.reshape(2, CMP_STRIDE // 2, 2, 2, HEAD_DIM, HEAD_DIM)
    w4 = jnp.einsum("hplcde,gG->pclgdhGe", w, jnp.eye(N_KV, dtype=f32))
    w4 = w4.reshape(CMP_STRIDE // 2, 2, KV_W, KV_W)

    def pos_row(p):
        return jnp.broadcast_to(p[:, :, None, :], (CMP_STRIDE, 2, N_KV, HEAD_DIM)).reshape(1, -1)

    pos = jnp.concatenate([pos_row(cmp_pos[:CMP_STRIDE]), pos_row(cmp_pos[CMP_STRIDE:]),
                           jnp.zeros((SUBLANES - 2, CMP_STRIDE * KV_W), f32)], axis=0)
    return w4.astype(bf16), pos.astype(bf16)


def _compress_prompt(kv_cmp, w4, pos):
    b, t, _ = kv_cmp.shape
    n_chunks = t // CMP_STRIDE
    cw = CMP_STRIDE * KV_W
    chunks = kv_cmp.reshape(b, n_chunks, cw)
    ncp = _ncp(n_chunks)
    return pl.pallas_call(
        functools.partial(_compress_prompt_kernel, n_chunks=n_chunks),
        grid=(b,),
        in_specs=[pl.BlockSpec((1, n_chunks, cw), lambda i: (i, 0, 0)), _resident(w4.shape), _resident(pos.shape)],
        out_specs=pl.BlockSpec((1, ncp, KV_W), lambda i: (i, 0, 0)),
        out_shape=jax.ShapeDtypeStruct((b, ncp, KV_W), bf16),
        compiler_params=_cparams(("parallel",)),
        name="compress_prompt",
    )(chunks, w4, pos)


def _pages_t(cache):
    n_pool = cache.shape[0]
    return jnp.transpose(cache, (0, 2, 3, 4, 1)).reshape(n_pool, KV_W, cache.shape[1])


def _compress_sample(pages_t, page_table, w4, pos):
    bd, n_pages = page_table.shape
    per_page = PAGE_SIZE // CMP_STRIDE
    n_chunks = n_pages * per_page
    ncp = _ncp(n_chunks)
    tok = np.arange(PAGE_SIZE)
    perm = jnp.asarray(tok[None, :] == (tok[:, None] % per_page) * CMP_STRIDE + tok[:, None] // per_page, dtype=bf16)

    def page_spec(k):
        return pl.BlockSpec((1, KV_W, PAGE_SIZE), lambda i, pt: (pt[i, k], 0, 0))

    res = lambda a: pl.BlockSpec(a.shape, lambda i, pt: (0,) * a.ndim)
    grid_spec = pltpu.PrefetchScalarGridSpec(
        num_scalar_prefetch=1,
        grid=(bd,),
        in_specs=[page_spec(k) for k in range(n_pages)] + [res(perm), res(w4), res(pos)],
        out_specs=pl.BlockSpec((1, ncp, KV_W), lambda i, pt: (i, 0, 0)),
        scratch_shapes=[pltpu.VMEM((n_pages, PAGE_SIZE, KV_W), f32)],
    )
    return pl.pallas_call(
        functools.partial(_compress_paged_kernel, n_pages=n_pages, n_chunks=n_chunks),
        grid_spec=grid_spec,
        out_shape=jax.ShapeDtypeStruct((bd, ncp, KV_W), bf16),
        compiler_params=_cparams(("parallel",)),
        name="compress_sample",
    )(page_table, *([pages_t] * n_pages), perm, w4, pos)


def _overlap_t(n_sel_rows, n_sel, ncp, n_cmp):
    j = np.arange(n_sel_rows)[:, None]
    n = np.arange(ncp)[None, :] - CMP_PAD
    hit = (n * CMP_STRIDE < (j + 1) * SEL_BLOCK) & (n * CMP_STRIDE + CMP_BLOCK > j * SEL_BLOCK)
    hit &= (n >= 0) & (n < n_cmp) & (j < n_sel)
    return jnp.asarray(hit, dtype=bf16)


def _masked_softmax_cols(s, maybe_empty=True):
    m = jnp.max(s, axis=0, keepdims=True)
    e = jnp.exp(s - m)
    inv = 1.0 / jnp.sum(e, axis=0, keepdims=True)
    if maybe_empty:
        inv = jnp.where(m > 0.5 * NEG, inv, 0.0)
    return e * inv


def _top_rows(imp, k):
    rows = lax.broadcasted_iota(i32, imp.shape, 0).astype(f32)
    sel = jnp.zeros(imp.shape, jnp.bool_)
    v = imp
    for _ in range(k):
        m = jnp.max(v, axis=0, keepdims=True)
        first = jnp.min(jnp.where(v == m, rows, 1e9), axis=0, keepdims=True)
        pick = rows == first
        sel = sel | pick
        v = jnp.where(pick, -jnp.inf, v)
    return sel


def _flash_steps(carry, scores, v_ext):
    m_new = [jnp.maximum(carry[2 * i], jnp.max(s, axis=0, keepdims=True)) for i, s in enumerate(scores)]
    p = [jnp.exp(s - m).astype(bf16) for s, m in zip(scores, m_new)]
    out = []
    for i in range(len(scores)):
        alpha = jnp.exp(carry[2 * i] - m_new[i])
        out.extend((m_new[i], alpha * carry[2 * i + 1] + _dot(v_ext, p[i])))
    return tuple(out)


def _attn_prompt_kernel(q_ref, gn_ref, kc_ref, vce_ref, ks_ref, vse_ref, kw_ref, vwe_ref, emat_ref, ove_ref,
                        wt_ref, ct_ref, out_ref, cadd_ref, *, n_sel, ncp):
    qt = pl.program_id(1)
    cols = GROUP * Q_TILE
    n_w = WINDOW // Q_TILE + 1
    vrows = KV_W // 2
    gates_t = gn_ref[0].T
    n_far = jnp.maximum(qt - 1, 0) // (FAR_CHUNK // Q_TILE)
    groups = range(N_KV)
    qp = [jnp.concatenate(
        [q_ref[0, :, (g * GROUP + hh) * LANES:(g * GROUP + hh + 1) * LANES] for hh in range(GROUP)], axis=0)
        for g in groups]

    wk = n_w * Q_TILE
    start = pl.multiple_of(qt * Q_TILE, Q_TILE)
    lc, sw = [], []
    for g in groups:
        ri = lax.broadcasted_iota(i32, (ncp, cols), 0)
        cadd_ref[g] = jnp.where((ri >= CMP_PAD) & (ri < SUBLANES * qt), 0.0, NEG)
        wr = SUBLANES * qt + lax.broadcasted_iota(i32, (CMP_WIN, cols), 0)
        cadd_ref[g, pl.ds(pl.multiple_of(SUBLANES * qt, SUBLANES), CMP_WIN), :] = jnp.where(wr >= CMP_PAD, ct_ref[g], NEG)
        lc.append(_dot_nt(kc_ref[0], qp[g]) + cadd_ref[g])
    for g in groups:
        tiles = [wt_ref[g, jnp.where(qt + w >= n_w - 1, w + 1, 0)] for w in range(n_w)]
        sw.append(_dot_nt(kw_ref[0, pl.ds(start, wk), :], qp[g]) + jnp.concatenate(tiles, axis=0))
    live, e_c, e_w = [], [], []
    for g in groups:
        m = jnp.max(lc[g], axis=0, keepdims=True)
        live.append(m > 0.5 * NEG)
        e_c.append(jnp.exp(lc[g] - m))
    for g in groups:
        e_w.append(jnp.exp(sw[g] - jnp.max(sw[g], axis=0, keepdims=True)).astype(bf16))

    o_c, o_w, qx = [], [], []
    for g in groups:
        oc = _dot(vce_ref[0], e_c[g].astype(bf16))
        o_c.append(oc[:vrows] * jnp.where(live[g], 1.0 / oc[vrows:vrows + 1], 0.0))
        e_hi, e_lo = _split_bf16(e_c[g])
        raw = _dot(ove_ref[...], e_hi) + _dot(ove_ref[...], e_lo)
        raw = raw[:n_sel] * jnp.where(live[g], 1.0 / raw[n_sel:n_sel + 1], 0.0)
        imp = raw[:, 0:Q_TILE]
        for hh in range(1, GROUP):
            imp = imp + raw[:, hh * Q_TILE:(hh + 1) * Q_TILE]

        blk = lax.broadcasted_iota(i32, (n_sel, Q_TILE), 0)
        qpos = qt * Q_TILE + lax.broadcasted_iota(i32, (n_sel, Q_TILE), 1)
        cur = lax.shift_right_logical(qpos, SEL_SHIFT)
        forced = (blk == 0) | (blk == cur) | (blk == cur - 1)
        future = blk * SEL_BLOCK > qpos
        imp = jnp.where(forced, FORCE, jnp.where(future, -FORCE, imp))
        sel = _top_rows(imp, min(SEL_TOP, n_sel))
        sb = jnp.where(sel, 0.0, NEG)
        if n_sel < LANES:
            sb = jnp.concatenate([sb, jnp.zeros((LANES - n_sel, Q_TILE), f32)], axis=0)
        sbq = sb.T.astype(bf16)
        qx.append(jnp.concatenate([qp[g], jnp.concatenate([sbq] * GROUP, axis=0)], axis=1))
    for g in groups:
        ow = _dot(vwe_ref[0, :, pl.ds(start, wk)], e_w[g])
        o_w.append(ow[:vrows] * (1.0 / ow[vrows:vrows + 1]))

    def scores(start, size, g):
        kx = jnp.concatenate([ks_ref[0, pl.ds(start, size), :], emat_ref[pl.ds(start, size), :]], axis=1)
        return _dot_nt(kx, qx[g])

    def far_step(i, carry):
        ks = pl.multiple_of(i * FAR_CHUNK, FAR_CHUNK)
        v_ext = vse_ref[0, :, pl.ds(ks, FAR_CHUNK)]
        return _flash_steps(carry, [scores(ks, FAR_CHUNK, g) for g in groups], v_ext)

    def near_step(kt, carry):
        ks = pl.multiple_of(kt * Q_TILE, Q_TILE)
        v_ext = vse_ref[0, :, pl.ds(ks, Q_TILE)]
        tile = jnp.where(kt >= qt - 1, kt - qt + n_w, 2)
        return _flash_steps(carry, [scores(ks, Q_TILE, g) + wt_ref[g, tile] for g in groups], v_ext)

    init = (jnp.full((1, cols), NEG, f32), jnp.zeros((vse_ref.shape[1], cols), f32)) * N_KV
    carry = lax.fori_loop(0, n_far, far_step, init)
    carry = lax.fori_loop(n_far * (FAR_CHUNK // Q_TILE), qt + 1, near_step, carry)

    for g in groups:
        acc = carry[2 * g + 1]
        o_s = acc[:vrows] * (1.0 / acc[vrows:vrows + 1])
        for hh in range(GROUP):
            head = g * GROUP + hh
            cs = slice(hh * Q_TILE, (hh + 1) * Q_TILE)
            ds = slice(g * HEAD_DIM, (g + 1) * HEAD_DIM)
            o = (gates_t[3 * head:3 * head + 1, :] * o_c[g][ds, cs]
                 + gates_t[3 * head + 1:3 * head + 2, :] * o_s[ds, cs]
                 + gates_t[3 * head + 2:3 * head + 3, :] * o_w[g][ds, cs])
            out_ref[0, head * HEAD_DIM:(head + 1) * HEAD_DIM, :] = o


def _attn_prompt(q_pk, gates, kcp, ks_bf, kw_bf, wt, ct):
    b, t, _ = q_pk.shape
    ncp = kcp.shape[1]
    n_sel = t // SEL_BLOCK
    n_cmp = t // CMP_STRIDE - 1
    assert t % FAR_CHUNK == 0 and n_sel % SUBLANES == 0 and n_sel <= LANES
    cols = GROUP * Q_TILE
    half = KV_W // 2

    def values_t(v):
        v_t = jnp.swapaxes(v, 1, 2)
        return jnp.concatenate([v_t, jnp.ones((b, ONES_ROWS, v.shape[1]), v.dtype)], axis=1)

    kw_pad = jnp.pad(kw_bf, ((0, 0), (WINDOW, 0), (0, 0)))
    vce, vse, vwe = values_t(kcp[:, :, half:]), values_t(ks_bf[:, :, half:]), values_t(kw_pad[:, :, half:])
    ove = jnp.concatenate([_overlap_t(n_sel, n_sel, ncp, n_cmp), jnp.ones((ONES_ROWS, ncp), bf16)], axis=0)
    emat = jnp.asarray(np.arange(t)[:, None] // SEL_BLOCK == np.arange(LANES)[None, :], dtype=bf16)
    tp = t + WINDOW
    vr = half + ONES_ROWS
    per_b = lambda shape: pl.BlockSpec((1,) + shape, lambda i, j: (i, 0, 0))
    attn_t = pl.pallas_call(
        functools.partial(_attn_prompt_kernel, n_sel=n_sel, ncp=ncp),
        grid=(b, t // Q_TILE),
        in_specs=[
            pl.BlockSpec((1, Q_TILE, N_HEADS * LANES), lambda i, j: (i, j, 0)),
            pl.BlockSpec((1, Q_TILE, LANES), lambda i, j: (i, j, 0)),
            per_b((ncp, half)), per_b((vr, ncp)),
            per_b((t, half)), per_b((vr, t)),
            per_b((tp, half)), per_b((vr, tp)),
            _resident(emat.shape), _resident(ove.shape), _resident(wt.shape), _resident(ct.shape),
        ],
        out_specs=pl.BlockSpec((1, ATTN_DIM, Q_TILE), lambda i, j: (i, 0, j)),
        out_shape=jax.ShapeDtypeStruct((b, ATTN_DIM, t), f32),
        scratch_shapes=[pltpu.VMEM((N_KV, ncp, cols), f32)],
        compiler_params=_cparams(("parallel", "arbitrary")),
        name="attn_prompt",
    )(q_pk, gates, kcp, vce, ks_bf, vse, kw_pad, vwe, emat, ove, wt, ct)
    return jnp.swapaxes(attn_t, 1, 2)


def _softmax_rows2(s_main, s_new):
    m = jnp.maximum(jnp.max(s_main, axis=1, keepdims=True), jnp.max(s_new, axis=1, keepdims=True))
    e_main, e_new = jnp.exp(s_main - m), jnp.exp(s_new - m)
    inv = 1.0 / (jnp.sum(e_main, axis=1, keepdims=True) + jnp.sum(e_new, axis=1, keepdims=True))
    return (e_main * inv).astype(bf16), (e_new * inv).astype(bf16)


def _attn_sample_kernel(*refs, n_pages, n_sel, past_len):
    page_refs = refs[1:1 + n_pages]
    (q_ref, g_ref, kcp_ref, knew_ref, win_ref, wnew_ref, ovt_ref, rmat_ref, emat_ref,
     scmp_ref, sk_ref, snew_ref, swin_ref, c31_ref, out_ref, ke_ref, vt_ref) = refs[1 + n_pages:]
    half = KV_W // 2
    q = q_ref[0]
    q_rows = q[0:SAMPLE_COLS]

    kcp = kcp_ref[0]
    pc = _masked_softmax_cols(_dot_nt(kcp[:, :half], q) + scmp_ref[...])
    o_c = _dot_tn(pc.astype(bf16), kcp[:, half:])[0:SAMPLE_COLS]
    p_hi, p_lo = _split_bf16(pc)
    imp = _dot(ovt_ref[...], p_hi) + _dot(ovt_ref[...], p_lo)
    i_hi, i_lo = _split_bf16(imp)
    imp = _dot(i_hi, rmat_ref[...]) + _dot(i_lo, rmat_ref[...])
    rows = imp.shape[0]
    blk = lax.broadcasted_iota(i32, (rows, LANES), 0)
    qpos = past_len + (lax.broadcasted_iota(i32, (rows, LANES), 1) & 3)
    cur = lax.shift_right_logical(qpos, SEL_SHIFT)
    forced = (blk == 0) | (blk == cur) | (blk == cur - 1)
    future = blk * SEL_BLOCK > qpos
    imp = jnp.where(forced, FORCE, jnp.where(future, -FORCE, imp))
    imp = jnp.where(blk < n_sel, imp, -jnp.inf)
    sel = _top_rows(imp, min(SEL_TOP, n_sel))
    sb = jnp.where(sel, 0.0, NEG)
    past_blocks = past_len // SEL_BLOCK
    sb_rows = sb[0:past_blocks].T[0:SAMPLE_COLS]

    for k in range(n_pages):
        lanes = slice(k * PAGE_SIZE, (k + 1) * PAGE_SIZE)
        ke_ref[0:half, lanes] = page_refs[k][0, 0:half, :].astype(bf16)
        vt_ref[:, lanes] = page_refs[k][0, half:, :].astype(bf16)
    ke_ref[half:, :] = emat_ref[...]
    n_far = past_len - PAGE_SIZE
    s = _dot(jnp.concatenate([q_rows, sb_rows.astype(bf16)], axis=1), ke_ref[...])
    s = jnp.concatenate([s[:, :n_far] + c31_ref[:, 0:1], s[:, n_far:] + sk_ref[...]], axis=1)
    knew = knew_ref[0].astype(bf16)
    s_n = _dot(q_rows, knew[0:half]) + snew_ref[...]
    p, p_n = _softmax_rows2(s, s_n)
    o_s = _dot_nt(p, vt_ref[...]) + _dot_nt(p_n, knew[half:])

    win = win_ref[0].astype(bf16)
    wnew = wnew_ref[0].astype(bf16)
    p, p_n = _softmax_rows2(_dot(q_rows, win[0:half]) + swin_ref[...], _dot(q_rows, wnew[0:half]) + snew_ref[...])
    o_w = _dot_nt(p, win[half:]) + _dot_nt(p_n, wnew[half:])

    gt = g_ref[0]
    out_ref[0] = gt[:, 0:1] * o_c + gt[:, 1:2] * o_s + gt[:, 2:3] * o_w


def _attn_sample(q_pk, gates, kcp, pages_t, page_table, ks_new, win_t, kw_new, tabs):
    scmp, sk, snew, swin, c31s = tabs
    bd, t_new, _ = q_pk.shape
    n_pages = page_table.shape[1]
    past_len = n_pages * PAGE_SIZE
    ncp = kcp.shape[1]
    n_cmp = past_len // CMP_STRIDE - 1
    n_sel = -(-(past_len + t_new) // SEL_BLOCK)
    n_sel_rows = -(-n_sel // SUBLANES) * SUBLANES
    past_blocks = past_len // SEL_BLOCK
    n_cols = N_KV * GROUP * t_new
    assert n_cols == SAMPLE_COLS and win_t.shape[2] == WINDOW and past_blocks == LANES
    qc = q_pk.reshape(bd, t_new, N_HEADS, LANES).transpose(0, 2, 1, 3).reshape(bd, n_cols, LANES)
    qc = jnp.pad(qc, ((0, 0), (0, LANES - n_cols), (0, 0)))
    gc = gates[:, :, :3 * N_HEADS].reshape(bd, t_new, N_HEADS, 3).transpose(0, 2, 1, 3).reshape(bd, n_cols, 3)
    gc = jnp.pad(gc, ((0, 0), (0, 0), (0, SUBLANES - 3)))
    new_t = lambda a: jnp.pad(jnp.swapaxes(a, 1, 2), ((0, 0), (0, 0), (0, LANES - t_new)))
    ovt = _overlap_t(n_sel_rows, n_sel, ncp, n_cmp)
    c = np.arange(LANES)
    same = (c[:, None] // (GROUP * t_new) == c[None, :] // (GROUP * t_new)) & (c[:, None] % t_new == c[None, :] % t_new)
    rmat = jnp.asarray(same & (c[:, None] < n_cols) & (c[None, :] < n_cols), dtype=bf16)
    emat = jnp.asarray(np.arange(past_blocks)[:, None] == np.arange(past_len)[None, :] // SEL_BLOCK, dtype=bf16)

    def page_spec(k):
        return pl.BlockSpec((1, KV_W, PAGE_SIZE), lambda i, pt: (pt[i, k], 0, 0))

    per_b = lambda shape: pl.BlockSpec((1,) + shape, lambda i, pt: (i, 0, 0))
    res = lambda a: pl.BlockSpec(a.shape, lambda i, pt: (0,) * a.ndim)
    half = KV_W // 2
    grid_spec = pltpu.PrefetchScalarGridSpec(
        num_scalar_prefetch=1,
        grid=(bd,),
        in_specs=[page_spec(k) for k in range(n_pages)] + [
            per_b((LANES, LANES)), per_b((n_cols, SUBLANES)), per_b((ncp, KV_W)), per_b((KV_W, LANES)),
            per_b((KV_W, WINDOW)), per_b((KV_W, LANES)),
            res(ovt), res(rmat), res(emat), res(scmp), res(sk), res(snew), res(swin), res(c31s)],
        out_specs=per_b((n_cols, LANES)),
        scratch_shapes=[pltpu.VMEM((half + past_blocks, past_len), bf16), pltpu.VMEM((half, past_len), bf16)],
    )
    o = pl.pallas_call(
        functools.partial(_attn_sample_kernel, n_pages=n_pages, n_sel=n_sel, past_len=past_len),
        grid_spec=grid_spec,
        out_shape=jax.ShapeDtypeStruct((bd, n_cols, LANES), f32),
        compiler_params=_cparams(("parallel",)),
        name="attn_sample",
    )(page_table, *([pages_t] * n_pages), qc, gc, kcp, new_t(ks_new), win_t, new_t(kw_new),
      ovt, rmat, emat, scmp, sk, snew, swin, c31s)
    o = o.reshape(bd, N_KV, GROUP, t_new, N_KV, HEAD_DIM)
    o = jnp.stack([o[:, g, :, :, g, :] for g in range(N_KV)], axis=1)
    return o.transpose(0, 3, 1, 2, 4).reshape(bd, t_new, ATTN_DIM)


def _merge_kernel(x_ref, a_ref, u0_ref, u1_ref, u2_ref, cb_ref, mg_ref, cw_ref, cbias_ref,
                  wau_ref, wcu_ref, wo_ref, g2_ref, x2_ref, h2_ref, *, d_model):
    y = cbias_ref[...] + cw_ref[0:1, :] * u2_ref[...]
    y = y + cw_ref[1:2, :] * u1_ref[...]
    y = y + cw_ref[2:3, :] * u0_ref[...]
    up_a = _dot(a_ref[...].astype(bf16), wau_ref[...])
    up_c = _dot((cb_ref[...] * y).astype(bf16), wcu_ref[...])
    mixed = mg_ref[:, :d_model] * up_a + mg_ref[:, d_model:] * up_c
    x2 = x_ref[...] + _dot(mixed.astype(bf16), wo_ref[...])
    x2_ref[...] = x2
    h2_ref[...] = _rmsnorm(x2, g2_ref[...]).astype(bf16)


def _merge(x, attn, u0, u1, u2, cb, mg, conv_w, conv_b, w_attn_up, w_conv_up, w_out, norm2_g):
    n, d = x.shape
    tm = ROW_TILE
    row = lambda w: pl.BlockSpec((tm, w), lambda i: (i, 0))
    cw = jnp.pad(conv_w, ((0, SUBLANES - CONV_W), (0, 0)))
    return pl.pallas_call(
        functools.partial(_merge_kernel, d_model=d),
        grid=(n // tm,),
        in_specs=[row(d), row(ATTN_DIM), row(CONV_DIM), row(CONV_DIM), row(CONV_DIM), row(CONV_DIM), row(2 * d),
                  _resident(cw.shape), _resident((1, CONV_DIM)), _resident(w_attn_up.shape),
                  _resident(w_conv_up.shape), _resident(w_out.shape), _resident((1, d))],
        out_specs=[row(d), row(d)],
        out_shape=[jax.ShapeDtypeStruct((n, d), f32), jax.ShapeDtypeStruct((n, d), bf16)],
        compiler_params=_cparams(("parallel",)),
        name="merge",
    )(x, attn, u0, u1, u2, cb, mg, cw, conv_b.reshape(1, CONV_DIM), w_attn_up.astype(bf16),
      w_conv_up.astype(bf16), w_out.astype(bf16), norm2_g.reshape(1, d))


def _top_rows_sorted(s, k):
    rows = lax.broadcasted_iota(i32, s.shape, 0).astype(f32)
    vals, idxs = [], []
    for _ in range(k):
        m = jnp.max(s, axis=0, keepdims=True)
        first = jnp.min(jnp.where(s == m, rows, 1e9), axis=0, keepdims=True)
        vals.append(m)
        idxs.append(first)
        s = jnp.where(rows == first, -jnp.inf, s)
    return jnp.concatenate(vals, axis=0), jnp.concatenate(idxs, axis=0)


def _peer_route_kernel(h_ref, wq_ref, k1_ref, k2_ref, a_ref, b_ref, g_ref):
    qp = _dot(h_ref[...], wq_ref[...]).astype(bf16)
    key_dim = 2 * PEER_HALF
    a_rows, b_rows, g_rows = [], [], []
    for h in range(PEER_HEADS):
        q1 = qp[:, h * key_dim:h * key_dim + PEER_HALF]
        q2 = qp[:, h * key_dim + PEER_HALF:(h + 1) * key_dim]
        v1, i1 = _top_rows_sorted(_dot_nt(k1_ref[...], q1), PEER_TOPK)
        v2, i2 = _top_rows_sorted(_dot_nt(k2_ref[...], q2), PEER_TOPK)
        counts = [PEER_TOPK // (r + 1) for r in range(PEER_TOPK)]
        cand = jnp.concatenate([v1[r:r + 1, :] + v2[0:n, :] for r, n in enumerate(counts)], axis=0)
        code = jnp.concatenate([i1[r:r + 1, :] * PEER_NKEYS + i2[0:n, :] for r, n in enumerate(counts)], axis=0)
        pad = -sum(counts) % SUBLANES
        cand = jnp.concatenate([cand, jnp.full((pad, cand.shape[1]), -jnp.inf, f32)], axis=0)
        code = jnp.concatenate([code, jnp.zeros((pad, code.shape[1]), f32)], axis=0)
        rows = lax.broadcasted_iota(i32, cand.shape, 0).astype(f32)
        sc, ex = [], []
        for _ in range(PEER_TOPK):
            m = jnp.max(cand, axis=0, keepdims=True)
            first = jnp.min(jnp.where(cand == m, rows, 1e9), axis=0, keepdims=True)
            pick = rows == first
            sc.append(m)
            ex.append(jnp.max(jnp.where(pick, code, -1.0), axis=0, keepdims=True))
            cand = jnp.where(pick, -jnp.inf, cand)
        sc = jnp.concatenate(sc, axis=0)
        ex = jnp.concatenate(ex, axis=0)
        e1 = jnp.floor(ex * (1.0 / PEER_NKEYS))
        e = jnp.exp(sc - sc[0:1, :])
        a_rows.append(e1)
        b_rows.append(ex - e1 * PEER_NKEYS)
        g_rows.append(e / jnp.sum(e, axis=0, keepdims=True))
    tn = h_ref.shape[0]
    for src, dst in ((a_rows, a_ref), (b_rows, b_ref), (g_rows, g_ref)):
        full = jnp.concatenate(src, axis=0)
        for c in range(tn // LANES):
            dst[c * LANES:(c + 1) * LANES, :] = full[:, c * LANES:(c + 1) * LANES].T


def _peer_route(h2, wq, k1, k2):
    n, d = h2.shape
    tn = ROW_TILE
    assert PEER_HEADS * PEER_TOPK == LANES
    row = lambda w: pl.BlockSpec((tn, w), lambda i: (i, 0))
    return pl.pallas_call(
        _peer_route_kernel,
        grid=(n // tn,),
        in_specs=[row(d), _resident(wq.shape), _resident(k1.shape), _resident(k2.shape)],
        out_specs=[row(LANES)] * 3,
        out_shape=[jax.ShapeDtypeStruct((n, LANES), f32)] * 3,
        compiler_params=_cparams(("parallel",)),
        name="peer_route",
    )(h2, wq.astype(bf16), k1.astype(bf16), k2.astype(bf16))


def _peer_expert_kernel(h_ref, a_ref, b_ref, w_ref, x_ref, u_ref, v_ref, out_ref, g_ref, acc_ref, *, pitch):
    e_step = pl.program_id(1)
    tn = h_ref.shape[0]
    keys_per_chunk = EXPERT_CHUNK // PEER_NKEYS
    half_keys = PEER_NKEYS // 2
    hi_mask = jnp.uint32(0xFFFF0000)

    @pl.when(e_step == 0)
    def _build_gate_matrix():
        key_row = lax.broadcasted_iota(i32, (PEER_NKEYS, LANES), 0).astype(f32)

        def token(n, c):
            a = jnp.broadcast_to(a_ref[pl.ds(n, 1), :], (PEER_NKEYS, LANES))
            b = jnp.broadcast_to(b_ref[pl.ds(n, 1), :], (PEER_NKEYS, LANES))
            w = jnp.broadcast_to(w_ref[pl.ds(n, 1), :], (PEER_NKEYS, LANES))
            pa = (a == key_row).astype(bf16)
            wb = jnp.where(b == key_row, w, 0.0).astype(bf16)
            g = _dot_nt(pa, wb).astype(bf16).astype(f32)
            bits = lax.bitcast_convert_type(g, jnp.uint32)
            word = lax.shift_right_logical(bits[:half_keys], jnp.uint32(16)) | (bits[half_keys:] & hi_mask)
            g_ref[pl.ds(n, half_keys, stride=pitch), :] = word
            return c

        lax.fori_loop(0, tn, token, 0, unroll=TOKEN_UNROLL)
        acc_ref[...] = jnp.zeros_like(acc_ref)

    s = _dot_nt(h_ref[...], u_ref[...])
    key0 = (e_step * keys_per_chunk) % half_keys
    words = jnp.concatenate(
        [g_ref[pl.ds(pl.multiple_of((key0 + j) * pitch, SUBLANES), tn), :] for j in range(keys_per_chunk)], axis=1)
    shift = jnp.where(e_step * keys_per_chunk < half_keys, 16, 0).astype(jnp.uint32)
    gates = lax.bitcast_convert_type(lax.shift_left(words, shift) & hi_mask, f32)
    act = 0.5 * s * (1.0 + lax.erf(s * math.sqrt(0.5)))
    acc_ref[...] += _dot((gates * act).astype(bf16), v_ref[...])

    @pl.when(e_step == pl.num_programs(1) - 1)
    def _finish():
        out_ref[...] = x_ref[...] + acc_ref[...]


def _peer_expert(h2, a_idx, b_idx, gate_w, x2, u_tab, v_tab):
    n, d = h2.shape
    n_exp = u_tab.shape[0]
    tn = PEER_TILE
    pitch = tn + G_PITCH_PAD
    assert n % tn == 0 and n_exp == PEER_NKEYS * PEER_NKEYS and (PEER_NKEYS // 2) % (EXPERT_CHUNK // PEER_NKEYS) == 0
    row = lambda w: pl.BlockSpec((tn, w), lambda i, e: (i, 0))
    tab = pl.BlockSpec((EXPERT_CHUNK, d), lambda i, e: (e, 0))
    return pl.pallas_call(
        functools.partial(_peer_expert_kernel, pitch=pitch),
        grid=(n // tn, n_exp // EXPERT_CHUNK),
        in_specs=[row(d), row(LANES), row(LANES), row(LANES), row(d), tab, tab],
        out_specs=row(d),
        out_shape=jax.ShapeDtypeStruct((n, d), f32),
        scratch_shapes=[pltpu.VMEM((PEER_NKEYS // 2 * pitch, LANES), jnp.uint32), pltpu.VMEM((tn, d), f32)],
        compiler_params=_cparams(("parallel", "arbitrary")),
        name="peer_expert",
    )(h2, a_idx, b_idx, gate_w, x2, u_tab.astype(bf16), v_tab.astype(bf16))


def _ple_kernel(x_ref, p_ref, pg_ref, wg_ref, wp_ref, fg_ref, y_ref):
    x = x_ref[...]
    gate = jax.nn.sigmoid(_dot(_rmsnorm(x, pg_ref[...]).astype(bf16), wg_ref[...]))
    x = x + gate * _dot(p_ref[...].astype(bf16), wp_ref[...])
    y_ref[...] = _rmsnorm(x, fg_ref[...])


def _ple(x3, p, ple_g, w_ple_gate, w_ple_proj, final_g):
    n, d = x3.shape
    tm = ROW_TILE
    row = lambda w: pl.BlockSpec((tm, w), lambda i: (i, 0))
    return pl.pallas_call(
        _ple_kernel,
        grid=(n // tm,),
        in_specs=[row(d), row(p.shape[1]), _resident((1, d)), _resident(w_ple_gate.shape),
                  _resident(w_ple_proj.shape), _resident((1, d))],
        out_specs=row(d),
        out_shape=jax.ShapeDtypeStruct((n, d), f32),
        compiler_params=_cparams(("parallel",)),
        name="ple_final",
    )(x3, p, ple_g.reshape(1, d), w_ple_gate.astype(bf16), w_ple_proj.astype(bf16), final_g.reshape(1, d))


def kernel(x_prompt, x_sample, p_prompt, p_sample, cache_cmp_kv, cache_slc_kv, page_table, state_win_kv, state_conv, norm1_g, w_in, w_cmp, cmp_pos, conv_w, conv_b, w_attn_up, w_conv_up, w_out, norm2_g, peer_wq, peer_k1, peer_k2, peer_u, peer_v, ple_g, w_ple_gate, w_ple_proj, rel_bias, final_g):
    depth = norm1_g.shape[0]
    assert depth == 1, "single-layer trunk"
    b, t, d = x_prompt.shape
    bd, t_new, _ = x_sample.shape
    n_pages = page_table.shape[1]
    past_len = n_pages * PAGE_SIZE
    n_p, n_s = b * t, bd * t_new
    kv_shape = lambda lead: lead + (2, N_KV, HEAD_DIM)

    x_all = jnp.concatenate([x_prompt.reshape(n_p, d), x_sample.reshape(n_s, d)], axis=0)
    w_pack = _pack_w_in(w_in[0], d)
    q_pk, kvc, kvs, kvw, kvs_bf, kvw_bf, gates, cin, cbg, mg = _in_proj(x_all, norm1_g[0], w_pack)
    split = lambda a: (a[:n_p].reshape(b, t, -1), a[n_p:].reshape(bd, t_new, -1))

    w4, pos_rows = _pack_w_cmp(w_cmp[0], cmp_pos[0])
    kvc_p, kvc_s = split(kvc)
    kcp_p = _compress_prompt(kvc_p, w4, pos_rows)
    kcp_s = _compress_sample(_pages_t(cache_cmp_kv[0]), page_table, w4, pos_rows)
    n_cmp_s = past_len // CMP_STRIDE - 1
    wt, ct, *sample_tabs = _tables(rel_bias, past_len, kcp_s.shape[1], n_cmp_s)

    q_p, q_s = split(q_pk)
    g_p, g_s = split(gates)
    kvs_p, kvs_s = split(kvs)
    kvw_p, kvw_s = split(kvw)
    ksb_p, _ = split(kvs_bf)
    kwb_p, _ = split(kvw_bf)
    attn_p = _attn_prompt(q_p, g_p, kcp_p, ksb_p, kwb_p, wt, ct)
    win_t = jnp.transpose(state_win_kv[0], (0, 2, 3, 4, 1)).reshape(bd, KV_W, -1)
    attn_s = _attn_sample(q_s, g_s, kcp_s, _pages_t(cache_slc_kv[0]), page_table, kvs_s, win_t, kvw_s, sample_tabs)
    attn = jnp.concatenate([attn_p.reshape(n_p, ATTN_DIM), attn_s.reshape(n_s, ATTN_DIM)], axis=0)

    cin_p, cin_s = split(cin)
    buf_p = jnp.zeros((b, CONV_W - 1, CONV_DIM), f32)
    buf_s = state_conv[0]

    def shifted(u, buf, k):
        return jnp.concatenate([buf[:, CONV_W - 1 - k:], u[:, :u.shape[1] - k]], axis=1).reshape(-1, CONV_DIM)

    u1 = jnp.concatenate([shifted(cin_p, buf_p, 1), shifted(cin_s, buf_s, 1)], axis=0)
    u2 = jnp.concatenate([shifted(cin_p, buf_p, 2), shifted(cin_s, buf_s, 2)], axis=0)
    x2, h2 = _merge(x_all, attn, cin, u1, u2, cbg, mg, conv_w[0], conv_b[0], w_attn_up[0], w_conv_up[0],
                    w_out[0], norm2_g[0])

    a_idx, b_idx, gate_w = _peer_route(h2, peer_wq[0], peer_k1[0], peer_k2[0])
    x3 = _peer_expert(h2, a_idx, b_idx, gate_w, x2, peer_u[0], peer_v[0])

    p_all = jnp.concatenate([p_prompt[0].reshape(n_p, -1), p_sample[0].reshape(n_s, -1)], axis=0)
    y = _ple(x3, p_all, ple_g[0], w_ple_gate[0], w_ple_proj[0], final_g)

    conv_tail = lambda u, buf: jnp.concatenate([buf, u], axis=1)[:, -(CONV_W - 1):]
    n_win = min(WINDOW, t)
    new_win_s = jnp.concatenate([state_win_kv[:, :, t_new:], kvw_s.reshape(kv_shape((1, bd, t_new)))], axis=2)
    return (
        y[:n_p].reshape(b, t, d),
        y[n_p:].reshape(bd, t_new, d),
        kvc_p.reshape(kv_shape((1, b, t))),
        kvc_s.reshape(kv_shape((1, bd, t_new))),
        kvs_p.reshape(kv_shape((1, b, t))),
        kvs_s.reshape(kv_shape((1, bd, t_new))),
        kvw_p[:, t - n_win:].reshape(kv_shape((1, b, n_win))),
        new_win_s,
        conv_tail(cin_p, buf_p)[None],
        conv_tail(cin_s, buf_s)[None],
    )
```

```python
import functools
import math

import jax
import jax.numpy as jnp
import numpy as np
from jax import lax
from jax.experimental import pallas as pl
from jax.experimental.pallas import tpu as pltpu

f32 = jnp.float32
bf16 = jnp.bfloat16
i32 = jnp.int32

N_HEADS = 8
N_KV = 2
GROUP = N_HEADS // N_KV
HEAD_DIM = 64
ATTN_DIM = N_HEADS * HEAD_DIM
KV_W = 2 * N_KV * HEAD_DIM
CMP_BLOCK = 32
CMP_STRIDE = 16
SEL_BLOCK = 64
SEL_SHIFT = 6
SEL_TOP = 16
WINDOW = 512
N_BUCKETS = 32
MAX_EXACT = N_BUCKETS // 2
MAX_DISTANCE = 128
CONV_DIM = 512
CONV_W = 3
PEER_HEADS = 8
PEER_NKEYS = 128
PEER_HALF = 128
PEER_TOPK = 16
PAGE_SIZE = 128
EPS = 1e-6
NEG = -1e30
FORCE = 1e4

LANES = 128
SUBLANES = 8
VMEM_LIMIT_BYTES = 56 * 1024 * 1024

Q_TILE = 128
SAMPLE_COLS = 32
FAR_CHUNK = 512
CMP_PAD = 16
CMP_WIN = 24
ONES_ROWS = 16
ROW_TILE = 256
PEER_TILE = 512
EXPERT_CHUNK = 1024
TOKEN_UNROLL = 16
G_PITCH_PAD = 8


def _cparams(sem):
    return pltpu.CompilerParams(dimension_semantics=sem, vmem_limit_bytes=VMEM_LIMIT_BYTES)


def _dot(a, b):
    return jnp.dot(a, b, preferred_element_type=f32)


def _dot_nt(a, b):
    return lax.dot_general(a, b, (((1,), (1,)), ((), ())), preferred_element_type=f32)


def _dot_tn(a, b):
    return lax.dot_general(a, b, (((0,), (0,)), ((), ())), preferred_element_type=f32)


def _split_bf16(x):
    hi = x.astype(bf16)
    lo = (x - hi.astype(f32)).astype(bf16)
    return hi, lo


def _resident(shape):
    nd = len(shape)
    return pl.BlockSpec(shape, lambda *_: (0,) * nd)


def _rmsnorm(x, g):
    ms = jnp.mean(x * x, axis=-1, keepdims=True)
    return x * lax.rsqrt(ms + EPS) * g


def _in_proj_kernel(x_ref, g_ref, w_ref, q_ref, kc_ref, ks_ref, kw_ref, ksb_ref, kwb_ref,
                    gn_ref, cin_ref, cb_ref, mg_ref, *, d_model):
    qw = N_HEADS * LANES
    h = _rmsnorm(x_ref[...], g_ref[...]).astype(bf16)
    o = 0
    q_ref[...] = (_dot(h, w_ref[:, o:o + qw]) * (HEAD_DIM ** -0.5)).astype(bf16)
    o += qw
    kc_ref[...] = _dot(h, w_ref[:, o:o + KV_W])
    o += KV_W
    ks = _dot(h, w_ref[:, o:o + KV_W])
    ks_ref[...] = ks
    ksb_ref[...] = ks.astype(bf16)
    o += KV_W
    kw = _dot(h, w_ref[:, o:o + KV_W])
    kw_ref[...] = kw
    kwb_ref[...] = kw.astype(bf16)
    o += KV_W
    gn_ref[...] = jax.nn.sigmoid(_dot(h, w_ref[:, o:o + LANES]))
    o += LANES
    cv = _dot(h, w_ref[:, o:o + CONV_DIM])
    o += CONV_DIM
    cb_ref[...] = _dot(h, w_ref[:, o:o + CONV_DIM])
    o += CONV_DIM
    cin_ref[...] = _dot(h, w_ref[:, o:o + CONV_DIM]) * cv
    o += CONV_DIM
    mg_ref[...] = jax.nn.sigmoid(_dot(h, w_ref[:, o:o + 2 * d_model]))


def _pack_w_in(w_in, d_model):
    sizes = (ATTN_DIM, KV_W, KV_W, KV_W, 3 * N_HEADS, CONV_DIM, CONV_DIM, CONV_DIM, 2 * d_model)
    parts, s = [], 0
    for n in sizes:
        parts.append(w_in[:, s:s + n])
        s += n
    wq = parts[0].reshape(d_model, N_KV, GROUP, HEAD_DIM)
    z = jnp.zeros_like(wq)
    slabs = [jnp.concatenate([wq[:, 0], z[:, 0]], axis=-1), jnp.concatenate([z[:, 1], wq[:, 1]], axis=-1)]
    wq = jnp.stack(slabs, axis=1).reshape(d_model, N_HEADS * LANES)
    gn = jnp.pad(parts[4], ((0, 0), (0, LANES - 3 * N_HEADS)))
    return jnp.concatenate([wq, parts[1], parts[2], parts[3], gn] + parts[5:], axis=1).astype(bf16)


def _in_proj(x, norm_g, w_pack):
    n, d = x.shape
    tm = ROW_TILE
    assert n % tm == 0
    row = lambda w: pl.BlockSpec((tm, w), lambda i: (i, 0))
    widths = (N_HEADS * LANES, KV_W, KV_W, KV_W, KV_W, KV_W, LANES, CONV_DIM, CONV_DIM, 2 * d)
    dtypes = (bf16, f32, f32, f32, bf16, bf16, f32, f32, f32, f32)
    return pl.pallas_call(
        functools.partial(_in_proj_kernel, d_model=d),
        grid=(n // tm,),
        in_specs=[row(d), _resident((1, d)), _resident(w_pack.shape)],
        out_specs=[row(w) for w in widths],
        out_shape=[jax.ShapeDtypeStruct((n, w), t) for w, t in zip(widths, dtypes)],
        compiler_params=_cparams(("parallel",)),
        name="in_proj",
    )(x, norm_g.reshape(1, d), w_pack)


def _bias_of_dist(dist, valid, rb_ref, head, shift=0.0):
    n = jnp.maximum(dist, 0)
    nf = jnp.maximum(n, 1).astype(f32)
    large = MAX_EXACT + (jnp.log(nf / MAX_EXACT) / math.log(MAX_DISTANCE / MAX_EXACT)
                         * (N_BUCKETS - MAX_EXACT)).astype(i32)
    large = jnp.minimum(large, N_BUCKETS - 1)
    bucket = jnp.where(n < MAX_EXACT, n, large)
    out = jnp.zeros(dist.shape, f32)
    for b in range(N_BUCKETS):
        out = jnp.where(bucket == b, rb_ref[b, head], out)
    return jnp.where(valid, out - shift, NEG)


def _tables_kernel(rb_ref, wt_ref, ct_ref, scmp_ref, sk_ref, snew_ref, swin_ref, c31s_ref,
                   *, past_len, ncp_s, n_cmp_s):
    ik = lax.broadcasted_iota(i32, (Q_TILE, Q_TILE), 0)
    iq = lax.broadcasted_iota(i32, (Q_TILE, Q_TILE), 1)
    n_w = WINDOW // Q_TILE + 1
    for g in range(N_KV):
        wt_ref[g, 0] = jnp.full(wt_ref.shape[2:], NEG, f32)
        for hh in range(GROUP):
            head = g * GROUP + hh
            far = rb_ref[N_BUCKETS - 1, head]
            cols = slice(hh * Q_TILE, (hh + 1) * Q_TILE)
            for w in range(n_w):
                dist = Q_TILE * (n_w - 1 - w) + iq - ik
                wt_ref[g, w + 1, :, cols] = _bias_of_dist(dist, (dist >= 0) & (dist < WINDOW), rb_ref, head, far)
            mm = lax.broadcasted_iota(i32, (CMP_WIN, Q_TILE), 0)
            jq = lax.broadcasted_iota(i32, (CMP_WIN, Q_TILE), 1)
            dist = jq - CMP_STRIDE * (mm - CMP_PAD) - (CMP_BLOCK - 1)
            ct_ref[g, :, cols] = _bias_of_dist(dist, dist >= 0, rb_ref, head, far)
    def sample_table(shape, q_axis, dist_fn, valid_fn, out_ref):
        c = lax.broadcasted_iota(i32, shape, q_axis)
        r = lax.broadcasted_iota(i32, shape, 1 - q_axis)
        t = c & 3
        hd = jnp.minimum(lax.shift_right_logical(c, 2), N_HEADS - 1)
        dist = dist_fn(r, t)
        valid = valid_fn(r, t, dist)
        acc = jnp.zeros(shape, f32)
        for head in range(N_HEADS):
            acc = jnp.where(hd == head, _bias_of_dist(dist, valid, rb_ref, head), acc)
        out_ref[...] = acc

    sample_table((ncp_s, LANES), 1, lambda r, t: past_len + t - CMP_STRIDE * (r - CMP_PAD) - (CMP_BLOCK - 1),
                 lambda r, t, d: (r >= CMP_PAD) & (r < CMP_PAD + n_cmp_s) & (d >= 0), scmp_ref)
    sample_table((SAMPLE_COLS, PAGE_SIZE), 0, lambda r, t: PAGE_SIZE + t - r, lambda r, t, d: d >= 0, sk_ref)
    sample_table((SAMPLE_COLS, LANES), 0, lambda r, t: t - r, lambda r, t, d: (d >= 0) & (r < 4), snew_ref)
    sample_table((SAMPLE_COLS, WINDOW), 0, lambda r, t: WINDOW + t - r, lambda r, t, d: d < WINDOW, swin_ref)
    sample_table((SAMPLE_COLS, LANES), 0, lambda r, t: jnp.full_like(r, MAX_DISTANCE), lambda r, t, d: d > 0, c31s_ref)


def _tables(rel_bias, past_len, ncp_s, n_cmp_s):
    n_w = WINDOW // Q_TILE + 1
    cols = GROUP * Q_TILE
    shapes = [
        (N_KV, n_w + 1, Q_TILE, cols), (N_KV, CMP_WIN, cols),
        (ncp_s, LANES), (SAMPLE_COLS, PAGE_SIZE), (SAMPLE_COLS, LANES), (SAMPLE_COLS, WINDOW), (SAMPLE_COLS, LANES),
    ]
    return pl.pallas_call(
        functools.partial(_tables_kernel, past_len=past_len, ncp_s=ncp_s, n_cmp_s=n_cmp_s),
        in_specs=[pl.BlockSpec(memory_space=pltpu.SMEM)],
        out_shape=[jax.ShapeDtypeStruct(s, f32) for s in shapes],
        compiler_params=pltpu.CompilerParams(vmem_limit_bytes=VMEM_LIMIT_BYTES),
        name="bias_tables",
    )(rel_bias)


def _ncp(n_chunks):
    return -(-(n_chunks + CMP_PAD + SUBLANES) // LANES) * LANES


def _compress_core(chunk_rows, w_ref, pos_ref, out_ref, n_chunks):
    half = KV_W // 2
    out_ref[...] = jnp.zeros(out_ref.shape, out_ref.dtype)
    for c in range(2):
        fs = None
        pb = None
        for p in range(CMP_STRIDE // 2):
            x = jnp.concatenate([chunk_rows(2 * p, c), chunk_rows(2 * p + 1, c)], axis=1).astype(bf16)
            lanes = [slice((2 * p + k) * KV_W + c * half, (2 * p + k) * KV_W + (c + 1) * half) for k in range(2)]
            px = jnp.concatenate([pos_ref[:, lanes[0]], pos_ref[:, lanes[1]]], axis=1)
            fs = _dot(x, w_ref[p, c]) if fs is None else fs + _dot(x, w_ref[p, c])
            pb = _dot(px, w_ref[p, c]) if pb is None else pb + _dot(px, w_ref[p, c])
        bias = pb[0:1, :half] + pb[1:2, half:]
        nxt = pltpu.roll(fs[:, half:], n_chunks - 1, axis=0)
        kc = fs[:, :half] + nxt + bias
        r = lax.broadcasted_iota(i32, kc.shape, 0)
        kc = jnp.where(r < n_chunks - 1, kc, 0.0)
        out_ref[0, CMP_PAD:CMP_PAD + n_chunks, c * half:(c + 1) * half] = kc.astype(out_ref.dtype)


def _compress_prompt_kernel(x_ref, w_ref, pos_ref, out_ref, *, n_chunks):
    half = KV_W // 2
    rows = lambda l, c: x_ref[:, l * KV_W + c * half:l * KV_W + (c + 1) * half]
    _compress_core(rows, w_ref, pos_ref, out_ref, n_chunks)


def _compress_paged_kernel(*refs, n_pages, n_chunks):
    page_refs = refs[1:1 + n_pages]
    perm_ref, w_ref, pos_ref, out_ref, xp_ref = refs[1 + n_pages:]
    half = KV_W // 2
    per_page = PAGE_SIZE // CMP_STRIDE
    for k in range(n_pages):
        xp_ref[k] = _dot_nt(perm_ref[...], page_refs[k][0].astype(bf16))

    def rows(l, c):
        x = xp_ref[:, l * per_page:(l + 1) * per_page, c * half:(c + 1) * half]
        return x.reshape(n_pages * per_page, half)

    _compress_core(rows, w_ref, pos_ref, out_ref, n_chunks)


def _pack_w_cmp(w_cmp, cmp_pos):
    w = w_cmp.reshape(2, CMP_STRIDE // 2, 2, 2, HEAD_DIM, HEAD_DIM)
    w4 = jnp.einsum("hplcde,gG->pclgdhGe", w, jnp.eye(N_KV, dtype=f32))
    w4 = w4.reshape(CMP_STRIDE // 2, 2, KV_W, KV_W)

    def pos_row(p):
        return jnp.broadcast_to(p[:, :, None, :], (CMP_STRIDE, 2, N_KV, HEAD_DIM)).reshape(1, -1)

    pos = jnp.concatenate([pos_row(cmp_pos[:CMP_STRIDE]), pos_row(cmp_pos[CMP_STRIDE:]),
                           jnp.zeros((SUBLANES - 2, CMP_STRIDE * KV_W), f32)], axis=0)
    return w4.astype(bf16), pos.astype(bf16)


def _compress_prompt(kv_cmp, w4, pos, b, t):
    n_chunks = t // CMP_STRIDE
    cw = CMP_STRIDE * KV_W
    chunks = kv_cmp.reshape(-1, cw)
    ncp = _ncp(n_chunks)
    return pl.pallas_call(
        functools.partial(_compress_prompt_kernel, n_chunks=n_chunks),
        grid=(b,),
        in_specs=[pl.BlockSpec((n_chunks, cw), lambda i: (i, 0)), _resident(w4.shape), _resident(pos.shape)],
        out_specs=pl.BlockSpec((1, ncp, KV_W), lambda i: (i, 0, 0)),
        out_shape=jax.ShapeDtypeStruct((b, ncp, KV_W), bf16),
        compiler_params=_cparams(("parallel",)),
        name="compress_prompt",
    )(chunks, w4, pos)


def _pages_t(cache):
    n_pool = cache.shape[0]
    return jnp.transpose(cache, (0, 2, 3, 4, 1)).reshape(n_pool, KV_W, cache.shape[1])


def _compress_sample(pages_t, page_table, w4, pos):
    bd, n_pages = page_table.shape
    per_page = PAGE_SIZE // CMP_STRIDE
    n_chunks = n_pages * per_page
    ncp = _ncp(n_chunks)
    tok = np.arange(PAGE_SIZE)
    perm = jnp.asarray(tok[None, :] == (tok[:, None] % per_page) * CMP_STRIDE + tok[:, None] // per_page, dtype=bf16)

    def page_spec(k):
        return pl.BlockSpec((1, KV_W, PAGE_SIZE), lambda i, pt: (pt[i, k], 0, 0))

    res = lambda a: pl.BlockSpec(a.shape, lambda i, pt: (0,) * a.ndim)
    grid_spec = pltpu.PrefetchScalarGridSpec(
        num_scalar_prefetch=1,
        grid=(bd,),
        in_specs=[page_spec(k) for k in range(n_pages)] + [res(perm), res(w4), res(pos)],
        out_specs=pl.BlockSpec((1, ncp, KV_W), lambda i, pt: (i, 0, 0)),
        scratch_shapes=[pltpu.VMEM((n_pages, PAGE_SIZE, KV_W), f32)],
    )
    return pl.pallas_call(
        functools.partial(_compress_paged_kernel, n_pages=n_pages, n_chunks=n_chunks),
        grid_spec=grid_spec,
        out_shape=jax.ShapeDtypeStruct((bd, ncp, KV_W), bf16),
        compiler_params=_cparams(("parallel",)),
        name="compress_sample",
    )(page_table, *([pages_t] * n_pages), perm, w4, pos)


def _overlap_t(n_sel_rows, n_sel, ncp, n_cmp):
    j = np.arange(n_sel_rows)[:, None]
    n = np.arange(ncp)[None, :] - CMP_PAD
    hit = (n * CMP_STRIDE < (j + 1) * SEL_BLOCK) & (n * CMP_STRIDE + CMP_BLOCK > j * SEL_BLOCK)
    hit &= (n >= 0) & (n < n_cmp) & (j < n_sel)
    return jnp.asarray(hit, dtype=bf16)


def _masked_softmax_cols(s, maybe_empty=True):
    m = jnp.max(s, axis=0, keepdims=True)
    e = jnp.exp(s - m)
    inv = 1.0 / jnp.sum(e, axis=0, keepdims=True)
    if maybe_empty:
        inv = jnp.where(m > 0.5 * NEG, inv, 0.0)
    return e * inv


def _top_rows(imp, k):
    rows = lax.broadcasted_iota(i32, imp.shape, 0).astype(f32)
    sel = jnp.zeros(imp.shape, jnp.bool_)
    v = imp
    for _ in range(k):
        m = jnp.max(v, axis=0, keepdims=True)
        first = jnp.min(jnp.where(v == m, rows, 1e9), axis=0, keepdims=True)
        pick = rows == first
        sel = sel | pick
        v = jnp.where(pick, -jnp.inf, v)
    return sel


def _flash_steps(carry, scores, v_ext):
    m_new = [jnp.maximum(carry[2 * i], jnp.max(s, axis=0, keepdims=True)) for i, s in enumerate(scores)]
    p = [jnp.exp(s - m).astype(bf16) for s, m in zip(scores, m_new)]
    out = []
    for i in range(len(scores)):
        alpha = jnp.exp(carry[2 * i] - m_new[i])
        out.extend((m_new[i], alpha * carry[2 * i + 1] + _dot(v_ext, p[i])))
    return tuple(out)


def _attn_prompt_kernel(q_ref, gn_ref, kc_ref, vce_ref, ks_ref, vse_ref, kw_ref, vwe_ref, emat_ref, ove_ref,
                        wt_ref, ct_ref, out_ref, cadd_ref, *, n_sel, ncp):
    qt = pl.program_id(1)
    cols = GROUP * Q_TILE
    n_w = WINDOW // Q_TILE + 1
    vrows = KV_W // 2
    gates_t = gn_ref[...].T
    n_far = jnp.maximum(qt - 1, 0) // (FAR_CHUNK // Q_TILE)
    groups = range(N_KV)
    qp = [jnp.concatenate(
        [q_ref[:, (g * GROUP + hh) * LANES:(g * GROUP + hh + 1) * LANES] for hh in range(GROUP)], axis=0)
        for g in groups]

    wk = n_w * Q_TILE
    start = pl.multiple_of(qt * Q_TILE, Q_TILE)
    lc, sw = [], []
    for g in groups:
        ri = lax.broadcasted_iota(i32, (ncp, cols), 0)
        cadd_ref[g] = jnp.where((ri >= CMP_PAD) & (ri < SUBLANES * qt), 0.0, NEG)
        wr = SUBLANES * qt + lax.broadcasted_iota(i32, (CMP_WIN, cols), 0)
        cadd_ref[g, pl.ds(pl.multiple_of(SUBLANES * qt, SUBLANES), CMP_WIN), :] = jnp.where(wr >= CMP_PAD, ct_ref[g], NEG)
        lc.append(_dot_nt(kc_ref[0], qp[g]) + cadd_ref[g])
    for g in groups:
        tiles = [wt_ref[g, jnp.where(qt + w >= n_w - 1, w + 1, 0)] for w in range(n_w)]
        sw.append(_dot_nt(kw_ref[0, pl.ds(start, wk), :], qp[g]) + jnp.concatenate(tiles, axis=0))
    live, e_c, e_w = [], [], []
    for g in groups:
        m = jnp.max(lc[g], axis=0, keepdims=True)
        live.append(m > 0.5 * NEG)
        e_c.append(jnp.exp(lc[g] - m))
    for g in groups:
        e_w.append(jnp.exp(sw[g] - jnp.max(sw[g], axis=0, keepdims=True)).astype(bf16))

    o_c, o_w, qx = [], [], []
    for g in groups:
        oc = _dot(vce_ref[0], e_c[g].astype(bf16))
        o_c.append(oc[:vrows] * jnp.where(live[g], 1.0 / oc[vrows:vrows + 1], 0.0))
        e_hi, e_lo = _split_bf16(e_c[g])
        raw = _dot(ove_ref[...], e_hi) + _dot(ove_ref[...], e_lo)
        raw = raw[:n_sel] * jnp.where(live[g], 1.0 / raw[n_sel:n_sel + 1], 0.0)
        imp = raw[:, 0:Q_TILE]
        for hh in range(1, GROUP):
            imp = imp + raw[:, hh * Q_TILE:(hh + 1) * Q_TILE]

        blk = lax.broadcasted_iota(i32, (n_sel, Q_TILE), 0)
        qpos = qt * Q_TILE + lax.broadcasted_iota(i32, (n_sel, Q_TILE), 1)
        cur = lax.shift_right_logical(qpos, SEL_SHIFT)
        forced = (blk == 0) | (blk == cur) | (blk == cur - 1)
        future = blk * SEL_BLOCK > qpos
        imp = jnp.where(forced, FORCE, jnp.where(future, -FORCE, imp))
        sel = _top_rows(imp, min(SEL_TOP, n_sel))
        sb = jnp.where(sel, 0.0, NEG)
        if n_sel < LANES:
            sb = jnp.concatenate([sb, jnp.zeros((LANES - n_sel, Q_TILE), f32)], axis=0)
        sbq = sb.T.astype(bf16)
        qx.append(jnp.concatenate([qp[g], jnp.concatenate([sbq] * GROUP, axis=0)], axis=1))
    for g in groups:
        ow = _dot(vwe_ref[0, :, pl.ds(start, wk)], e_w[g])
        o_w.append(ow[:vrows] * (1.0 / ow[vrows:vrows + 1]))

    def scores(start, size, g):
        kx = jnp.concatenate([ks_ref[pl.ds(start, size), :], emat_ref[pl.ds(start, size), :]], axis=1)
        return _dot_nt(kx, qx[g])

    def far_step(i, carry):
        ks = pl.multiple_of(i * FAR_CHUNK, FAR_CHUNK)
        v_ext = vse_ref[0, :, pl.ds(ks, FAR_CHUNK)]
        return _flash_steps(carry, [scores(ks, FAR_CHUNK, g) for g in groups], v_ext)

    def near_step(kt, carry):
        ks = pl.multiple_of(kt * Q_TILE, Q_TILE)
        v_ext = vse_ref[0, :, pl.ds(ks, Q_TILE)]
        tile = jnp.where(kt >= qt - 1, kt - qt + n_w, 2)
        return _flash_steps(carry, [scores(ks, Q_TILE, g) + wt_ref[g, tile] for g in groups], v_ext)

    init = (jnp.full((1, cols), NEG, f32), jnp.zeros((vse_ref.shape[1], cols), f32)) * N_KV
    carry = lax.fori_loop(0, n_far, far_step, init)
    carry = lax.fori_loop(n_far * (FAR_CHUNK // Q_TILE), qt + 1, near_step, carry)

    heads_t = []
    for g in groups:
        acc = carry[2 * g + 1]
        o_s = acc[:vrows] * (1.0 / acc[vrows:vrows + 1])
        for hh in range(GROUP):
            head = g * GROUP + hh
            cs = slice(hh * Q_TILE, (hh + 1) * Q_TILE)
            ds = slice(g * HEAD_DIM, (g + 1) * HEAD_DIM)
            heads_t.append(gates_t[3 * head:3 * head + 1, :] * o_c[g][ds, cs]
                           + gates_t[3 * head + 1:3 * head + 2, :] * o_s[ds, cs]
                           + gates_t[3 * head + 2:3 * head + 3, :] * o_w[g][ds, cs])
    for pair in range(N_HEADS // 2):
        both = jnp.concatenate(heads_t[2 * pair:2 * pair + 2], axis=0)
        out_ref[:, pair * LANES:(pair + 1) * LANES] = both.T


def _attn_prompt(q_pk, gates, kcp, ks_bf, kw_bf, wt, ct, b, t):
    ncp = kcp.shape[1]
    n_sel = t // SEL_BLOCK
    n_cmp = t // CMP_STRIDE - 1
    n_qt = t // Q_TILE
    assert t % FAR_CHUNK == 0 and n_sel % SUBLANES == 0 and n_sel <= LANES
    cols = GROUP * Q_TILE
    half = KV_W // 2

    def values_t(v):
        v_t = jnp.swapaxes(v, 1, 2)
        return jnp.concatenate([v_t, jnp.ones((b, ONES_ROWS, v.shape[1]), v.dtype)], axis=1)

    prompt = lambda a: a[:b * t].reshape(b, t, -1)
    kw_pad = jnp.pad(prompt(kw_bf), ((0, 0), (WINDOW, 0), (0, 0)))
    vce, vse, vwe = values_t(kcp[:, :, half:]), values_t(prompt(ks_bf)[:, :, half:]), values_t(kw_pad[:, :, half:])
    ove = jnp.concatenate([_overlap_t(n_sel, n_sel, ncp, n_cmp), jnp.ones((ONES_ROWS, ncp), bf16)], axis=0)
    emat = jnp.asarray(np.arange(t)[:, None] // SEL_BLOCK == np.arange(LANES)[None, :], dtype=bf16)
    tp = t + WINDOW
    vr = half + ONES_ROWS
    per_b = lambda shape: pl.BlockSpec((1,) + shape, lambda i, j: (i, 0, 0))
    q_rows = lambda w: pl.BlockSpec((Q_TILE, w), lambda i, j: (i * n_qt + j, 0))
    return pl.pallas_call(
        functools.partial(_attn_prompt_kernel, n_sel=n_sel, ncp=ncp),
        grid=(b, n_qt),
        in_specs=[
            q_rows(N_HEADS * LANES), q_rows(LANES),
            per_b((ncp, half)), per_b((vr, ncp)),
            pl.BlockSpec((t, half), lambda i, j: (i, 0)), per_b((vr, t)),
            per_b((tp, half)), per_b((vr, tp)),
            _resident(emat.shape), _resident(ove.shape), _resident(wt.shape), _resident(ct.shape),
        ],
        out_specs=q_rows(ATTN_DIM),
        out_shape=jax.ShapeDtypeStruct((b * t, ATTN_DIM), f32),
        scratch_shapes=[pltpu.VMEM((N_KV, ncp, cols), f32)],
        compiler_params=_cparams(("parallel", "arbitrary")),
        name="attn_prompt",
    )(q_pk, gates, kcp, vce, ks_bf, vse, kw_pad, vwe, emat, ove, wt, ct)


def _softmax_rows2(s_main, s_new):
    m = jnp.maximum(jnp.max(s_main, axis=1, keepdims=True), jnp.max(s_new, axis=1, keepdims=True))
    e_main, e_new = jnp.exp(s_main - m), jnp.exp(s_new - m)
    inv = 1.0 / (jnp.sum(e_main, axis=1, keepdims=True) + jnp.sum(e_new, axis=1, keepdims=True))
    return (e_main * inv).astype(bf16), (e_new * inv).astype(bf16)


def _attn_sample_kernel(*refs, n_pages, n_sel, past_len):
    page_refs = refs[1:1 + n_pages]
    (q_ref, g_ref, kcp_ref, knew_ref, win_ref, wnew_ref, ovt_ref, rmat_ref, emat_ref,
     scmp_ref, sk_ref, snew_ref, swin_ref, c31_ref, out_ref, ke_ref, vt_ref) = refs[1 + n_pages:]
    half = KV_W // 2
    q = q_ref[0]
    q_rows = q[0:SAMPLE_COLS]

    kcp = kcp_ref[0]
    pc = _masked_softmax_cols(_dot_nt(kcp[:, :half], q) + scmp_ref[...])
    o_c = _dot_tn(pc.astype(bf16), kcp[:, half:])[0:SAMPLE_COLS]
    p_hi, p_lo = _split_bf16(pc)
    imp = _dot(ovt_ref[...], p_hi) + _dot(ovt_ref[...], p_lo)
    i_hi, i_lo = _split_bf16(imp)
    imp = _dot(i_hi, rmat_ref[...]) + _dot(i_lo, rmat_ref[...])
    rows = imp.shape[0]
    blk = lax.broadcasted_iota(i32, (rows, LANES), 0)
    qpos = past_len + (lax.broadcasted_iota(i32, (rows, LANES), 1) & 3)
    cur = lax.shift_right_logical(qpos, SEL_SHIFT)
    forced = (blk == 0) | (blk == cur) | (blk == cur - 1)
    future = blk * SEL_BLOCK > qpos
    imp = jnp.where(forced, FORCE, jnp.where(future, -FORCE, imp))
    imp = jnp.where(blk < n_sel, imp, -jnp.inf)
    sel = _top_rows(imp, min(SEL_TOP, n_sel))
    sb = jnp.where(sel, 0.0, NEG)
    past_blocks = past_len // SEL_BLOCK
    sb_rows = sb[0:past_blocks].T[0:SAMPLE_COLS]

    for k in range(n_pages):
        lanes = slice(k * PAGE_SIZE, (k + 1) * PAGE_SIZE)
        ke_ref[0:half, lanes] = page_refs[k][0, 0:half, :].astype(bf16)
        vt_ref[:, lanes] = page_refs[k][0, half:, :].astype(bf16)
    ke_ref[half:, :] = emat_ref[...]
    n_far = past_len - PAGE_SIZE
    s = _dot(jnp.concatenate([q_rows, sb_rows.astype(bf16)], axis=1), ke_ref[...])
    s = jnp.concatenate([s[:, :n_far] + c31_ref[:, 0:1], s[:, n_far:] + sk_ref[...]], axis=1)
    knew = knew_ref[0].astype(bf16)
    s_n = _dot(q_rows, knew[0:half]) + snew_ref[...]
    p, p_n = _softmax_rows2(s, s_n)
    o_s = _dot_nt(p, vt_ref[...]) + _dot_nt(p_n, knew[half:])

    win = win_ref[0].astype(bf16)
    wnew = wnew_ref[0].astype(bf16)
    p, p_n = _softmax_rows2(_dot(q_rows, win[0:half]) + swin_ref[...], _dot(q_rows, wnew[0:half]) + snew_ref[...])
    o_w = _dot_nt(p, win[half:]) + _dot_nt(p_n, wnew[half:])

    gt = g_ref[0]
    out_ref[0] = gt[:, 0:1] * o_c + gt[:, 1:2] * o_s + gt[:, 2:3] * o_w


def _attn_sample(q_pk, gates, kcp, pages_t, page_table, ks_new, win_t, kw_new, tabs):
    scmp, sk, snew, swin, c31s = tabs
    bd, t_new, _ = q_pk.shape
    n_pages = page_table.shape[1]
    past_len = n_pages * PAGE_SIZE
    ncp = kcp.shape[1]
    n_cmp = past_len // CMP_STRIDE - 1
    n_sel = -(-(past_len + t_new) // SEL_BLOCK)
    n_sel_rows = -(-n_sel // SUBLANES) * SUBLANES
    past_blocks = past_len // SEL_BLOCK
    n_cols = N_KV * GROUP * t_new
    assert n_cols == SAMPLE_COLS and win_t.shape[2] == WINDOW and past_blocks == LANES
    qc = q_pk.reshape(bd, t_new, N_HEADS, LANES).transpose(0, 2, 1, 3).reshape(bd, n_cols, LANES)
    qc = jnp.pad(qc, ((0, 0), (0, LANES - n_cols), (0, 0)))
    gc = gates[:, :, :3 * N_HEADS].reshape(bd, t_new, N_HEADS, 3).transpose(0, 2, 1, 3).reshape(bd, n_cols, 3)
    gc = jnp.pad(gc, ((0, 0), (0, 0), (0, SUBLANES - 3)))
    new_t = lambda a: jnp.pad(jnp.swapaxes(a, 1, 2), ((0, 0), (0, 0), (0, LANES - t_new)))
    ovt = _overlap_t(n_sel_rows, n_sel, ncp, n_cmp)
    c = np.arange(LANES)
    same = (c[:, None] // (GROUP * t_new) == c[None, :] // (GROUP * t_new)) & (c[:, None] % t_new == c[None, :] % t_new)
    rmat = jnp.asarray(same & (c[:, None] < n_cols) & (c[None, :] < n_cols), dtype=bf16)
    emat = jnp.asarray(np.arange(past_blocks)[:, None] == np.arange(past_len)[None, :] // SEL_BLOCK, dtype=bf16)

    def page_spec(k):
        return pl.BlockSpec((1, KV_W, PAGE_SIZE), lambda i, pt: (pt[i, k], 0, 0))

    per_b = lambda shape: pl.BlockSpec((1,) + shape, lambda i, pt: (i, 0, 0))
    res = lambda a: pl.BlockSpec(a.shape, lambda i, pt: (0,) * a.ndim)
    half = KV_W // 2
    grid_spec = pltpu.PrefetchScalarGridSpec(
        num_scalar_prefetch=1,
        grid=(bd,),
        in_specs=[page_spec(k) for k in range(n_pages)] + [
            per_b((LANES, LANES)), per_b((n_cols, SUBLANES)), per_b((ncp, KV_W)), per_b((KV_W, LANES)),
            per_b((KV_W, WINDOW)), per_b((KV_W, LANES)),
            res(ovt), res(rmat), res(emat), res(scmp), res(sk), res(snew), res(swin), res(c31s)],
        out_specs=per_b((n_cols, LANES)),
        scratch_shapes=[pltpu.VMEM((half + past_blocks, past_len), bf16), pltpu.VMEM((half, past_len), bf16)],
    )
    o = pl.pallas_call(
        functools.partial(_attn_sample_kernel, n_pages=n_pages, n_sel=n_sel, past_len=past_len),
        grid_spec=grid_spec,
        out_shape=jax.ShapeDtypeStruct((bd, n_cols, LANES), f32),
        compiler_params=_cparams(("parallel",)),
        name="attn_sample",
    )(page_table, *([pages_t] * n_pages), qc, gc, kcp, new_t(ks_new), win_t, new_t(kw_new),
      ovt, rmat, emat, scmp, sk, snew, swin, c31s)
    o = o.reshape(bd, N_KV, GROUP, t_new, N_KV, HEAD_DIM)
    o = jnp.stack([o[:, g, :, :, g, :] for g in range(N_KV)], axis=1)
    return o.transpose(0, 3, 1, 2, 4).reshape(bd, t_new, ATTN_DIM)


def _merge_kernel(x_ref, a_ref, u0_ref, halo_ref, u1s_ref, u2s_ref, cb_ref, mg_ref, cw_ref, cbias_ref,
                  wau_ref, wcu_ref, wo_ref, g2_ref, x2_ref, h2_ref, *, d_model, prompt_tiles, seq_tiles):
    i = pl.program_id(0)
    u0 = u0_ref[...]
    r = lax.broadcasted_iota(i32, u0.shape, 0)
    keep = jnp.where(i % seq_tiles == 0, 0.0, 1.0)
    p1 = halo_ref[SUBLANES - 1:SUBLANES, :] * keep
    p2 = halo_ref[SUBLANES - 2:SUBLANES - 1, :] * keep
    u1 = jnp.where(r == 0, p1, pltpu.roll(u0, 1, axis=0))
    u2 = jnp.where(r == 0, p2, jnp.where(r == 1, p1, pltpu.roll(u0, 2, axis=0)))
    decode = i >= prompt_tiles
    u1 = jnp.where(decode, u1s_ref[...], u1)
    u2 = jnp.where(decode, u2s_ref[...], u2)
    y = cbias_ref[...] + cw_ref[0:1, :] * u2
    y = y + cw_ref[1:2, :] * u1
    y = y + cw_ref[2:3, :] * u0
    up_a = _dot(a_ref[...].astype(bf16), wau_ref[...])
    up_c = _dot((cb_ref[...] * y).astype(bf16), wcu_ref[...])
    mixed = mg_ref[:, :d_model] * up_a + mg_ref[:, d_model:] * up_c
    x2 = x_ref[...] + _dot(mixed.astype(bf16), wo_ref[...])
    x2_ref[...] = x2
    h2_ref[...] = _rmsnorm(x2, g2_ref[...]).astype(bf16)


def _merge(x, attn, cin, u1s, u2s, cb, mg, conv_w, conv_b, w_attn_up, w_conv_up, w_out, norm2_g, n_prompt, t):
    n, d = x.shape
    tm = ROW_TILE
    assert CONV_W == 3 and t % tm == 0 and n_prompt % tm == 0 and (n - n_prompt) % tm == 0
    prompt_tiles = n_prompt // tm
    row = lambda w: pl.BlockSpec((tm, w), lambda i: (i, 0))
    halo = pl.BlockSpec((SUBLANES, CONV_DIM), lambda i: (jnp.maximum(i * (tm // SUBLANES) - 1, 0), 0))
    dec = pl.BlockSpec((tm, CONV_DIM), lambda i: (jnp.maximum(i - prompt_tiles, 0), 0))
    cw = jnp.pad(conv_w, ((0, SUBLANES - CONV_W), (0, 0)))
    return pl.pallas_call(
        functools.partial(_merge_kernel, d_model=d, prompt_tiles=prompt_tiles, seq_tiles=t // tm),
        grid=(n // tm,),
        in_specs=[row(d), row(ATTN_DIM), row(CONV_DIM), halo, dec, dec, row(CONV_DIM), row(2 * d),
                  _resident(cw.shape), _resident((1, CONV_DIM)), _resident(w_attn_up.shape),
                  _resident(w_conv_up.shape), _resident(w_out.shape), _resident((1, d))],
        out_specs=[row(d), row(d)],
        out_shape=[jax.ShapeDtypeStruct((n, d), f32), jax.ShapeDtypeStruct((n, d), bf16)],
        compiler_params=_cparams(("parallel",)),
        name="merge",
    )(x, attn, cin, cin, u1s, u2s, cb, mg, cw, conv_b.reshape(1, CONV_DIM), w_attn_up.astype(bf16),
      w_conv_up.astype(bf16), w_out.astype(bf16), norm2_g.reshape(1, d))


def _top_rows_sorted(s, k):
    rows = lax.broadcasted_iota(i32, s.shape, 0).astype(f32)
    vals, idxs = [], []
    for _ in range(k):
        m = jnp.max(s, axis=0, keepdims=True)
        first = jnp.min(jnp.where(s == m, rows, 1e9), axis=0, keepdims=True)
        vals.append(m)
        idxs.append(first)
        s = jnp.where(rows == first, -jnp.inf, s)
    return jnp.concatenate(vals, axis=0), jnp.concatenate(idxs, axis=0)


def _peer_route_kernel(h_ref, wq_ref, k1_ref, k2_ref, a_ref, b_ref, g_ref):
    qp = _dot(h_ref[...], wq_ref[...]).astype(bf16)
    key_dim = 2 * PEER_HALF
    a_rows, b_rows, g_rows = [], [], []
    for h in range(PEER_HEADS):
        q1 = qp[:, h * key_dim:h * key_dim + PEER_HALF]
        q2 = qp[:, h * key_dim + PEER_HALF:(h + 1) * key_dim]
        v1, i1 = _top_rows_sorted(_dot_nt(k1_ref[...], q1), PEER_TOPK)
        v2, i2 = _top_rows_sorted(_dot_nt(k2_ref[...], q2), PEER_TOPK)
        counts = [PEER_TOPK // (r + 1) for r in range(PEER_TOPK)]
        cand = jnp.concatenate([v1[r:r + 1, :] + v2[0:n, :] for r, n in enumerate(counts)], axis=0)
        code = jnp.concatenate([i1[r:r + 1, :] * PEER_NKEYS + i2[0:n, :] for r, n in enumerate(counts)], axis=0)
        pad = -sum(counts) % SUBLANES
        cand = jnp.concatenate([cand, jnp.full((pad, cand.shape[1]), -jnp.inf, f32)], axis=0)
        code = jnp.concatenate([code, jnp.zeros((pad, code.shape[1]), f32)], axis=0)
        rows = lax.broadcasted_iota(i32, cand.shape, 0).astype(f32)
        sc, ex = [], []
        for _ in range(PEER_TOPK):
            m = jnp.max(cand, axis=0, keepdims=True)
            first = jnp.min(jnp.where(cand == m, rows, 1e9), axis=0, keepdims=True)
            pick = rows == first
            sc.append(m)
            ex.append(jnp.max(jnp.where(pick, code, -1.0), axis=0, keepdims=True))
            cand = jnp.where(pick, -jnp.inf, cand)
        sc = jnp.concatenate(sc, axis=0)
        ex = jnp.concatenate(ex, axis=0)
        e1 = jnp.floor(ex * (1.0 / PEER_NKEYS))
        e = jnp.exp(sc - sc[0:1, :])
        a_rows.append(e1)
        b_rows.append(ex - e1 * PEER_NKEYS)
        g_rows.append(e / jnp.sum(e, axis=0, keepdims=True))
    tn = h_ref.shape[0]
    for src, dst in ((a_rows, a_ref), (b_rows, b_ref), (g_rows, g_ref)):
        full = jnp.concatenate(src, axis=0)
        for c in range(tn // LANES):
            dst[c * LANES:(c + 1) * LANES, :] = full[:, c * LANES:(c + 1) * LANES].T


def _peer_route(h2, wq, k1, k2):
    n, d = h2.shape
    tn = ROW_TILE
    assert PEER_HEADS * PEER_TOPK == LANES
    row = lambda w: pl.BlockSpec((tn, w), lambda i: (i, 0))
    return pl.pallas_call(
        _peer_route_kernel,
        grid=(n // tn,),
        in_specs=[row(d), _resident(wq.shape), _resident(k1.shape), _resident(k2.shape)],
        out_specs=[row(LANES)] * 3,
        out_shape=[jax.ShapeDtypeStruct((n, LANES), f32)] * 3,
        compiler_params=_cparams(("parallel",)),
        name="peer_route",
    )(h2, wq.astype(bf16), k1.astype(bf16), k2.astype(bf16))


def _peer_expert_kernel(h_ref, a_ref, b_ref, w_ref, x_ref, u_ref, v_ref, out_ref, g_ref, acc_ref, *, pitch):
    e_step = pl.program_id(1)
    tn = h_ref.shape[0]
    keys_per_chunk = EXPERT_CHUNK // PEER_NKEYS
    half_keys = PEER_NKEYS // 2
    hi_mask = jnp.uint32(0xFFFF0000)

    @pl.when(e_step == 0)
    def _build_gate_matrix():
        key_row = lax.broadcasted_iota(i32, (PEER_NKEYS, LANES), 0).astype(f32)

        def token(n, c):
            a = jnp.broadcast_to(a_ref[pl.ds(n, 1), :], (PEER_NKEYS, LANES))
            b = jnp.broadcast_to(b_ref[pl.ds(n, 1), :], (PEER_NKEYS, LANES))
            w = jnp.broadcast_to(w_ref[pl.ds(n, 1), :], (PEER_NKEYS, LANES))
            pa = (a == key_row).astype(bf16)
            wb = jnp.where(b == key_row, w, 0.0).astype(bf16)
            g = _dot_nt(pa, wb).astype(bf16).astype(f32)
            bits = lax.bitcast_convert_type(g, jnp.uint32)
            word = lax.shift_right_logical(bits[:half_keys], jnp.uint32(16)) | (bits[half_keys:] & hi_mask)
            g_ref[pl.ds(n, half_keys, stride=pitch), :] = word
            return c

        lax.fori_loop(0, tn, token, 0, unroll=TOKEN_UNROLL)
        acc_ref[...] = jnp.zeros_like(acc_ref)

    s = _dot_nt(h_ref[...], u_ref[...])
    key0 = (e_step * keys_per_chunk) % half_keys
    words = jnp.concatenate(
        [g_ref[pl.ds(pl.multiple_of((key0 + j) * pitch, SUBLANES), tn), :] for j in range(keys_per_chunk)], axis=1)
    shift = jnp.where(e_step * keys_per_chunk < half_keys, 16, 0).astype(jnp.uint32)
    gates = lax.bitcast_convert_type(lax.shift_left(words, shift) & hi_mask, f32)
    act = 0.5 * s * (1.0 + lax.erf(s * math.sqrt(0.5)))
    acc_ref[...] += _dot((gates * act).astype(bf16), v_ref[...])

    @pl.when(e_step == pl.num_programs(1) - 1)
    def _finish():
        out_ref[...] = x_ref[...] + acc_ref[...]


def _peer_expert(h2, a_idx, b_idx, gate_w, x2, u_tab, v_tab):
    n, d = h2.shape
    n_exp = u_tab.shape[0]
    tn = PEER_TILE
    pitch = tn + G_PITCH_PAD
    assert n % tn == 0 and n_exp == PEER_NKEYS * PEER_NKEYS and (PEER_NKEYS // 2) % (EXPERT_CHUNK // PEER_NKEYS) == 0
    row = lambda w: pl.BlockSpec((tn, w), lambda i, e: (i, 0))
    tab = pl.BlockSpec((EXPERT_CHUNK, d), lambda i, e: (e, 0))
    return pl.pallas_call(
        functools.partial(_peer_expert_kernel, pitch=pitch),
        grid=(n // tn, n_exp // EXPERT_CHUNK),
        in_specs=[row(d), row(LANES), row(LANES), row(LANES), row(d), tab, tab],
        out_specs=row(d),
        out_shape=jax.ShapeDtypeStruct((n, d), f32),
        scratch_shapes=[pltpu.VMEM((PEER_NKEYS // 2 * pitch, LANES), jnp.uint32), pltpu.VMEM((tn, d), f32)],
        compiler_params=_cparams(("parallel", "arbitrary")),
        name="peer_expert",
    )(h2, a_idx, b_idx, gate_w, x2, u_tab.astype(bf16), v_tab.astype(bf16))


def _ple_kernel(x_ref, p_ref, pg_ref, wg_ref, wp_ref, fg_ref, y_ref):
    x = x_ref[...]
    gate = jax.nn.sigmoid(_dot(_rmsnorm(x, pg_ref[...]).astype(bf16), wg_ref[...]))
    x = x + gate * _dot(p_ref[...].astype(bf16), wp_ref[...])
    y_ref[...] = _rmsnorm(x, fg_ref[...])


def _ple(x3, p, ple_g, w_ple_gate, w_ple_proj, final_g):
    n, d = x3.shape
    tm = ROW_TILE
    row = lambda w: pl.BlockSpec((tm, w), lambda i: (i, 0))
    return pl.pallas_call(
        _ple_kernel,
        grid=(n // tm,),
        in_specs=[row(d), row(p.shape[1]), _resident((1, d)), _resident(w_ple_gate.shape),
                  _resident(w_ple_proj.shape), _resident((1, d))],
        out_specs=row(d),
        out_shape=jax.ShapeDtypeStruct((n, d), f32),
        compiler_params=_cparams(("parallel",)),
        name="ple_final",
    )(x3, p, ple_g.reshape(1, d), w_ple_gate.astype(bf16), w_ple_proj.astype(bf16), final_g.reshape(1, d))


def kernel(x_prompt, x_sample, p_prompt, p_sample, cache_cmp_kv, cache_slc_kv, page_table, state_win_kv, state_conv, norm1_g, w_in, w_cmp, cmp_pos, conv_w, conv_b, w_attn_up, w_conv_up, w_out, norm2_g, peer_wq, peer_k1, peer_k2, peer_u, peer_v, ple_g, w_ple_gate, w_ple_proj, rel_bias, final_g):
    depth = norm1_g.shape[0]
    assert depth == 1, "single-layer trunk"
    b, t, d = x_prompt.shape
    bd, t_new, _ = x_sample.shape
    n_pages = page_table.shape[1]
    past_len = n_pages * PAGE_SIZE
    n_p, n_s = b * t, bd * t_new
    kv_shape = lambda lead: lead + (2, N_KV, HEAD_DIM)

    x_all = jnp.concatenate([x_prompt.reshape(n_p, d), x_sample.reshape(n_s, d)], axis=0)
    w_pack = _pack_w_in(w_in[0], d)
    q_pk, kvc, kvs, kvw, kvs_bf, kvw_bf, gates, cin, cbg, mg = _in_proj(x_all, norm1_g[0], w_pack)
    prompt = lambda a: a[:n_p].reshape(b, t, -1)
    sample = lambda a: a[n_p:].reshape(bd, t_new, -1)

    w4, pos_rows = _pack_w_cmp(w_cmp[0], cmp_pos[0])
    kcp_p = _compress_prompt(kvc, w4, pos_rows, b, t)
    kcp_s = _compress_sample(_pages_t(cache_cmp_kv[0]), page_table, w4, pos_rows)
    n_cmp_s = past_len // CMP_STRIDE - 1
    wt, ct, *sample_tabs = _tables(rel_bias, past_len, kcp_s.shape[1], n_cmp_s)

    kvs_s, kvw_s = sample(kvs), sample(kvw)
    attn_p = _attn_prompt(q_pk, gates, kcp_p, kvs_bf, kvw_bf, wt, ct, b, t)
    win_t = jnp.transpose(state_win_kv[0], (0, 2, 3, 4, 1)).reshape(bd, KV_W, -1)
    attn_s = _attn_sample(sample(q_pk), sample(gates), kcp_s, _pages_t(cache_slc_kv[0]), page_table, kvs_s, win_t,
                          kvw_s, sample_tabs)
    attn = jnp.concatenate([attn_p, attn_s.reshape(n_s, ATTN_DIM)], axis=0)

    cin_s = sample(cin)
    buf_s = state_conv[0]
    back = lambda k: jnp.concatenate([buf_s[:, CONV_W - 1 - k:], cin_s[:, :t_new - k]], axis=1).reshape(n_s, CONV_DIM)
    x2, h2 = _merge(x_all, attn, cin, back(1), back(2), cbg, mg, conv_w[0], conv_b[0], w_attn_up[0], w_conv_up[0],
                    w_out[0], norm2_g[0], n_p, t)

    a_idx, b_idx, gate_w = _peer_route(h2, peer_wq[0], peer_k1[0], peer_k2[0])
    x3 = _peer_expert(h2, a_idx, b_idx, gate_w, x2, peer_u[0], peer_v[0])

    p_all = jnp.concatenate([p_prompt[0].reshape(n_p, -1), p_sample[0].reshape(n_s, -1)], axis=0)
    y = _ple(x3, p_all, ple_g[0], w_ple_gate[0], w_ple_proj[0], final_g)

    n_win = min(WINDOW, t)
    assert t >= CONV_W - 1 and t_new >= CONV_W - 1
    new_win_s = jnp.concatenate([state_win_kv[:, :, t_new:], kvw_s.reshape(kv_shape((1, bd, t_new)))], axis=2)
    return (
        y[:n_p].reshape(b, t, d),
        y[n_p:].reshape(bd, t_new, d),
        prompt(kvc).reshape(kv_shape((1, b, t))),
        sample(kvc).reshape(kv_shape((1, bd, t_new))),
        prompt(kvs).reshape(kv_shape((1, b, t))),
        kvs_s.reshape(kv_shape((1, bd, t_new))),
        prompt(kvw)[:, t - n_win:].reshape(kv_shape((1, b, n_win))),
        new_win_s,
        prompt(cin)[None, :, t - (CONV_W - 1):],
        cin_s[None, :, t_new - (CONV_W - 1):],
    )
```

```python
import functools
import math

import jax
import jax.numpy as jnp
import numpy as np
from jax import lax
from jax.experimental import pallas as pl
from jax.experimental.pallas import tpu as pltpu

f32 = jnp.float32
bf16 = jnp.bfloat16
i32 = jnp.int32

N_HEADS = 8
N_KV = 2
GROUP = N_HEADS // N_KV
HEAD_DIM = 64
ATTN_DIM = N_HEADS * HEAD_DIM
KV_W = 2 * N_KV * HEAD_DIM
CMP_BLOCK = 32
CMP_STRIDE = 16
SEL_BLOCK = 64
SEL_SHIFT = 6
SEL_TOP = 16
WINDOW = 512
N_BUCKETS = 32
MAX_EXACT = N_BUCKETS // 2
MAX_DISTANCE = 128
CONV_DIM = 512
CONV_W = 3
PEER_HEADS = 8
PEER_NKEYS = 128
PEER_HALF = 128
PEER_TOPK = 16
PAGE_SIZE = 128
EPS = 1e-6
NEG = -1e30
FORCE = 1e4

LANES = 128
SUBLANES = 8
VMEM_LIMIT_BYTES = 56 * 1024 * 1024

Q_TILE = 128
SAMPLE_COLS = 32
FAR_CHUNK = 512
CMP_PAD = 16
CMP_WIN = 24
ONES_ROWS = 16
ROW_TILE = 256
PEER_TILE = 512
EXPERT_CHUNK = 1024
TOKEN_UNROLL = 16
G_PITCH_PAD = 8


def _cparams(sem):
    return pltpu.CompilerParams(dimension_semantics=sem, vmem_limit_bytes=VMEM_LIMIT_BYTES)


def _dot(a, b):
    return jnp.dot(a, b, preferred_element_type=f32)


def _dot_nt(a, b):
    return lax.dot_general(a, b, (((1,), (1,)), ((), ())), preferred_element_type=f32)


def _dot_tn(a, b):
    return lax.dot_general(a, b, (((0,), (0,)), ((), ())), preferred_element_type=f32)


def _split_bf16(x):
    hi = x.astype(bf16)
    lo = (x - hi.astype(f32)).astype(bf16)
    return hi, lo


def _resident(shape):
    nd = len(shape)
    return pl.BlockSpec(shape, lambda *_: (0,) * nd)


def _rmsnorm(x, g):
    ms = jnp.mean(x * x, axis=-1, keepdims=True)
    return x * lax.rsqrt(ms + EPS) * g


def _in_proj_kernel(x_ref, g_ref, w_ref, q_ref, kc_ref, ks_ref, kw_ref, ksb_ref, kwb_ref,
                    gn_ref, cin_ref, cb_ref, mg_ref, *, d_model):
    qw = N_HEADS * LANES
    h = _rmsnorm(x_ref[...], g_ref[...]).astype(bf16)
    o = 0
    q_ref[...] = (_dot(h, w_ref[:, o:o + qw]) * (HEAD_DIM ** -0.5)).astype(bf16)
    o += qw
    kc_ref[...] = _dot(h, w_ref[:, o:o + KV_W])
    o += KV_W
    ks = _dot(h, w_ref[:, o:o + KV_W])
    ks_ref[...] = ks
    ksb_ref[...] = ks.astype(bf16)
    o += KV_W
    kw = _dot(h, w_ref[:, o:o + KV_W])
    kw_ref[...] = kw
    kwb_ref[...] = kw.astype(bf16)
    o += KV_W
    gn_ref[...] = jax.nn.sigmoid(_dot(h, w_ref[:, o:o + LANES]))
    o += LANES
    cv = _dot(h, w_ref[:, o:o + CONV_DIM])
    o += CONV_DIM
    cb_ref[...] = _dot(h, w_ref[:, o:o + CONV_DIM])
    o += CONV_DIM
    cin_ref[...] = _dot(h, w_ref[:, o:o + CONV_DIM]) * cv
    o += CONV_DIM
    mg_ref[...] = jax.nn.sigmoid(_dot(h, w_ref[:, o:o + 2 * d_model]))


def _pack_w_in(w_in, d_model):
    sizes = (ATTN_DIM, KV_W, KV_W, KV_W, 3 * N_HEADS, CONV_DIM, CONV_DIM, CONV_DIM, 2 * d_model)
    parts, s = [], 0
    for n in sizes:
        parts.append(w_in[:, s:s + n])
        s += n
    wq = parts[0].reshape(d_model, N_KV, GROUP, HEAD_DIM)
    z = jnp.zeros_like(wq)
    slabs = [jnp.concatenate([wq[:, 0], z[:, 0]], axis=-1), jnp.concatenate([z[:, 1], wq[:, 1]], axis=-1)]
    wq = jnp.stack(slabs, axis=1).reshape(d_model, N_HEADS * LANES)
    gn = jnp.pad(parts[4], ((0, 0), (0, LANES - 3 * N_HEADS)))
    return jnp.concatenate([wq, parts[1], parts[2], parts[3], gn] + parts[5:], axis=1).astype(bf16)


def _in_proj(x, norm_g, w_pack):
    n, d = x.shape
    tm = ROW_TILE
    assert n % tm == 0
    row = lambda w: pl.BlockSpec((tm, w), lambda i: (i, 0))
    widths = (N_HEADS * LANES, KV_W, KV_W, KV_W, KV_W, KV_W, LANES, CONV_DIM, CONV_DIM, 2 * d)
    dtypes = (bf16, f32, f32, f32, bf16, bf16, f32, f32, f32, f32)
    return pl.pallas_call(
        functools.partial(_in_proj_kernel, d_model=d),
        grid=(n // tm,),
        in_specs=[row(d), _resident((1, d)), _resident(w_pack.shape)],
        out_specs=[row(w) for w in widths],
        out_shape=[jax.ShapeDtypeStruct((n, w), t) for w, t in zip(widths, dtypes)],
        compiler_params=_cparams(("parallel",)),
        name="in_proj",
    )(x, norm_g.reshape(1, d), w_pack)


def _bias_of_dist(dist, valid, rb_ref, head, shift=0.0):
    n = jnp.maximum(dist, 0)
    nf = jnp.maximum(n, 1).astype(f32)
    large = MAX_EXACT + (jnp.log(nf / MAX_EXACT) / math.log(MAX_DISTANCE / MAX_EXACT)
                         * (N_BUCKETS - MAX_EXACT)).astype(i32)
    large = jnp.minimum(large, N_BUCKETS - 1)
    bucket = jnp.where(n < MAX_EXACT, n, large)
    out = jnp.zeros(dist.shape, f32)
    for b in range(N_BUCKETS):
        out = jnp.where(bucket == b, rb_ref[b, head], out)
    return jnp.where(valid, out - shift, NEG)


def _tables_kernel(rb_ref, wt_ref, ct_ref, scmp_ref, sk_ref, snew_ref, swin_ref, c31s_ref,
                   *, past_len, ncp_s, n_cmp_s):
    ik = lax.broadcasted_iota(i32, (Q_TILE, Q_TILE), 0)
    iq = lax.broadcasted_iota(i32, (Q_TILE, Q_TILE), 1)
    n_w = WINDOW // Q_TILE + 1
    for g in range(N_KV):
        wt_ref[g, 0] = jnp.full(wt_ref.shape[2:], NEG, f32)
        for hh in range(GROUP):
            head = g * GROUP + hh
            far = rb_ref[N_BUCKETS - 1, head]
            cols = slice(hh * Q_TILE, (hh + 1) * Q_TILE)
            for w in range(n_w):
                dist = Q_TILE * (n_w - 1 - w) + iq - ik
                wt_ref[g, w + 1, :, cols] = _bias_of_dist(dist, (dist >= 0) & (dist < WINDOW), rb_ref, head, far)
            mm = lax.broadcasted_iota(i32, (CMP_WIN, Q_TILE), 0)
            jq = lax.broadcasted_iota(i32, (CMP_WIN, Q_TILE), 1)
            dist = jq - CMP_STRIDE * (mm - CMP_PAD) - (CMP_BLOCK - 1)
            ct_ref[g, :, cols] = _bias_of_dist(dist, dist >= 0, rb_ref, head, far)
    def sample_table(shape, q_axis, dist_fn, valid_fn, out_ref):
        c = lax.broadcasted_iota(i32, shape, q_axis)
        r = lax.broadcasted_iota(i32, shape, 1 - q_axis)
        t = c & 3
        hd = jnp.minimum(lax.shift_right_logical(c, 2), N_HEADS - 1)
        dist = dist_fn(r, t)
        valid = valid_fn(r, t, dist)
        acc = jnp.zeros(shape, f32)
        for head in range(N_HEADS):
            acc = jnp.where(hd == head, _bias_of_dist(dist, valid, rb_ref, head), acc)
        out_ref[...] = acc

    sample_table((ncp_s, LANES), 1, lambda r, t: past_len + t - CMP_STRIDE * (r - CMP_PAD) - (CMP_BLOCK - 1),
                 lambda r, t, d: (r >= CMP_PAD) & (r < CMP_PAD + n_cmp_s) & (d >= 0), scmp_ref)
    sample_table((SAMPLE_COLS, PAGE_SIZE), 0, lambda r, t: PAGE_SIZE + t - r, lambda r, t, d: d >= 0, sk_ref)
    sample_table((SAMPLE_COLS, LANES), 0, lambda r, t: t - r, lambda r, t, d: (d >= 0) & (r < 4), snew_ref)
    sample_table((SAMPLE_COLS, WINDOW), 0, lambda r, t: WINDOW + t - r, lambda r, t, d: d < WINDOW, swin_ref)
    sample_table((SAMPLE_COLS, LANES), 0, lambda r, t: jnp.full_like(r, MAX_DISTANCE), lambda r, t, d: d > 0, c31s_ref)


def _tables(rel_bias, past_len, ncp_s, n_cmp_s):
    n_w = WINDOW // Q_TILE + 1
    cols = GROUP * Q_TILE
    shapes = [
        (N_KV, n_w + 1, Q_TILE, cols), (N_KV, CMP_WIN, cols),
        (ncp_s, LANES), (SAMPLE_COLS, PAGE_SIZE), (SAMPLE_COLS, LANES), (SAMPLE_COLS, WINDOW), (SAMPLE_COLS, LANES),
    ]
    return pl.pallas_call(
        functools.partial(_tables_kernel, past_len=past_len, ncp_s=ncp_s, n_cmp_s=n_cmp_s),
        in_specs=[pl.BlockSpec(memory_space=pltpu.SMEM)],
        out_shape=[jax.ShapeDtypeStruct(s, f32) for s in shapes],
        compiler_params=pltpu.CompilerParams(vmem_limit_bytes=VMEM_LIMIT_BYTES),
        name="bias_tables",
    )(rel_bias)


def _ncp(n_chunks):
    return -(-(n_chunks + CMP_PAD + SUBLANES) // LANES) * LANES


def _compress_core(chunk_rows, w_ref, pos_ref, out_ref, n_chunks):
    half = KV_W // 2
    out_ref[...] = jnp.zeros(out_ref.shape, out_ref.dtype)
    for c in range(2):
        fs = None
        pb = None
        for p in range(CMP_STRIDE // 2):
            x = jnp.concatenate([chunk_rows(2 * p, c), chunk_rows(2 * p + 1, c)], axis=1).astype(bf16)
            lanes = [slice((2 * p + k) * KV_W + c * half, (2 * p + k) * KV_W + (c + 1) * half) for k in range(2)]
            px = jnp.concatenate([pos_ref[:, lanes[0]], pos_ref[:, lanes[1]]], axis=1)
            fs = _dot(x, w_ref[p, c]) if fs is None else fs + _dot(x, w_ref[p, c])
            pb = _dot(px, w_ref[p, c]) if pb is None else pb + _dot(px, w_ref[p, c])
        bias = pb[0:1, :half] + pb[1:2, half:]
        nxt = pltpu.roll(fs[:, half:], n_chunks - 1, axis=0)
        kc = fs[:, :half] + nxt + bias
        r = lax.broadcasted_iota(i32, kc.shape, 0)
        kc = jnp.where(r < n_chunks - 1, kc, 0.0)
        out_ref[0, CMP_PAD:CMP_PAD + n_chunks, c * half:(c + 1) * half] = kc.astype(out_ref.dtype)


def _compress_prompt_kernel(x_ref, w_ref, pos_ref, out_ref, *, n_chunks):
    half = KV_W // 2
    rows = lambda l, c: x_ref[:, l * KV_W + c * half:l * KV_W + (c + 1) * half]
    _compress_core(rows, w_ref, pos_ref, out_ref, n_chunks)


def _gather_pages(pt_ref, pages_hbm, buf_ref, sem_ref, n_pages):
    b = pl.program_id(0)
    slot = b % 2

    def copies(seq, into):
        return [pltpu.make_async_copy(pages_hbm.at[pt_ref[seq, k]], buf_ref.at[into, k], sem_ref.at[into])
                for k in range(n_pages)]

    @pl.when(b == 0)
    def _first():
        for c in copies(0, 0):
            c.start()

    @pl.when(b + 1 < pl.num_programs(0))
    def _next():
        for c in copies(b + 1, 1 - slot):
            c.start()

    for c in copies(b, slot):
        c.wait()
    return slot


def _compress_paged_kernel(pt_ref, pages_hbm, perm_ref, w_ref, pos_ref, out_ref, buf_ref, xp_ref, sem_ref,
                           *, n_pages, n_chunks):
    half = KV_W // 2
    per_page = PAGE_SIZE // CMP_STRIDE
    slot = _gather_pages(pt_ref, pages_hbm, buf_ref, sem_ref, n_pages)
    for k in range(n_pages):
        xp_ref[k] = _dot_nt(perm_ref[...], buf_ref[slot, k].astype(bf16))

    def rows(l, c):
        x = xp_ref[:, l * per_page:(l + 1) * per_page, c * half:(c + 1) * half]
        return x.reshape(n_pages * per_page, half)

    _compress_core(rows, w_ref, pos_ref, out_ref, n_chunks)


def _pack_w_cmp(w_cmp, cmp_pos):
    w = w_cmp.reshape(2, CMP_STRIDE // 2, 2, 2, HEAD_DIM, HEAD_DIM)
    w4 = jnp.einsum("hplcde,gG->pclgdhGe", w, jnp.eye(N_KV, dtype=f32))
    w4 = w4.reshape(CMP_STRIDE // 2, 2, KV_W, KV_W)

    def pos_row(p):
        return jnp.broadcast_to(p[:, :, None, :], (CMP_STRIDE, 2, N_KV, HEAD_DIM)).reshape(1, -1)

    pos = jnp.concatenate([pos_row(cmp_pos[:CMP_STRIDE]), pos_row(cmp_pos[CMP_STRIDE:]),
                           jnp.zeros((SUBLANES - 2, CMP_STRIDE * KV_W), f32)], axis=0)
    return w4.astype(bf16), pos.astype(bf16)


def _compress_prompt(kv_cmp, w4, pos, b, t):
    n_chunks = t // CMP_STRIDE
    cw = CMP_STRIDE * KV_W
    chunks = kv_cmp.reshape(-1, cw)
    ncp = _ncp(n_chunks)
    return pl.pallas_call(
        functools.partial(_compress_prompt_kernel, n_chunks=n_chunks),
        grid=(b,),
        in_specs=[pl.BlockSpec((n_chunks, cw), lambda i: (i, 0)), _resident(w4.shape), _resident(pos.shape)],
        out_specs=pl.BlockSpec((1, ncp, KV_W), lambda i: (i, 0, 0)),
        out_shape=jax.ShapeDtypeStruct((b, ncp, KV_W), bf16),
        compiler_params=_cparams(("parallel",)),
        name="compress_prompt",
    )(chunks, w4, pos)


def _pages_t(cache):
    n_pool = cache.shape[0]
    return jnp.transpose(cache, (0, 2, 3, 4, 1)).reshape(n_pool, KV_W, cache.shape[1])


def _compress_sample(pages_t, page_table, w4, pos):
    bd, n_pages = page_table.shape
    per_page = PAGE_SIZE // CMP_STRIDE
    n_chunks = n_pages * per_page
    ncp = _ncp(n_chunks)
    tok = np.arange(PAGE_SIZE)
    perm = jnp.asarray(tok[None, :] == (tok[:, None] % per_page) * CMP_STRIDE + tok[:, None] // per_page, dtype=bf16)

    res = lambda a: pl.BlockSpec(a.shape, lambda i, pt: (0,) * a.ndim)
    grid_spec = pltpu.PrefetchScalarGridSpec(
        num_scalar_prefetch=1,
        grid=(bd,),
        in_specs=[pl.BlockSpec(memory_space=pl.ANY), res(perm), res(w4), res(pos)],
        out_specs=pl.BlockSpec((1, ncp, KV_W), lambda i, pt: (i, 0, 0)),
        scratch_shapes=[pltpu.VMEM((2, n_pages, KV_W, PAGE_SIZE), f32), pltpu.VMEM((n_pages, PAGE_SIZE, KV_W), f32),
                        pltpu.SemaphoreType.DMA((2,))],
    )
    return pl.pallas_call(
        functools.partial(_compress_paged_kernel, n_pages=n_pages, n_chunks=n_chunks),
        grid_spec=grid_spec,
        out_shape=jax.ShapeDtypeStruct((bd, ncp, KV_W), bf16),
        compiler_params=_cparams(("arbitrary",)),
        name="compress_sample",
    )(page_table, pages_t, perm, w4, pos)


def _overlap_t(n_sel_rows, n_sel, ncp, n_cmp):
    j = np.arange(n_sel_rows)[:, None]
    n = np.arange(ncp)[None, :] - CMP_PAD
    hit = (n * CMP_STRIDE < (j + 1) * SEL_BLOCK) & (n * CMP_STRIDE + CMP_BLOCK > j * SEL_BLOCK)
    hit &= (n >= 0) & (n < n_cmp) & (j < n_sel)
    return jnp.asarray(hit, dtype=bf16)


def _masked_softmax_cols(s, maybe_empty=True):
    m = jnp.max(s, axis=0, keepdims=True)
    e = jnp.exp(s - m)
    inv = 1.0 / jnp.sum(e, axis=0, keepdims=True)
    if maybe_empty:
        inv = jnp.where(m > 0.5 * NEG, inv, 0.0)
    return e * inv


def _top_rows(imp, k):
    rows = lax.broadcasted_iota(i32, imp.shape, 0).astype(f32)
    sel = jnp.zeros(imp.shape, jnp.bool_)
    v = imp
    for _ in range(k):
        m = jnp.max(v, axis=0, keepdims=True)
        first = jnp.min(jnp.where(v == m, rows, 1e9), axis=0, keepdims=True)
        pick = rows == first
        sel = sel | pick
        v = jnp.where(pick, -jnp.inf, v)
    return sel


def _flash_steps(carry, scores, v_ext):
    m_new = [jnp.maximum(carry[2 * i], jnp.max(s, axis=0, keepdims=True)) for i, s in enumerate(scores)]
    p = [jnp.exp(s - m).astype(bf16) for s, m in zip(scores, m_new)]
    out = []
    for i in range(len(scores)):
        alpha = jnp.exp(carry[2 * i] - m_new[i])
        out.extend((m_new[i], alpha * carry[2 * i + 1] + _dot(v_ext, p[i])))
    return tuple(out)


def _attn_prompt_kernel(q_ref, gn_ref, kc_ref, vce_ref, ks_ref, vse_ref, kw_ref, vwe_ref, emat_ref, ove_ref,
                        wt_ref, ct_ref, out_ref, cadd_ref, *, n_sel, ncp):
    qt = pl.program_id(1)
    cols = GROUP * Q_TILE
    n_w = WINDOW // Q_TILE + 1
    vrows = KV_W // 2
    gates_t = gn_ref[...].T
    n_far = jnp.maximum(qt - 1, 0) // (FAR_CHUNK // Q_TILE)
    groups = range(N_KV)
    qp = [jnp.concatenate(
        [q_ref[:, (g * GROUP + hh) * LANES:(g * GROUP + hh + 1) * LANES] for hh in range(GROUP)], axis=0)
        for g in groups]

    wk = n_w * Q_TILE
    start = pl.multiple_of(qt * Q_TILE, Q_TILE)
    lc, sw = [], []
    for g in groups:
        ri = lax.broadcasted_iota(i32, (ncp, cols), 0)
        cadd_ref[g] = jnp.where((ri >= CMP_PAD) & (ri < SUBLANES * qt), 0.0, NEG)
        wr = SUBLANES * qt + lax.broadcasted_iota(i32, (CMP_WIN, cols), 0)
        cadd_ref[g, pl.ds(pl.multiple_of(SUBLANES * qt, SUBLANES), CMP_WIN), :] = jnp.where(wr >= CMP_PAD, ct_ref[g], NEG)
        lc.append(_dot_nt(kc_ref[0], qp[g]) + cadd_ref[g])
    for g in groups:
        tiles = [wt_ref[g, jnp.where(qt + w >= n_w - 1, w + 1, 0)] for w in range(n_w)]
        sw.append(_dot_nt(kw_ref[0, pl.ds(start, wk), :], qp[g]) + jnp.concatenate(tiles, axis=0))
    live, e_c, e_w = [], [], []
    for g in groups:
        m = jnp.max(lc[g], axis=0, keepdims=True)
        live.append(m > 0.5 * NEG)
        e_c.append(jnp.exp(lc[g] - m))
    for g in groups:
        e_w.append(jnp.exp(sw[g] - jnp.max(sw[g], axis=0, keepdims=True)).astype(bf16))

    o_c, o_w, qx = [], [], []
    for g in groups:
        oc = _dot(vce_ref[0], e_c[g].astype(bf16))
        o_c.append(oc[:vrows] * jnp.where(live[g], 1.0 / oc[vrows:vrows + 1], 0.0))
        e_hi, e_lo = _split_bf16(e_c[g])
        raw = _dot(ove_ref[...], e_hi) + _dot(ove_ref[...], e_lo)
        raw = raw[:n_sel] * jnp.where(live[g], 1.0 / raw[n_sel:n_sel + 1], 0.0)
        imp = raw[:, 0:Q_TILE]
        for hh in range(1, GROUP):
            imp = imp + raw[:, hh * Q_TILE:(hh + 1) * Q_TILE]

        blk = lax.broadcasted_iota(i32, (n_sel, Q_TILE), 0)
        qpos = qt * Q_TILE + lax.broadcasted_iota(i32, (n_sel, Q_TILE), 1)
        cur = lax.shift_right_logical(qpos, SEL_SHIFT)
        forced = (blk == 0) | (blk == cur) | (blk == cur - 1)
        future = blk * SEL_BLOCK > qpos
        imp = jnp.where(forced, FORCE, jnp.where(future, -FORCE, imp))
        sel = _top_rows(imp, min(SEL_TOP, n_sel))
        sb = jnp.where(sel, 0.0, NEG)
        if n_sel < LANES:
            sb = jnp.concatenate([sb, jnp.zeros((LANES - n_sel, Q_TILE), f32)], axis=0)
        sbq = sb.T.astype(bf16)
        qx.append(jnp.concatenate([qp[g], jnp.concatenate([sbq] * GROUP, axis=0)], axis=1))
    for g in groups:
        ow = _dot(vwe_ref[0, :, pl.ds(start, wk)], e_w[g])
        o_w.append(ow[:vrows] * (1.0 / ow[vrows:vrows + 1]))

    def scores(start, size, g):
        kx = jnp.concatenate([ks_ref[pl.ds(start, size), :], emat_ref[pl.ds(start, size), :]], axis=1)
        return _dot_nt(kx, qx[g])

    def far_step(i, carry):
        ks = pl.multiple_of(i * FAR_CHUNK, FAR_CHUNK)
        v_ext = vse_ref[0, :, pl.ds(ks, FAR_CHUNK)]
        return _flash_steps(carry, [scores(ks, FAR_CHUNK, g) for g in groups], v_ext)

    def near_step(kt, carry):
        ks = pl.multiple_of(kt * Q_TILE, Q_TILE)
        v_ext = vse_ref[0, :, pl.ds(ks, Q_TILE)]
        tile = jnp.where(kt >= qt - 1, kt - qt + n_w, 2)
        return _flash_steps(carry, [scores(ks, Q_TILE, g) + wt_ref[g, tile] for g in groups], v_ext)

    init = (jnp.full((1, cols), NEG, f32), jnp.zeros((vse_ref.shape[1], cols), f32)) * N_KV
    carry = lax.fori_loop(0, n_far, far_step, init)
    carry = lax.fori_loop(n_far * (FAR_CHUNK // Q_TILE), qt + 1, near_step, carry)

    heads_t = []
    for g in groups:
        acc = carry[2 * g + 1]
        o_s = acc[:vrows] * (1.0 / acc[vrows:vrows + 1])
        for hh in range(GROUP):
            head = g * GROUP + hh
            cs = slice(hh * Q_TILE, (hh + 1) * Q_TILE)
            ds = slice(g * HEAD_DIM, (g + 1) * HEAD_DIM)
            heads_t.append(gates_t[3 * head:3 * head + 1, :] * o_c[g][ds, cs]
                           + gates_t[3 * head + 1:3 * head + 2, :] * o_s[ds, cs]
                           + gates_t[3 * head + 2:3 * head + 3, :] * o_w[g][ds, cs])
    for pair in range(N_HEADS // 2):
        both = jnp.concatenate(heads_t[2 * pair:2 * pair + 2], axis=0)
        out_ref[:, pair * LANES:(pair + 1) * LANES] = both.T


def _attn_prompt(q_pk, gates, kcp, ks_bf, kw_bf, wt, ct, b, t):
    ncp = kcp.shape[1]
    n_sel = t // SEL_BLOCK
    n_cmp = t // CMP_STRIDE - 1
    n_qt = t // Q_TILE
    assert t % FAR_CHUNK == 0 and n_sel % SUBLANES == 0 and n_sel <= LANES
    cols = GROUP * Q_TILE
    half = KV_W // 2

    def values_t(v):
        v_t = jnp.swapaxes(v, 1, 2)
        return jnp.concatenate([v_t, jnp.ones((b, ONES_ROWS, v.shape[1]), v.dtype)], axis=1)

    prompt = lambda a: a[:b * t].reshape(b, t, -1)
    kw_pad = jnp.pad(prompt(kw_bf), ((0, 0), (WINDOW, 0), (0, 0)))
    vce, vse, vwe = values_t(kcp[:, :, half:]), values_t(prompt(ks_bf)[:, :, half:]), values_t(kw_pad[:, :, half:])
    ove = jnp.concatenate([_overlap_t(n_sel, n_sel, ncp, n_cmp), jnp.ones((ONES_ROWS, ncp), bf16)], axis=0)
    emat = jnp.asarray(np.arange(t)[:, None] // SEL_BLOCK == np.arange(LANES)[None, :], dtype=bf16)
    tp = t + WINDOW
    vr = half + ONES_ROWS
    per_b = lambda shape: pl.BlockSpec((1,) + shape, lambda i, j: (i, 0, 0))
    q_rows = lambda w: pl.BlockSpec((Q_TILE, w), lambda i, j: (i * n_qt + j, 0))
    return pl.pallas_call(
        functools.partial(_attn_prompt_kernel, n_sel=n_sel, ncp=ncp),
        grid=(b, n_qt),
        in_specs=[
            q_rows(N_HEADS * LANES), q_rows(LANES),
            per_b((ncp, half)), per_b((vr, ncp)),
            pl.BlockSpec((t, half), lambda i, j: (i, 0)), per_b((vr, t)),
            per_b((tp, half)), per_b((vr, tp)),
            _resident(emat.shape), _resident(ove.shape), _resident(wt.shape), _resident(ct.shape),
        ],
        out_specs=q_rows(ATTN_DIM),
        out_shape=jax.ShapeDtypeStruct((b * t, ATTN_DIM), f32),
        scratch_shapes=[pltpu.VMEM((N_KV, ncp, cols), f32)],
        compiler_params=_cparams(("parallel", "arbitrary")),
        name="attn_prompt",
    )(q_pk, gates, kcp, vce, ks_bf, vse, kw_pad, vwe, emat, ove, wt, ct)


def _softmax_rows2(s_main, s_new):
    m = jnp.maximum(jnp.max(s_main, axis=1, keepdims=True), jnp.max(s_new, axis=1, keepdims=True))
    e_main, e_new = jnp.exp(s_main - m), jnp.exp(s_new - m)
    inv = 1.0 / (jnp.sum(e_main, axis=1, keepdims=True) + jnp.sum(e_new, axis=1, keepdims=True))
    return (e_main * inv).astype(bf16), (e_new * inv).astype(bf16)


def _attn_sample_kernel(pt_ref, pages_hbm, q_ref, g_ref, kcp_ref, knew_ref, win_ref, wnew_ref, ovt_ref, rmat_ref,
                        emat_ref, scmp_ref, sk_ref, snew_ref, swin_ref, c31_ref, out_ref, buf_ref, ke_ref, vt_ref,
                        sem_ref, *, n_pages, n_sel, past_len):
    half = KV_W // 2
    slot = _gather_pages(pt_ref, pages_hbm, buf_ref, sem_ref, n_pages)
    q = q_ref[0]
    q_rows = q[0:SAMPLE_COLS]

    kcp = kcp_ref[0]
    pc = _masked_softmax_cols(_dot_nt(kcp[:, :half], q) + scmp_ref[...])
    o_c = _dot_tn(pc.astype(bf16), kcp[:, half:])[0:SAMPLE_COLS]
    p_hi, p_lo = _split_bf16(pc)
    imp = _dot(ovt_ref[...], p_hi) + _dot(ovt_ref[...], p_lo)
    i_hi, i_lo = _split_bf16(imp)
    imp = _dot(i_hi, rmat_ref[...]) + _dot(i_lo, rmat_ref[...])
    rows = imp.shape[0]
    blk = lax.broadcasted_iota(i32, (rows, LANES), 0)
    qpos = past_len + (lax.broadcasted_iota(i32, (rows, LANES), 1) & 3)
    cur = lax.shift_right_logical(qpos, SEL_SHIFT)
    forced = (blk == 0) | (blk == cur) | (blk == cur - 1)
    future = blk * SEL_BLOCK > qpos
    imp = jnp.where(forced, FORCE, jnp.where(future, -FORCE, imp))
    imp = jnp.where(blk < n_sel, imp, -jnp.inf)
    sel = _top_rows(imp, min(SEL_TOP, n_sel))
    sb = jnp.where(sel, 0.0, NEG)
    past_blocks = past_len // SEL_BLOCK
    sb_rows = sb[0:past_blocks].T[0:SAMPLE_COLS]

    for k in range(n_pages):
        lanes = slice(k * PAGE_SIZE, (k + 1) * PAGE_SIZE)
        ke_ref[0:half, lanes] = buf_ref[slot, k, 0:half, :].astype(bf16)
        vt_ref[:, lanes] = buf_ref[slot, k, half:, :].astype(bf16)
    ke_ref[half:, :] = emat_ref[...]
    n_far = past_len - PAGE_SIZE
    s = _dot(jnp.concatenate([q_rows, sb_rows.astype(bf16)], axis=1), ke_ref[...])
    s = jnp.concatenate([s[:, :n_far] + c31_ref[:, 0:1], s[:, n_far:] + sk_ref[...]], axis=1)
    knew = knew_ref[0].astype(bf16)
    s_n = _dot(q_rows, knew[0:half]) + snew_ref[...]
    p, p_n = _softmax_rows2(s, s_n)
    o_s = _dot_nt(p, vt_ref[...]) + _dot_nt(p_n, knew[half:])

    win = win_ref[0].astype(bf16)
    wnew = wnew_ref[0].astype(bf16)
    p, p_n = _softmax_rows2(_dot(q_rows, win[0:half]) + swin_ref[...], _dot(q_rows, wnew[0:half]) + snew_ref[...])
    o_w = _dot_nt(p, win[half:]) + _dot_nt(p_n, wnew[half:])

    gt = g_ref[0]
    out_ref[0] = gt[:, 0:1] * o_c + gt[:, 1:2] * o_s + gt[:, 2:3] * o_w


def _attn_sample(q_pk, gates, kcp, pages_t, page_table, ks_new, win_t, kw_new, tabs):
    scmp, sk, snew, swin, c31s = tabs
    bd, t_new, _ = q_pk.shape
    n_pages = page_table.shape[1]
    past_len = n_pages * PAGE_SIZE
    ncp = kcp.shape[1]
    n_cmp = past_len // CMP_STRIDE - 1
    n_sel = -(-(past_len + t_new) // SEL_BLOCK)
    n_sel_rows = -(-n_sel // SUBLANES) * SUBLANES
    past_blocks = past_len // SEL_BLOCK
    n_cols = N_KV * GROUP * t_new
    assert n_cols == SAMPLE_COLS and win_t.shape[2] == WINDOW and past_blocks == LANES
    qc = q_pk.reshape(bd, t_new, N_HEADS, LANES).transpose(0, 2, 1, 3).reshape(bd, n_cols, LANES)
    qc = jnp.pad(qc, ((0, 0), (0, LANES - n_cols), (0, 0)))
    gc = gates[:, :, :3 * N_HEADS].reshape(bd, t_new, N_HEADS, 3).transpose(0, 2, 1, 3).reshape(bd, n_cols, 3)
    gc = jnp.pad(gc, ((0, 0), (0, 0), (0, SUBLANES - 3)))
    new_t = lambda a: jnp.pad(jnp.swapaxes(a, 1, 2), ((0, 0), (0, 0), (0, LANES - t_new)))
    ovt = _overlap_t(n_sel_rows, n_sel, ncp, n_cmp)
    c = np.arange(LANES)
    same = (c[:, None] // (GROUP * t_new) == c[None, :] // (GROUP * t_new)) & (c[:, None] % t_new == c[None, :] % t_new)
    rmat = jnp.asarray(same & (c[:, None] < n_cols) & (c[None, :] < n_cols), dtype=bf16)
    emat = jnp.asarray(np.arange(past_blocks)[:, None] == np.arange(past_len)[None, :] // SEL_BLOCK, dtype=bf16)

    per_b = lambda shape: pl.BlockSpec((1,) + shape, lambda i, pt: (i, 0, 0))
    res = lambda a: pl.BlockSpec(a.shape, lambda i, pt: (0,) * a.ndim)
    half = KV_W // 2
    grid_spec = pltpu.PrefetchScalarGridSpec(
        num_scalar_prefetch=1,
        grid=(bd,),
        in_specs=[pl.BlockSpec(memory_space=pl.ANY),
                  per_b((LANES, LANES)), per_b((n_cols, SUBLANES)), per_b((ncp, KV_W)), per_b((KV_W, LANES)),
                  per_b((KV_W, WINDOW)), per_b((KV_W, LANES)),
                  res(ovt), res(rmat), res(emat), res(scmp), res(sk), res(snew), res(swin), res(c31s)],
        out_specs=per_b((n_cols, LANES)),
        scratch_shapes=[pltpu.VMEM((2, n_pages, KV_W, PAGE_SIZE), f32),
                        pltpu.VMEM((half + past_blocks, past_len), bf16), pltpu.VMEM((half, past_len), bf16),
                        pltpu.SemaphoreType.DMA((2,))],
    )
    o = pl.pallas_call(
        functools.partial(_attn_sample_kernel, n_pages=n_pages, n_sel=n_sel, past_len=past_len),
        grid_spec=grid_spec,
        out_shape=jax.ShapeDtypeStruct((bd, n_cols, LANES), f32),
        compiler_params=_cparams(("arbitrary",)),
        name="attn_sample",
    )(page_table, pages_t, qc, gc, kcp, new_t(ks_new), win_t, new_t(kw_new),
      ovt, rmat, emat, scmp, sk, snew, swin, c31s)
    o = o.reshape(bd, N_KV, GROUP, t_new, N_KV, HEAD_DIM)
    o = jnp.stack([o[:, g, :, :, g, :] for g in range(N_KV)], axis=1)
    return o.transpose(0, 3, 1, 2, 4).reshape(bd, t_new, ATTN_DIM)


def _merge_kernel(x_ref, a_ref, u0_ref, halo_ref, u1s_ref, u2s_ref, cb_ref, mg_ref, cw_ref, cbias_ref,
                  wau_ref, wcu_ref, wo_ref, g2_ref, x2_ref, h2_ref, *, d_model, prompt_tiles, seq_tiles):
    i = pl.program_id(0)
    u0 = u0_ref[...]
    r = lax.broadcasted_iota(i32, u0.shape, 0)
    keep = jnp.where(i % seq_tiles == 0, 0.0, 1.0)
    p1 = halo_ref[SUBLANES - 1:SUBLANES, :] * keep
    p2 = halo_ref[SUBLANES - 2:SUBLANES - 1, :] * keep
    u1 = jnp.where(r == 0, p1, pltpu.roll(u0, 1, axis=0))
    u2 = jnp.where(r == 0, p2, jnp.where(r == 1, p1, pltpu.roll(u0, 2, axis=0)))
    decode = i >= prompt_tiles
    u1 = jnp.where(decode, u1s_ref[...], u1)
    u2 = jnp.where(decode, u2s_ref[...], u2)
    y = cbias_ref[...] + cw_ref[0:1, :] * u2
    y = y + cw_ref[1:2, :] * u1
    y = y + cw_ref[2:3, :] * u0
    up_a = _dot(a_ref[...].astype(bf16), wau_ref[...])
    up_c = _dot((cb_ref[...] * y).astype(bf16), wcu_ref[...])
    mixed = mg_ref[:, :d_model] * up_a + mg_ref[:, d_model:] * up_c
    x2 = x_ref[...] + _dot(mixed.astype(bf16), wo_ref[...])
    x2_ref[...] = x2
    h2_ref[...] = _rmsnorm(x2, g2_ref[...]).astype(bf16)


def _merge(x, attn, cin, u1s, u2s, cb, mg, conv_w, conv_b, w_attn_up, w_conv_up, w_out, norm2_g, n_prompt, t):
    n, d = x.shape
    tm = ROW_TILE
    assert CONV_W == 3 and t % tm == 0 and n_prompt % tm == 0 and (n - n_prompt) % tm == 0
    prompt_tiles = n_prompt // tm
    row = lambda w: pl.BlockSpec((tm, w), lambda i: (i, 0))
    halo = pl.BlockSpec((SUBLANES, CONV_DIM), lambda i: (jnp.maximum(i * (tm // SUBLANES) - 1, 0), 0))
    dec = pl.BlockSpec((tm, CONV_DIM), lambda i: (jnp.maximum(i - prompt_tiles, 0), 0))
    cw = jnp.pad(conv_w, ((0, SUBLANES - CONV_W), (0, 0)))
    return pl.pallas_call(
        functools.partial(_merge_kernel, d_model=d, prompt_tiles=prompt_tiles, seq_tiles=t // tm),
        grid=(n // tm,),
        in_specs=[row(d), row(ATTN_DIM), row(CONV_DIM), halo, dec, dec, row(CONV_DIM), row(2 * d),
                  _resident(cw.shape), _resident((1, CONV_DIM)), _resident(w_attn_up.shape),
                  _resident(w_conv_up.shape), _resident(w_out.shape), _resident((1, d))],
        out_specs=[row(d), row(d)],
        out_shape=[jax.ShapeDtypeStruct((n, d), f32), jax.ShapeDtypeStruct((n, d), bf16)],
        compiler_params=_cparams(("parallel",)),
        name="merge",
    )(x, attn, cin, cin, u1s, u2s, cb, mg, cw, conv_b.reshape(1, CONV_DIM), w_attn_up.astype(bf16),
      w_conv_up.astype(bf16), w_out.astype(bf16), norm2_g.reshape(1, d))


def _top_rows_sorted(s, k):
    rows = lax.broadcasted_iota(i32, s.shape, 0).astype(f32)
    vals, idxs = [], []
    for _ in range(k):
        m = jnp.max(s, axis=0, keepdims=True)
        first = jnp.min(jnp.where(s == m, rows, 1e9), axis=0, keepdims=True)
        vals.append(m)
        idxs.append(first)
        s = jnp.where(rows == first, -jnp.inf, s)
    return jnp.concatenate(vals, axis=0), jnp.concatenate(idxs, axis=0)


def _peer_route_kernel(h_ref, wq_ref, k1_ref, k2_ref, a_ref, b_ref, g_ref):
    qp = _dot(h_ref[...], wq_ref[...]).astype(bf16)
    key_dim = 2 * PEER_HALF
    a_rows, b_rows, g_rows = [], [], []
    for h in range(PEER_HEADS):
        q1 = qp[:, h * key_dim:h * key_dim + PEER_HALF]
        q2 = qp[:, h * key_dim + PEER_HALF:(h + 1) * key_dim]
        v1, i1 = _top_rows_sorted(_dot_nt(k1_ref[...], q1), PEER_TOPK)
        v2, i2 = _top_rows_sorted(_dot_nt(k2_ref[...], q2), PEER_TOPK)
        counts = [PEER_TOPK // (r + 1) for r in range(PEER_TOPK)]
        cand = jnp.concatenate([v1[r:r + 1, :] + v2[0:n, :] for r, n in enumerate(counts)], axis=0)
        code = jnp.concatenate([i1[r:r + 1, :] * PEER_NKEYS + i2[0:n, :] for r, n in enumerate(counts)], axis=0)
        pad = -sum(counts) % SUBLANES
        cand = jnp.concatenate([cand, jnp.full((pad, cand.shape[1]), -jnp.inf, f32)], axis=0)
        code = jnp.concatenate([code, jnp.zeros((pad, code.shape[1]), f32)], axis=0)
        rows = lax.broadcasted_iota(i32, cand.shape, 0).astype(f32)
        sc, ex = [], []
        for _ in range(PEER_TOPK):
            m = jnp.max(cand, axis=0, keepdims=True)
            first = jnp.min(jnp.where(cand == m, rows, 1e9), axis=0, keepdims=True)
            pick = rows == first
            sc.append(m)
            ex.append(jnp.max(jnp.where(pick, code, -1.0), axis=0, keepdims=True))
            cand = jnp.where(pick, -jnp.inf, cand)
        sc = jnp.concatenate(sc, axis=0)
        ex = jnp.concatenate(ex, axis=0)
        e1 = jnp.floor(ex * (1.0 / PEER_NKEYS))
        e = jnp.exp(sc - sc[0:1, :])
        a_rows.append(e1)
        b_rows.append(ex - e1 * PEER_NKEYS)
        g_rows.append(e / jnp.sum(e, axis=0, keepdims=True))
    tn = h_ref.shape[0]
    for src, dst in ((a_rows, a_ref), (b_rows, b_ref), (g_rows, g_ref)):
        full = jnp.concatenate(src, axis=0)
        for c in range(tn // LANES):
            dst[c * LANES:(c + 1) * LANES, :] = full[:, c * LANES:(c + 1) * LANES].T


def _peer_route(h2, wq, k1, k2):
    n, d = h2.shape
    tn = ROW_TILE
    assert PEER_HEADS * PEER_TOPK == LANES
    row = lambda w: pl.BlockSpec((tn, w), lambda i: (i, 0))
    return pl.pallas_call(
        _peer_route_kernel,
        grid=(n // tn,),
        in_specs=[row(d), _resident(wq.shape), _resident(k1.shape), _resident(k2.shape)],
        out_specs=[row(LANES)] * 3,
        out_shape=[jax.ShapeDtypeStruct((n, LANES), f32)] * 3,
        compiler_params=_cparams(("parallel",)),
        name="peer_route",
    )(h2, wq.astype(bf16), k1.astype(bf16), k2.astype(bf16))


def _peer_expert_kernel(h_ref, a_ref, b_ref, w_ref, x_ref, u_ref, v_ref, out_ref, g_ref, acc_ref, *, pitch):
    e_step = pl.program_id(1)
    tn = h_ref.shape[0]
    keys_per_chunk = EXPERT_CHUNK // PEER_NKEYS
    half_keys = PEER_NKEYS // 2
    hi_mask = jnp.uint32(0xFFFF0000)

    @pl.when(e_step == 0)
    def _build_gate_matrix():
        key_row = lax.broadcasted_iota(i32, (PEER_NKEYS, LANES), 0).astype(f32)

        def token(n, c):
            a = jnp.broadcast_to(a_ref[pl.ds(n, 1), :], (PEER_NKEYS, LANES))
            b = jnp.broadcast_to(b_ref[pl.ds(n, 1), :], (PEER_NKEYS, LANES))
            w = jnp.broadcast_to(w_ref[pl.ds(n, 1), :], (PEER_NKEYS, LANES))
            pa = (a == key_row).astype(bf16)
            wb = jnp.where(b == key_row, w, 0.0).astype(bf16)
            g = _dot_nt(pa, wb).astype(bf16).astype(f32)
            bits = lax.bitcast_convert_type(g, jnp.uint32)
            word = lax.shift_right_logical(bits[:half_keys], jnp.uint32(16)) | (bits[half_keys:] & hi_mask)
            g_ref[pl.ds(n, half_keys, stride=pitch), :] = word
            return c

        lax.fori_loop(0, tn, token, 0, unroll=TOKEN_UNROLL)
        acc_ref[...] = jnp.zeros_like(acc_ref)

    s = _dot_nt(h_ref[...], u_ref[...])
    key0 = (e_step * keys_per_chunk) % half_keys
    words = jnp.concatenate(
        [g_ref[pl.ds(pl.multiple_of((key0 + j) * pitch, SUBLANES), tn), :] for j in range(keys_per_chunk)], axis=1)
    shift = jnp.where(e_step * keys_per_chunk < half_keys, 16, 0).astype(jnp.uint32)
    gates = lax.bitcast_convert_type(lax.shift_left(words, shift) & hi_mask, f32)
    act = 0.5 * s * (1.0 + lax.erf(s * math.sqrt(0.5)))
    acc_ref[...] += _dot((gates * act).astype(bf16), v_ref[...])

    @pl.when(e_step == pl.num_programs(1) - 1)
    def _finish():
        out_ref[...] = x_ref[...] + acc_ref[...]


def _peer_expert(h2, a_idx, b_idx, gate_w, x2, u_tab, v_tab):
    n, d = h2.shape
    n_exp = u_tab.shape[0]
    tn = PEER_TILE
    pitch = tn + G_PITCH_PAD
    assert n % tn == 0 and n_exp == PEER_NKEYS * PEER_NKEYS and (PEER_NKEYS // 2) % (EXPERT_CHUNK // PEER_NKEYS) == 0
    row = lambda w: pl.BlockSpec((tn, w), lambda i, e: (i, 0))
    tab = pl.BlockSpec((EXPERT_CHUNK, d), lambda i, e: (e, 0))
    return pl.pallas_call(
        functools.partial(_peer_expert_kernel, pitch=pitch),
        grid=(n // tn, n_exp // EXPERT_CHUNK),
        in_specs=[row(d), row(LANES), row(LANES), row(LANES), row(d), tab, tab],
        out_specs=row(d),
        out_shape=jax.ShapeDtypeStruct((n, d), f32),
        scratch_shapes=[pltpu.VMEM((PEER_NKEYS // 2 * pitch, LANES), jnp.uint32), pltpu.VMEM((tn, d), f32)],
        compiler_params=_cparams(("parallel", "arbitrary")),
        name="peer_expert",
    )(h2, a_idx, b_idx, gate_w, x2, u_tab.astype(bf16), v_tab.astype(bf16))


def _ple_kernel(x_ref, p_ref, pg_ref, wg_ref, wp_ref, fg_ref, y_ref):
    x = x_ref[...]
    gate = jax.nn.sigmoid(_dot(_rmsnorm(x, pg_ref[...]).astype(bf16), wg_ref[...]))
    x = x + gate * _dot(p_ref[...].astype(bf16), wp_ref[...])
    y_ref[...] = _rmsnorm(x, fg_ref[...])


def _ple(x3, p, ple_g, w_ple_gate, w_ple_proj, final_g):
    n, d = x3.shape
    tm = ROW_TILE
    row = lambda w: pl.BlockSpec((tm, w), lambda i: (i, 0))
    return pl.pallas_call(
        _ple_kernel,
        grid=(n // tm,),
        in_specs=[row(d), row(p.shape[1]), _resident((1, d)), _resident(w_ple_gate.shape),
                  _resident(w_ple_proj.shape), _resident((1, d))],
        out_specs=row(d),
        out_shape=jax.ShapeDtypeStruct((n, d), f32),
        compiler_params=_cparams(("parallel",)),
        name="ple_final",
    )(x3, p, ple_g.reshape(1, d), w_ple_gate.astype(bf16), w_ple_proj.astype(bf16), final_g.reshape(1, d))


def kernel(x_prompt, x_sample, p_prompt, p_sample, cache_cmp_kv, cache_slc_kv, page_table, state_win_kv, state_conv, norm1_g, w_in, w_cmp, cmp_pos, conv_w, conv_b, w_attn_up, w_conv_up, w_out, norm2_g, peer_wq, peer_k1, peer_k2, peer_u, peer_v, ple_g, w_ple_gate, w_ple_proj, rel_bias, final_g):
    depth = norm1_g.shape[0]
    assert depth == 1, "single-layer trunk"
    b, t, d = x_prompt.shape
    bd, t_new, _ = x_sample.shape
    n_pages = page_table.shape[1]
    past_len = n_pages * PAGE_SIZE
    n_p, n_s = b * t, bd * t_new
    kv_shape = lambda lead: lead + (2, N_KV, HEAD_DIM)

    x_all = jnp.concatenate([x_prompt.reshape(n_p, d), x_sample.reshape(n_s, d)], axis=0)
    w_pack = _pack_w_in(w_in[0], d)
    q_pk, kvc, kvs, kvw, kvs_bf, kvw_bf, gates, cin, cbg, mg = _in_proj(x_all, norm1_g[0], w_pack)
    prompt = lambda a: a[:n_p].reshape(b, t, -1)
    sample = lambda a: a[n_p:].reshape(bd, t_new, -1)

    w4, pos_rows = _pack_w_cmp(w_cmp[0], cmp_pos[0])
    kcp_p = _compress_prompt(kvc, w4, pos_rows, b, t)
    kcp_s = _compress_sample(_pages_t(cache_cmp_kv[0]), page_table, w4, pos_rows)
    n_cmp_s = past_len // CMP_STRIDE - 1
    wt, ct, *sample_tabs = _tables(rel_bias, past_len, kcp_s.shape[1], n_cmp_s)

    kvs_s, kvw_s = sample(kvs), sample(kvw)
    attn_p = _attn_prompt(q_pk, gates, kcp_p, kvs_bf, kvw_bf, wt, ct, b, t)
    win_t = jnp.transpose(state_win_kv[0], (0, 2, 3, 4, 1)).reshape(bd, KV_W, -1)
    attn_s = _attn_sample(sample(q_pk), sample(gates), kcp_s, _pages_t(cache_slc_kv[0]), page_table, kvs_s, win_t,
                          kvw_s, sample_tabs)
    attn = jnp.concatenate([attn_p, attn_s.reshape(n_s, ATTN_DIM)], axis=0)

    cin_s = sample(cin)
    buf_s = state_conv[0]
    back = lambda k: jnp.concatenate([buf_s[:, CONV_W - 1 - k:], cin_s[:, :t_new - k]], axis=1).reshape(n_s, CONV_DIM)
    x2, h2 = _merge(x_all, attn, cin, back(1), back(2), cbg, mg, conv_w[0], conv_b[0], w_attn_up[0], w_conv_up[0],
                    w_out[0], norm2_g[0], n_p, t)

    a_idx, b_idx, gate_w = _peer_route(h2, peer_wq[0], peer_k1[0], peer_k2[0])
    x3 = _peer_expert(h2, a_idx, b_idx, gate_w, x2, peer_u[0], peer_v[0])

    p_all = jnp.concatenate([p_prompt[0].reshape(n_p, -1), p_sample[0].reshape(n_s, -1)], axis=0)
    y = _ple(x3, p_all, ple_g[0], w_ple_gate[0], w_ple_proj[0], final_g)

    n_win = min(WINDOW, t)
    assert t >= CONV_W - 1 and t_new >= CONV_W - 1
    new_win_s = jnp.concatenate([state_win_kv[:, :, t_new:], kvw_s.reshape(kv_shape((1, bd, t_new)))], axis=2)
    return (
        y[:n_p].reshape(b, t, d),
        y[n_p:].reshape(bd, t_new, d),
        prompt(kvc).reshape(kv_shape((1, b, t))),
        sample(kvc).reshape(kv_shape((1, bd, t_new))),
        prompt(kvs).reshape(kv_shape((1, b, t))),
        kvs_s.reshape(kv_shape((1, bd, t_new))),
        prompt(kvw)[:, t - n_win:].reshape(kv_shape((1, b, n_win))),
        new_win_s,
        prompt(cin)[None, :, t - (CONV_W - 1):],
        cin_s[None, :, t_new - (CONV_W - 1):],
    )
```

```python
import functools
import math

import jax
import jax.numpy as jnp
import numpy as np
from jax import lax
from jax.experimental import pallas as pl
from jax.experimental.pallas import tpu as pltpu

f32 = jnp.float32
bf16 = jnp.bfloat16
i32 = jnp.int32

N_HEADS = 8
N_KV = 2
GROUP = N_HEADS // N_KV
HEAD_DIM = 64
ATTN_DIM = N_HEADS * HEAD_DIM
KV_W = 2 * N_KV * HEAD_DIM
CMP_BLOCK = 32
CMP_STRIDE = 16
SEL_BLOCK = 64
SEL_SHIFT = 6
SEL_TOP = 16
WINDOW = 512
N_BUCKETS = 32
MAX_EXACT = N_BUCKETS // 2
MAX_DISTANCE = 128
CONV_DIM = 512
CONV_W = 3
PEER_HEADS = 8
PEER_NKEYS = 128
PEER_HALF = 128
PEER_TOPK = 16
PAGE_SIZE = 128
EPS = 1e-6
NEG = -1e30
FORCE = 1e4

LANES = 128
SUBLANES = 8
VMEM_LIMIT_BYTES = 56 * 1024 * 1024

Q_TILE = 128
SAMPLE_COLS = 32
FAR_CHUNK = 512
CMP_PAD = 16
CMP_WIN = 24
ONES_ROWS = 16
ROW_TILE = 256
PEER_TILE = 512
EXPERT_CHUNK = 1024
TOKEN_UNROLL = 16
G_PITCH_PAD = 8


def _cparams(sem):
    return pltpu.CompilerParams(dimension_semantics=sem, vmem_limit_bytes=VMEM_LIMIT_BYTES)


def _dot(a, b):
    return jnp.dot(a, b, preferred_element_type=f32)


def _dot_nt(a, b):
    return lax.dot_general(a, b, (((1,), (1,)), ((), ())), preferred_element_type=f32)


def _dot_tn(a, b):
    return lax.dot_general(a, b, (((0,), (0,)), ((), ())), preferred_element_type=f32)


def _split_bf16(x):
    hi = x.astype(bf16)
    lo = (x - hi.astype(f32)).astype(bf16)
    return hi, lo


def _resident(shape):
    nd = len(shape)
    return pl.BlockSpec(shape, lambda *_: (0,) * nd)


def _rmsnorm(x, g):
    ms = jnp.mean(x * x, axis=-1, keepdims=True)
    return x * lax.rsqrt(ms + EPS) * g


def _two_sources(first_ref, second_ref, first_tiles):
    return jnp.where(pl.program_id(0) < first_tiles, first_ref[...], second_ref[...])


def _two_specs(tile, width, first_tiles):
    return [pl.BlockSpec((tile, width), lambda i: (jnp.minimum(i, first_tiles - 1), 0)),
            pl.BlockSpec((tile, width), lambda i: (jnp.maximum(i - first_tiles, 0), 0))]


def _in_proj_kernel(xp_ref, xs_ref, g_ref, w_ref, q_ref, kc_ref, ks_ref, kw_ref, ksb_ref, kwb_ref,
                    gn_ref, cin_ref, cb_ref, mg_ref, *, d_model, prompt_tiles):
    qw = N_HEADS * LANES
    h = _rmsnorm(_two_sources(xp_ref, xs_ref, prompt_tiles), g_ref[...]).astype(bf16)
    o = 0
    q_ref[...] = (_dot(h, w_ref[:, o:o + qw]) * (HEAD_DIM ** -0.5)).astype(bf16)
    o += qw
    kc_ref[...] = _dot(h, w_ref[:, o:o + KV_W])
    o += KV_W
    ks = _dot(h, w_ref[:, o:o + KV_W])
    ks_ref[...] = ks
    ksb_ref[...] = ks.astype(bf16)
    o += KV_W
    kw = _dot(h, w_ref[:, o:o + KV_W])
    kw_ref[...] = kw
    kwb_ref[...] = kw.astype(bf16)
    o += KV_W
    gn_ref[...] = jax.nn.sigmoid(_dot(h, w_ref[:, o:o + LANES]))
    o += LANES
    cv = _dot(h, w_ref[:, o:o + CONV_DIM])
    o += CONV_DIM
    cb_ref[...] = _dot(h, w_ref[:, o:o + CONV_DIM])
    o += CONV_DIM
    cin_ref[...] = _dot(h, w_ref[:, o:o + CONV_DIM]) * cv
    o += CONV_DIM
    mg_ref[...] = jax.nn.sigmoid(_dot(h, w_ref[:, o:o + 2 * d_model]))


def _pack_w_in(w_in, d_model):
    sizes = (ATTN_DIM, KV_W, KV_W, KV_W, 3 * N_HEADS, CONV_DIM, CONV_DIM, CONV_DIM, 2 * d_model)
    parts, s = [], 0
    for n in sizes:
        parts.append(w_in[:, s:s + n])
        s += n
    wq = parts[0].reshape(d_model, N_KV, GROUP, HEAD_DIM)
    z = jnp.zeros_like(wq)
    slabs = [jnp.concatenate([wq[:, 0], z[:, 0]], axis=-1), jnp.concatenate([z[:, 1], wq[:, 1]], axis=-1)]
    wq = jnp.stack(slabs, axis=1).reshape(d_model, N_HEADS * LANES)
    gn = jnp.pad(parts[4], ((0, 0), (0, LANES - 3 * N_HEADS)))
    return jnp.concatenate([wq, parts[1], parts[2], parts[3], gn] + parts[5:], axis=1).astype(bf16)


def _in_proj(x_prompt, x_sample, norm_g, w_pack):
    d = x_prompt.shape[1]
    n = x_prompt.shape[0] + x_sample.shape[0]
    tm = ROW_TILE
    assert x_prompt.shape[0] % tm == 0 and x_sample.shape[0] % tm == 0
    prompt_tiles = x_prompt.shape[0] // tm
    row = lambda w: pl.BlockSpec((tm, w), lambda i: (i, 0))
    widths = (N_HEADS * LANES, KV_W, KV_W, KV_W, KV_W, KV_W, LANES, CONV_DIM, CONV_DIM, 2 * d)
    dtypes = (bf16, f32, f32, f32, bf16, bf16, f32, f32, f32, f32)
    return pl.pallas_call(
        functools.partial(_in_proj_kernel, d_model=d, prompt_tiles=prompt_tiles),
        grid=(n // tm,),
        in_specs=_two_specs(tm, d, prompt_tiles) + [_resident((1, d)), _resident(w_pack.shape)],
        out_specs=[row(w) for w in widths],
        out_shape=[jax.ShapeDtypeStruct((n, w), t) for w, t in zip(widths, dtypes)],
        compiler_params=_cparams(("parallel",)),
        name="in_proj",
    )(x_prompt, x_sample, norm_g.reshape(1, d), w_pack)


def _bias_of_dist(dist, valid, rb_ref, head, shift=0.0):
    n = jnp.maximum(dist, 0)
    nf = jnp.maximum(n, 1).astype(f32)
    large = MAX_EXACT + jnp.floor(jnp.log(nf / MAX_EXACT) / math.log(MAX_DISTANCE / MAX_EXACT)
                                  * (N_BUCKETS - MAX_EXACT)).astype(i32)
    large = jnp.minimum(large, N_BUCKETS - 1)
    bucket = jnp.where(n < MAX_EXACT, n, large)
    out = jnp.zeros(dist.shape, f32)
    for b in range(N_BUCKETS):
        out = jnp.where(bucket == b, rb_ref[b, head], out)
    return jnp.where(valid, out - shift, NEG)


def _tables_kernel(rb_ref, wt_ref, ct_ref, scmp_ref, sk_ref, snew_ref, swin_ref, c31s_ref,
                   *, past_len, ncp_s, n_cmp_s):
    ik = lax.broadcasted_iota(i32, (Q_TILE, Q_TILE), 0)
    iq = lax.broadcasted_iota(i32, (Q_TILE, Q_TILE), 1)
    n_w = WINDOW // Q_TILE + 1
    for g in range(N_KV):
        wt_ref[g, 0] = jnp.full(wt_ref.shape[2:], NEG, f32)
        for hh in range(GROUP):
            head = g * GROUP + hh
            far = rb_ref[N_BUCKETS - 1, head]
            cols = slice(hh * Q_TILE, (hh + 1) * Q_TILE)
            for w in range(n_w):
                dist = Q_TILE * (n_w - 1 - w) + iq - ik
                wt_ref[g, w + 1, :, cols] = _bias_of_dist(dist, (dist >= 0) & (dist < WINDOW), rb_ref, head, far)
            mm = lax.broadcasted_iota(i32, (CMP_WIN, Q_TILE), 0)
            jq = lax.broadcasted_iota(i32, (CMP_WIN, Q_TILE), 1)
            dist = jq - CMP_STRIDE * (mm - CMP_PAD) - (CMP_BLOCK - 1)
            ct_ref[g, :, cols] = _bias_of_dist(dist, dist >= 0, rb_ref, head, far)
    def sample_table(shape, q_axis, dist_fn, valid_fn, out_ref):
        c = lax.broadcasted_iota(i32, shape, q_axis)
        r = lax.broadcasted_iota(i32, shape, 1 - q_axis)
        t = c & 3
        hd = jnp.minimum(lax.shift_right_logical(c, 2), N_HEADS - 1)
        dist = dist_fn(r, t)
        valid = valid_fn(r, t, dist)
        acc = jnp.zeros(shape, f32)
        for head in range(N_HEADS):
            acc = jnp.where(hd == head, _bias_of_dist(dist, valid, rb_ref, head), acc)
        out_ref[...] = acc

    sample_table((ncp_s, LANES), 1, lambda r, t: past_len + t - CMP_STRIDE * (r - CMP_PAD) - (CMP_BLOCK - 1),
                 lambda r, t, d: (r >= CMP_PAD) & (r < CMP_PAD + n_cmp_s) & (d >= 0), scmp_ref)
    sample_table((SAMPLE_COLS, PAGE_SIZE), 0, lambda r, t: PAGE_SIZE + t - r, lambda r, t, d: d >= 0, sk_ref)
    sample_table((SAMPLE_COLS, LANES), 0, lambda r, t: t - r, lambda r, t, d: (d >= 0) & (r < 4), snew_ref)
    sample_table((SAMPLE_COLS, WINDOW), 0, lambda r, t: WINDOW + t - r, lambda r, t, d: d < WINDOW, swin_ref)
    sample_table((SAMPLE_COLS, LANES), 0, lambda r, t: jnp.full_like(r, MAX_DISTANCE), lambda r, t, d: d > 0, c31s_ref)


def _tables(rel_bias, past_len, ncp_s, n_cmp_s):
    n_w = WINDOW // Q_TILE + 1
    cols = GROUP * Q_TILE
    shapes = [
        (N_KV, n_w + 1, Q_TILE, cols), (N_KV, CMP_WIN, cols),
        (ncp_s, LANES), (SAMPLE_COLS, PAGE_SIZE), (SAMPLE_COLS, LANES), (SAMPLE_COLS, WINDOW), (SAMPLE_COLS, LANES),
    ]
    return pl.pallas_call(
        functools.partial(_tables_kernel, past_len=past_len, ncp_s=ncp_s, n_cmp_s=n_cmp_s),
        in_specs=[pl.BlockSpec(memory_space=pltpu.SMEM)],
        out_shape=[jax.ShapeDtypeStruct(s, f32) for s in shapes],
        compiler_params=pltpu.CompilerParams(vmem_limit_bytes=VMEM_LIMIT_BYTES),
        name="bias_tables",
    )(rel_bias)


def _ncp(n_chunks):
    return -(-(n_chunks + CMP_PAD + SUBLANES) // LANES) * LANES


def _compress_core(chunk_rows, w_ref, pos_ref, out_ref, n_chunks):
    half = KV_W // 2
    out_ref[...] = jnp.zeros(out_ref.shape, out_ref.dtype)
    for c in range(2):
        fs = None
        pb = None
        for p in range(CMP_STRIDE // 2):
            x = jnp.concatenate([chunk_rows(2 * p, c), chunk_rows(2 * p + 1, c)], axis=1).astype(bf16)
            lanes = [slice((2 * p + k) * KV_W + c * half, (2 * p + k) * KV_W + (c + 1) * half) for k in range(2)]
            px = jnp.concatenate([pos_ref[:, lanes[0]], pos_ref[:, lanes[1]]], axis=1)
            fs = _dot(x, w_ref[p, c]) if fs is None else fs + _dot(x, w_ref[p, c])
            pb = _dot(px, w_ref[p, c]) if pb is None else pb + _dot(px, w_ref[p, c])
        bias = pb[0:1, :half] + pb[1:2, half:]
        nxt = pltpu.roll(fs[:, half:], n_chunks - 1, axis=0)
        kc = fs[:, :half] + nxt + bias
        r = lax.broadcasted_iota(i32, kc.shape, 0)
        kc = jnp.where(r < n_chunks - 1, kc, 0.0)
        out_ref[0, CMP_PAD:CMP_PAD + n_chunks, c * half:(c + 1) * half] = kc.astype(out_ref.dtype)


def _compress_prompt_kernel(x_ref, w_ref, pos_ref, out_ref, *, n_chunks):
    half = KV_W // 2
    rows = lambda l, c: x_ref[:, l * KV_W + c * half:l * KV_W + (c + 1) * half]
    _compress_core(rows, w_ref, pos_ref, out_ref, n_chunks)


def _gather_pages(pt_ref, pages_hbm, buf_ref, sem_ref, n_pages):
    b = pl.program_id(0)
    slot = b % 2

    def copies(seq, into):
        return [pltpu.make_async_copy(pages_hbm.at[pt_ref[seq, k]], buf_ref.at[into, k], sem_ref.at[into])
                for k in range(n_pages)]

    @pl.when(b == 0)
    def _first():
        for c in copies(0, 0):
            c.start()

    @pl.when(b + 1 < pl.num_programs(0))
    def _next():
        for c in copies(b + 1, 1 - slot):
            c.start()

    for c in copies(b, slot):
        c.wait()
    return slot


def _compress_paged_kernel(pt_ref, pages_hbm, perm_ref, w_ref, pos_ref, out_ref, buf_ref, xp_ref, sem_ref,
                           *, n_pages, n_chunks):
    half = KV_W // 2
    per_page = PAGE_SIZE // CMP_STRIDE
    slot = _gather_pages(pt_ref, pages_hbm, buf_ref, sem_ref, n_pages)
    for k in range(n_pages):
        xp_ref[k] = _dot_nt(perm_ref[...], buf_ref[slot, k].astype(bf16))

    def rows(l, c):
        x = xp_ref[:, l * per_page:(l + 1) * per_page, c * half:(c + 1) * half]
        return x.reshape(n_pages * per_page, half)

    _compress_core(rows, w_ref, pos_ref, out_ref, n_chunks)


def _pack_w_cmp(w_cmp, cmp_pos):
    w = w_cmp.reshape(2, CMP_STRIDE // 2, 2, 2, HEAD_DIM, HEAD_DIM)
    w4 = jnp.einsum("hplcde,gG->pclgdhGe", w, jnp.eye(N_KV, dtype=f32))
    w4 = w4.reshape(CMP_STRIDE // 2, 2, KV_W, KV_W)

    def pos_row(p):
        return jnp.broadcast_to(p[:, :, None, :], (CMP_STRIDE, 2, N_KV, HEAD_DIM)).reshape(1, -1)

    pos = jnp.concatenate([pos_row(cmp_pos[:CMP_STRIDE]), pos_row(cmp_pos[CMP_STRIDE:]),
                           jnp.zeros((SUBLANES - 2, CMP_STRIDE * KV_W), f32)], axis=0)
    return w4.astype(bf16), pos.astype(bf16)


def _compress_prompt(kv_cmp, w4, pos, b, t):
    n_chunks = t // CMP_STRIDE
    cw = CMP_STRIDE * KV_W
    chunks = kv_cmp.reshape(-1, cw)
    ncp = _ncp(n_chunks)
    return pl.pallas_call(
        functools.partial(_compress_prompt_kernel, n_chunks=n_chunks),
        grid=(b,),
        in_specs=[pl.BlockSpec((n_chunks, cw), lambda i: (i, 0)), _resident(w4.shape), _resident(pos.shape)],
        out_specs=pl.BlockSpec((1, ncp, KV_W), lambda i: (i, 0, 0)),
        out_shape=jax.ShapeDtypeStruct((b, ncp, KV_W), bf16),
        compiler_params=_cparams(("parallel",)),
        name="compress_prompt",
    )(chunks, w4, pos)


def _pages_t(cache):
    n_pool = cache.shape[0]
    return jnp.transpose(cache, (0, 2, 3, 4, 1)).reshape(n_pool, KV_W, cache.shape[1])


def _compress_sample(pages_t, page_table, w4, pos):
    bd, n_pages = page_table.shape
    per_page = PAGE_SIZE // CMP_STRIDE
    n_chunks = n_pages * per_page
    ncp = _ncp(n_chunks)
    tok = np.arange(PAGE_SIZE)
    perm = jnp.asarray(tok[None, :] == (tok[:, None] % per_page) * CMP_STRIDE + tok[:, None] // per_page, dtype=bf16)

    res = lambda a: pl.BlockSpec(a.shape, lambda i, pt: (0,) * a.ndim)
    grid_spec = pltpu.PrefetchScalarGridSpec(
        num_scalar_prefetch=1,
        grid=(bd,),
        in_specs=[pl.BlockSpec(memory_space=pl.ANY), res(perm), res(w4), res(pos)],
        out_specs=pl.BlockSpec((1, ncp, KV_W), lambda i, pt: (i, 0, 0)),
        scratch_shapes=[pltpu.VMEM((2, n_pages, KV_W, PAGE_SIZE), f32), pltpu.VMEM((n_pages, PAGE_SIZE, KV_W), f32),
                        pltpu.SemaphoreType.DMA((2,))],
    )
    return pl.pallas_call(
        functools.partial(_compress_paged_kernel, n_pages=n_pages, n_chunks=n_chunks),
        grid_spec=grid_spec,
        out_shape=jax.ShapeDtypeStruct((bd, ncp, KV_W), bf16),
        compiler_params=_cparams(("arbitrary",)),
        name="compress_sample",
    )(page_table, pages_t, perm, w4, pos)


def _overlap_t(n_sel_rows, n_sel, ncp, n_cmp):
    j = np.arange(n_sel_rows)[:, None]
    n = np.arange(ncp)[None, :] - CMP_PAD
    hit = (n * CMP_STRIDE < (j + 1) * SEL_BLOCK) & (n * CMP_STRIDE + CMP_BLOCK > j * SEL_BLOCK)
    hit &= (n >= 0) & (n < n_cmp) & (j < n_sel)
    return jnp.asarray(hit, dtype=bf16)


def _masked_softmax_cols(s, maybe_empty=True):
    m = jnp.max(s, axis=0, keepdims=True)
    e = jnp.exp(s - m)
    inv = 1.0 / jnp.sum(e, axis=0, keepdims=True)
    if maybe_empty:
        inv = jnp.where(m > 0.5 * NEG, inv, 0.0)
    return e * inv


def _top_rows(imp, k):
    rows = lax.broadcasted_iota(i32, imp.shape, 0).astype(f32)
    sel = jnp.zeros(imp.shape, jnp.bool_)
    v = imp
    for _ in range(k):
        m = jnp.max(v, axis=0, keepdims=True)
        first = jnp.min(jnp.where(v == m, rows, 1e9), axis=0, keepdims=True)
        pick = rows == first
        sel = sel | pick
        v = jnp.where(pick, -jnp.inf, v)
    return sel


def _flash_steps(carry, scores, v_ext):
    m_new = [jnp.maximum(carry[2 * i], jnp.max(s, axis=0, keepdims=True)) for i, s in enumerate(scores)]
    p = [jnp.exp(s - m).astype(bf16) for s, m in zip(scores, m_new)]
    out = []
    for i in range(len(scores)):
        alpha = jnp.exp(carry[2 * i] - m_new[i])
        out.extend((m_new[i], alpha * carry[2 * i + 1] + _dot(v_ext, p[i])))
    return tuple(out)


def _attn_prompt_kernel(q_ref, gn_ref, kc_ref, vce_ref, ks_ref, vse_ref, kw_ref, vwe_ref, emat_ref, ove_ref,
                        wt_ref, ct_ref, out_ref, cadd_ref, *, n_sel, ncp):
    qt = pl.program_id(1)
    cols = GROUP * Q_TILE
    n_w = WINDOW // Q_TILE + 1
    vrows = KV_W // 2
    gates_t = gn_ref[...].T
    n_far = jnp.maximum(qt - 1, 0) // (FAR_CHUNK // Q_TILE)
    groups = range(N_KV)
    qp = [jnp.concatenate(
        [q_ref[:, (g * GROUP + hh) * LANES:(g * GROUP + hh + 1) * LANES] for hh in range(GROUP)], axis=0)
        for g in groups]

    wk = n_w * Q_TILE
    start = pl.multiple_of(qt * Q_TILE, Q_TILE)
    lc, sw = [], []
    for g in groups:
        ri = lax.broadcasted_iota(i32, (ncp, cols), 0)
        cadd_ref[g] = jnp.where((ri >= CMP_PAD) & (ri < SUBLANES * qt), 0.0, NEG)
        wr = SUBLANES * qt + lax.broadcasted_iota(i32, (CMP_WIN, cols), 0)
        cadd_ref[g, pl.ds(pl.multiple_of(SUBLANES * qt, SUBLANES), CMP_WIN), :] = jnp.where(wr >= CMP_PAD, ct_ref[g], NEG)
        lc.append(_dot_nt(kc_ref[0], qp[g]) + cadd_ref[g])
    for g in groups:
        tiles = [wt_ref[g, jnp.where(qt + w >= n_w - 1, w + 1, 0)] for w in range(n_w)]
        sw.append(_dot_nt(kw_ref[0, pl.ds(start, wk), :], qp[g]) + jnp.concatenate(tiles, axis=0))
    live, e_c, e_w = [], [], []
    for g in groups:
        m = jnp.max(lc[g], axis=0, keepdims=True)
        live.append(m > 0.5 * NEG)
        e_c.append(jnp.exp(lc[g] - m))
    for g in groups:
        e_w.append(jnp.exp(sw[g] - jnp.max(sw[g], axis=0, keepdims=True)).astype(bf16))

    o_c, o_w, qx = [], [], []
    for g in groups:
        oc = _dot(vce_ref[0], e_c[g].astype(bf16))
        o_c.append(oc[:vrows] * jnp.where(live[g], 1.0 / oc[vrows:vrows + 1], 0.0))
        e_hi, e_lo = _split_bf16(e_c[g])
        raw = _dot(ove_ref[...], e_hi) + _dot(ove_ref[...], e_lo)
        raw = raw[:n_sel] * jnp.where(live[g], 1.0 / raw[n_sel:n_sel + 1], 0.0)
        imp = raw[:, 0:Q_TILE]
        for hh in range(1, GROUP):
            imp = imp + raw[:, hh * Q_TILE:(hh + 1) * Q_TILE]

        blk = lax.broadcasted_iota(i32, (n_sel, Q_TILE), 0)
        qpos = qt * Q_TILE + lax.broadcasted_iota(i32, (n_sel, Q_TILE), 1)
        cur = lax.shift_right_logical(qpos, SEL_SHIFT)
        forced = (blk == 0) | (blk == cur) | (blk == cur - 1)
        future = blk * SEL_BLOCK > qpos
        imp = jnp.where(forced, FORCE, jnp.where(future, -FORCE, imp))
        sel = _top_rows(imp, min(SEL_TOP, n_sel))
        sb = jnp.where(sel, 0.0, NEG)
        if n_sel < LANES:
            sb = jnp.concatenate([sb, jnp.zeros((LANES - n_sel, Q_TILE), f32)], axis=0)
        sbq = sb.T.astype(bf16)
        qx.append(jnp.concatenate([qp[g], jnp.concatenate([sbq] * GROUP, axis=0)], axis=1))
    for g in groups:
        ow = _dot(vwe_ref[0, :, pl.ds(start, wk)], e_w[g])
        o_w.append(ow[:vrows] * (1.0 / ow[vrows:vrows + 1]))

    def scores(start, size, g):
        kx = jnp.concatenate([ks_ref[pl.ds(start, size), :], emat_ref[pl.ds(start, size), :]], axis=1)
        return _dot_nt(kx, qx[g])

    def far_step(i, carry):
        ks = pl.multiple_of(i * FAR_CHUNK, FAR_CHUNK)
        v_ext = vse_ref[0, :, pl.ds(ks, FAR_CHUNK)]
        return _flash_steps(carry, [scores(ks, FAR_CHUNK, g) for g in groups], v_ext)

    def near_step(kt, carry):
        ks = pl.multiple_of(kt * Q_TILE, Q_TILE)
        v_ext = vse_ref[0, :, pl.ds(ks, Q_TILE)]
        tile = jnp.where(kt >= qt - 1, kt - qt + n_w, 2)
        return _flash_steps(carry, [scores(ks, Q_TILE, g) + wt_ref[g, tile] for g in groups], v_ext)

    init = (jnp.full((1, cols), NEG, f32), jnp.zeros((vse_ref.shape[1], cols), f32)) * N_KV
    carry = lax.fori_loop(0, n_far, far_step, init)
    carry = lax.fori_loop(n_far * (FAR_CHUNK // Q_TILE), qt + 1, near_step, carry)

    heads_t = []
    for g in groups:
        acc = carry[2 * g + 1]
        o_s = acc[:vrows] * (1.0 / acc[vrows:vrows + 1])
        for hh in range(GROUP):
            head = g * GROUP + hh
            cs = slice(hh * Q_TILE, (hh + 1) * Q_TILE)
            ds = slice(g * HEAD_DIM, (g + 1) * HEAD_DIM)
            heads_t.append(gates_t[3 * head:3 * head + 1, :] * o_c[g][ds, cs]
                           + gates_t[3 * head + 1:3 * head + 2, :] * o_s[ds, cs]
                           + gates_t[3 * head + 2:3 * head + 3, :] * o_w[g][ds, cs])
    for pair in range(N_HEADS // 2):
        both = jnp.concatenate(heads_t[2 * pair:2 * pair + 2], axis=0)
        out_ref[:, pair * LANES:(pair + 1) * LANES] = both.T


def _attn_prompt(q_pk, gates, kcp, ks_bf, kw_bf, wt, ct, b, t):
    ncp = kcp.shape[1]
    n_sel = t // SEL_BLOCK
    n_cmp = t // CMP_STRIDE - 1
    n_qt = t // Q_TILE
    assert t % FAR_CHUNK == 0 and n_sel % SUBLANES == 0 and n_sel <= LANES
    cols = GROUP * Q_TILE
    half = KV_W // 2

    def values_t(v):
        v_t = jnp.swapaxes(v, 1, 2)
        return jnp.concatenate([v_t, jnp.ones((b, ONES_ROWS, v.shape[1]), v.dtype)], axis=1)

    prompt = lambda a: a[:b * t].reshape(b, t, -1)
    kw_pad = jnp.pad(prompt(kw_bf), ((0, 0), (WINDOW, 0), (0, 0)))
    vce, vse, vwe = values_t(kcp[:, :, half:]), values_t(prompt(ks_bf)[:, :, half:]), values_t(kw_pad[:, :, half:])
    ove = jnp.concatenate([_overlap_t(n_sel, n_sel, ncp, n_cmp), jnp.ones((ONES_ROWS, ncp), bf16)], axis=0)
    emat = jnp.asarray(np.arange(t)[:, None] // SEL_BLOCK == np.arange(LANES)[None, :], dtype=bf16)
    tp = t + WINDOW
    vr = half + ONES_ROWS
    per_b = lambda shape: pl.BlockSpec((1,) + shape, lambda i, j: (i, 0, 0))
    q_rows = lambda w: pl.BlockSpec((Q_TILE, w), lambda i, j: (i * n_qt + j, 0))
    return pl.pallas_call(
        functools.partial(_attn_prompt_kernel, n_sel=n_sel, ncp=ncp),
        grid=(b, n_qt),
        in_specs=[
            q_rows(N_HEADS * LANES), q_rows(LANES),
            per_b((ncp, half)), per_b((vr, ncp)),
            pl.BlockSpec((t, half), lambda i, j: (i, 0)), per_b((vr, t)),
            per_b((tp, half)), per_b((vr, tp)),
            _resident(emat.shape), _resident(ove.shape), _resident(wt.shape), _resident(ct.shape),
        ],
        out_specs=q_rows(ATTN_DIM),
        out_shape=jax.ShapeDtypeStruct((b * t, ATTN_DIM), f32),
        scratch_shapes=[pltpu.VMEM((N_KV, ncp, cols), f32)],
        compiler_params=_cparams(("parallel", "arbitrary")),
        name="attn_prompt",
    )(q_pk, gates, kcp, vce, ks_bf, vse, kw_pad, vwe, emat, ove, wt, ct)


def _softmax_rows2(s_main, s_new):
    m = jnp.maximum(jnp.max(s_main, axis=1, keepdims=True), jnp.max(s_new, axis=1, keepdims=True))
    e_main, e_new = jnp.exp(s_main - m), jnp.exp(s_new - m)
    inv = 1.0 / (jnp.sum(e_main, axis=1, keepdims=True) + jnp.sum(e_new, axis=1, keepdims=True))
    return (e_main * inv).astype(bf16), (e_new * inv).astype(bf16)


def _attn_sample_kernel(pt_ref, pages_hbm, q_ref, g_ref, kcp_ref, knew_ref, win_ref, wnew_ref, ovt_ref, rmat_ref,
                        emat_ref, scmp_ref, sk_ref, snew_ref, swin_ref, c31_ref, out_ref, buf_ref, ke_ref, vt_ref,
                        sem_ref, *, n_pages, n_sel, past_len):
    half = KV_W // 2
    slot = _gather_pages(pt_ref, pages_hbm, buf_ref, sem_ref, n_pages)
    q = q_ref[0]
    q_rows = q[0:SAMPLE_COLS]

    kcp = kcp_ref[0]
    pc = _masked_softmax_cols(_dot_nt(kcp[:, :half], q) + scmp_ref[...])
    o_c = _dot_tn(pc.astype(bf16), kcp[:, half:])[0:SAMPLE_COLS]
    p_hi, p_lo = _split_bf16(pc)
    imp = _dot(ovt_ref[...], p_hi) + _dot(ovt_ref[...], p_lo)
    i_hi, i_lo = _split_bf16(imp)
    imp = _dot(i_hi, rmat_ref[...]) + _dot(i_lo, rmat_ref[...])
    rows = imp.shape[0]
    blk = lax.broadcasted_iota(i32, (rows, LANES), 0)
    qpos = past_len + (lax.broadcasted_iota(i32, (rows, LANES), 1) & 3)
    cur = lax.shift_right_logical(qpos, SEL_SHIFT)
    forced = (blk == 0) | (blk == cur) | (blk == cur - 1)
    future = blk * SEL_BLOCK > qpos
    imp = jnp.where(forced, FORCE, jnp.where(future, -FORCE, imp))
    imp = jnp.where(blk < n_sel, imp, -jnp.inf)
    sel = _top_rows(imp, min(SEL_TOP, n_sel))
    sb = jnp.where(sel, 0.0, NEG)
    past_blocks = past_len // SEL_BLOCK
    sb_rows = sb[0:past_blocks].T[0:SAMPLE_COLS]

    for k in range(n_pages):
        lanes = slice(k * PAGE_SIZE, (k + 1) * PAGE_SIZE)
        ke_ref[0:half, lanes] = buf_ref[slot, k, 0:half, :].astype(bf16)
        vt_ref[:, lanes] = buf_ref[slot, k, half:, :].astype(bf16)
    ke_ref[half:, :] = emat_ref[...]
    n_far = past_len - PAGE_SIZE
    s = _dot(jnp.concatenate([q_rows, sb_rows.astype(bf16)], axis=1), ke_ref[...])
    s = jnp.concatenate([s[:, :n_far] + c31_ref[:, 0:1], s[:, n_far:] + sk_ref[...]], axis=1)
    knew = knew_ref[0].astype(bf16)
    s_n = _dot(q_rows, knew[0:half]) + snew_ref[...]
    p, p_n = _softmax_rows2(s, s_n)
    o_s = _dot_nt(p, vt_ref[...]) + _dot_nt(p_n, knew[half:])

    win = win_ref[0].astype(bf16)
    wnew = wnew_ref[0].astype(bf16)
    p, p_n = _softmax_rows2(_dot(q_rows, win[0:half]) + swin_ref[...], _dot(q_rows, wnew[0:half]) + snew_ref[...])
    o_w = _dot_nt(p, win[half:]) + _dot_nt(p_n, wnew[half:])

    gt = g_ref[0]
    out_ref[0] = gt[:, 0:1] * o_c + gt[:, 1:2] * o_s + gt[:, 2:3] * o_w


def _attn_sample(q_pk, gates, kcp, pages_t, page_table, ks_new, win_t, kw_new, tabs):
    scmp, sk, snew, swin, c31s = tabs
    bd, t_new, _ = q_pk.shape
    n_pages = page_table.shape[1]
    past_len = n_pages * PAGE_SIZE
    ncp = kcp.shape[1]
    n_cmp = past_len // CMP_STRIDE - 1
    n_sel = -(-(past_len + t_new) // SEL_BLOCK)
    n_sel_rows = -(-n_sel // SUBLANES) * SUBLANES
    past_blocks = past_len // SEL_BLOCK
    n_cols = N_KV * GROUP * t_new
    assert n_cols == SAMPLE_COLS and win_t.shape[2] == WINDOW and past_blocks == LANES
    qc = q_pk.reshape(bd, t_new, N_HEADS, LANES).transpose(0, 2, 1, 3).reshape(bd, n_cols, LANES)
    qc = jnp.pad(qc, ((0, 0), (0, LANES - n_cols), (0, 0)))
    gc = gates[:, :, :3 * N_HEADS].reshape(bd, t_new, N_HEADS, 3).transpose(0, 2, 1, 3).reshape(bd, n_cols, 3)
    gc = jnp.pad(gc, ((0, 0), (0, 0), (0, SUBLANES - 3)))
    new_t = lambda a: jnp.pad(jnp.swapaxes(a, 1, 2), ((0, 0), (0, 0), (0, LANES - t_new)))
    ovt = _overlap_t(n_sel_rows, n_sel, ncp, n_cmp)
    c = np.arange(LANES)
    same = (c[:, None] // (GROUP * t_new) == c[None, :] // (GROUP * t_new)) & (c[:, None] % t_new == c[None, :] % t_new)
    rmat = jnp.asarray(same & (c[:, None] < n_cols) & (c[None, :] < n_cols), dtype=bf16)
    emat = jnp.asarray(np.arange(past_blocks)[:, None] == np.arange(past_len)[None, :] // SEL_BLOCK, dtype=bf16)

    per_b = lambda shape: pl.BlockSpec((1,) + shape, lambda i, pt: (i, 0, 0))
    res = lambda a: pl.BlockSpec(a.shape, lambda i, pt: (0,) * a.ndim)
    half = KV_W // 2
    grid_spec = pltpu.PrefetchScalarGridSpec(
        num_scalar_prefetch=1,
        grid=(bd,),
        in_specs=[pl.BlockSpec(memory_space=pl.ANY),
                  per_b((LANES, LANES)), per_b((n_cols, SUBLANES)), per_b((ncp, KV_W)), per_b((KV_W, LANES)),
                  per_b((KV_W, WINDOW)), per_b((KV_W, LANES)),
                  res(ovt), res(rmat), res(emat), res(scmp), res(sk), res(snew), res(swin), res(c31s)],
        out_specs=per_b((n_cols, LANES)),
        scratch_shapes=[pltpu.VMEM((2, n_pages, KV_W, PAGE_SIZE), f32),
                        pltpu.VMEM((half + past_blocks, past_len), bf16), pltpu.VMEM((half, past_len), bf16),
                        pltpu.SemaphoreType.DMA((2,))],
    )
    o = pl.pallas_call(
        functools.partial(_attn_sample_kernel, n_pages=n_pages, n_sel=n_sel, past_len=past_len),
        grid_spec=grid_spec,
        out_shape=jax.ShapeDtypeStruct((bd, n_cols, LANES), f32),
        compiler_params=_cparams(("arbitrary",)),
        name="attn_sample",
    )(page_table, pages_t, qc, gc, kcp, new_t(ks_new), win_t, new_t(kw_new),
      ovt, rmat, emat, scmp, sk, snew, swin, c31s)
    o = o.reshape(bd, N_KV, GROUP, t_new, N_KV, HEAD_DIM)
    o = jnp.stack([o[:, g, :, :, g, :] for g in range(N_KV)], axis=1)
    return o.transpose(0, 3, 1, 2, 4).reshape(bd, t_new, ATTN_DIM)


def _merge_kernel(xp_ref, xs_ref, ap_ref, as_ref, u0_ref, halo_ref, u1s_ref, u2s_ref, cb_ref, mg_ref, cw_ref, cbias_ref,
                  wau_ref, wcu_ref, wo_ref, g2_ref, x2_ref, h2_ref, *, d_model, prompt_tiles, seq_tiles):
    i = pl.program_id(0)
    u0 = u0_ref[...]
    r = lax.broadcasted_iota(i32, u0.shape, 0)
    keep = jnp.where(i % seq_tiles == 0, 0.0, 1.0)
    p1 = halo_ref[SUBLANES - 1:SUBLANES, :] * keep
    p2 = halo_ref[SUBLANES - 2:SUBLANES - 1, :] * keep
    u1 = jnp.where(r == 0, p1, pltpu.roll(u0, 1, axis=0))
    u2 = jnp.where(r == 0, p2, jnp.where(r == 1, p1, pltpu.roll(u0, 2, axis=0)))
    decode = i >= prompt_tiles
    u1 = jnp.where(decode, u1s_ref[...], u1)
    u2 = jnp.where(decode, u2s_ref[...], u2)
    y = cbias_ref[...] + cw_ref[0:1, :] * u2
    y = y + cw_ref[1:2, :] * u1
    y = y + cw_ref[2:3, :] * u0
    up_a = _dot(_two_sources(ap_ref, as_ref, prompt_tiles).astype(bf16), wau_ref[...])
    up_c = _dot((cb_ref[...] * y).astype(bf16), wcu_ref[...])
    mixed = mg_ref[:, :d_model] * up_a + mg_ref[:, d_model:] * up_c
    x2 = _two_sources(xp_ref, xs_ref, prompt_tiles) + _dot(mixed.astype(bf16), wo_ref[...])
    x2_ref[...] = x2
    h2_ref[...] = _rmsnorm(x2, g2_ref[...]).astype(bf16)


def _merge(x_p, x_s, attn_p, attn_s, cin, u1s, u2s, cb, mg, conv_w, conv_b, w_attn_up, w_conv_up, w_out, norm2_g, t):
    n_prompt, d = x_p.shape
    n = n_prompt + x_s.shape[0]
    tm = ROW_TILE
    assert CONV_W == 3 and t % tm == 0 and n_prompt % tm == 0 and (n - n_prompt) % tm == 0
    prompt_tiles = n_prompt // tm
    row = lambda w: pl.BlockSpec((tm, w), lambda i: (i, 0))
    halo = pl.BlockSpec((SUBLANES, CONV_DIM), lambda i: (jnp.maximum(i * (tm // SUBLANES) - 1, 0), 0))
    dec = pl.BlockSpec((tm, CONV_DIM), lambda i: (jnp.maximum(i - prompt_tiles, 0), 0))
    cw = jnp.pad(conv_w, ((0, SUBLANES - CONV_W), (0, 0)))
    return pl.pallas_call(
        functools.partial(_merge_kernel, d_model=d, prompt_tiles=prompt_tiles, seq_tiles=t // tm),
        grid=(n // tm,),
        in_specs=_two_specs(tm, d, prompt_tiles) + _two_specs(tm, ATTN_DIM, prompt_tiles)
        + [row(CONV_DIM), halo, dec, dec, row(CONV_DIM), row(2 * d),
           _resident(cw.shape), _resident((1, CONV_DIM)), _resident(w_attn_up.shape),
           _resident(w_conv_up.shape), _resident(w_out.shape), _resident((1, d))],
        out_specs=[row(d), row(d)],
        out_shape=[jax.ShapeDtypeStruct((n, d), f32), jax.ShapeDtypeStruct((n, d), bf16)],
        compiler_params=_cparams(("parallel",)),
        name="merge",
    )(x_p, x_s, attn_p, attn_s, cin, cin, u1s, u2s, cb, mg, cw, conv_b.reshape(1, CONV_DIM), w_attn_up.astype(bf16),
      w_conv_up.astype(bf16), w_out.astype(bf16), norm2_g.reshape(1, d))


def _top_rows_sorted(s, k):
    rows = lax.broadcasted_iota(i32, s.shape, 0).astype(f32)
    vals, idxs = [], []
    for _ in range(k):
        m = jnp.max(s, axis=0, keepdims=True)
        first = jnp.min(jnp.where(s == m, rows, 1e9), axis=0, keepdims=True)
        vals.append(m)
        idxs.append(first)
        s = jnp.where(rows == first, -jnp.inf, s)
    return jnp.concatenate(vals, axis=0), jnp.concatenate(idxs, axis=0)


def _peer_route_kernel(h_ref, wq_ref, k1_ref, k2_ref, a_ref, b_ref, g_ref):
    qp = _dot(h_ref[...], wq_ref[...]).astype(bf16)
    key_dim = 2 * PEER_HALF
    a_rows, b_rows, g_rows = [], [], []
    for h in range(PEER_HEADS):
        q1 = qp[:, h * key_dim:h * key_dim + PEER_HALF]
        q2 = qp[:, h * key_dim + PEER_HALF:(h + 1) * key_dim]
        v1, i1 = _top_rows_sorted(_dot_nt(k1_ref[...], q1), PEER_TOPK)
        v2, i2 = _top_rows_sorted(_dot_nt(k2_ref[...], q2), PEER_TOPK)
        counts = [PEER_TOPK // (r + 1) for r in range(PEER_TOPK)]
        cand = jnp.concatenate([v1[r:r + 1, :] + v2[0:n, :] for r, n in enumerate(counts)], axis=0)
        code = jnp.concatenate([i1[r:r + 1, :] * PEER_NKEYS + i2[0:n, :] for r, n in enumerate(counts)], axis=0)
        pad = -sum(counts) % SUBLANES
        cand = jnp.concatenate([cand, jnp.full((pad, cand.shape[1]), -jnp.inf, f32)], axis=0)
        code = jnp.concatenate([code, jnp.zeros((pad, code.shape[1]), f32)], axis=0)
        rows = lax.broadcasted_iota(i32, cand.shape, 0).astype(f32)
        sc, ex = [], []
        for _ in range(PEER_TOPK):
            m = jnp.max(cand, axis=0, keepdims=True)
            first = jnp.min(jnp.where(cand == m, rows, 1e9), axis=0, keepdims=True)
            pick = rows == first
            sc.append(m)
            ex.append(jnp.max(jnp.where(pick, code, -1.0), axis=0, keepdims=True))
            cand = jnp.where(pick, -jnp.inf, cand)
        sc = jnp.concatenate(sc, axis=0)
        ex = jnp.concatenate(ex, axis=0)
        e1 = jnp.floor(ex * (1.0 / PEER_NKEYS))
        e = jnp.exp(sc - sc[0:1, :])
        a_rows.append(e1)
        b_rows.append(ex - e1 * PEER_NKEYS)
        g_rows.append(e / jnp.sum(e, axis=0, keepdims=True))
    tn = h_ref.shape[0]
    for src, dst in ((a_rows, a_ref), (b_rows, b_ref), (g_rows, g_ref)):
        full = jnp.concatenate(src, axis=0)
        for c in range(tn // LANES):
            dst[c * LANES:(c + 1) * LANES, :] = full[:, c * LANES:(c + 1) * LANES].T


def _peer_route(h2, wq, k1, k2):
    n, d = h2.shape
    tn = ROW_TILE
    assert PEER_HEADS * PEER_TOPK == LANES
    row = lambda w: pl.BlockSpec((tn, w), lambda i: (i, 0))
    return pl.pallas_call(
        _peer_route_kernel,
        grid=(n // tn,),
        in_specs=[row(d), _resident(wq.shape), _resident(k1.shape), _resident(k2.shape)],
        out_specs=[row(LANES)] * 3,
        out_shape=[jax.ShapeDtypeStruct((n, LANES), f32)] * 3,
        compiler_params=_cparams(("parallel",)),
        name="peer_route",
    )(h2, wq.astype(bf16), k1.astype(bf16), k2.astype(bf16))


def _peer_expert_kernel(h_ref, a_ref, b_ref, w_ref, x_ref, u_ref, v_ref, out_ref, g_ref, acc_ref, *, pitch):
    e_step = pl.program_id(1)
    tn = h_ref.shape[0]
    pairs_per_chunk = EXPERT_CHUNK // PEER_NKEYS // 2
    half_keys = PEER_NKEYS // 2

    @pl.when(e_step == 0)
    def _build_gate_matrix():
        key_row = lax.broadcasted_iota(i32, (PEER_NKEYS, LANES), 0).astype(f32)

        def token(n, c):
            a = jnp.broadcast_to(a_ref[pl.ds(n, 1), :], (PEER_NKEYS, LANES))
            b = jnp.broadcast_to(b_ref[pl.ds(n, 1), :], (PEER_NKEYS, LANES))
            w = jnp.broadcast_to(w_ref[pl.ds(n, 1), :], (PEER_NKEYS, LANES))
            pa = (a == key_row).astype(bf16)
            wb = jnp.where(b == key_row, w, 0.0).astype(bf16)
            g = _dot_nt(pa, wb).astype(bf16)
            g_ref[pl.ds(n, half_keys, stride=pitch), :] = pltpu.bitcast(g, jnp.uint32)
            return c

        lax.fori_loop(0, tn, token, 0, unroll=TOKEN_UNROLL)
        acc_ref[...] = jnp.zeros_like(acc_ref)

    s = _dot_nt(h_ref[...], u_ref[...])
    gates = []
    for j in range(pairs_per_chunk):
        words = g_ref[pl.ds(pl.multiple_of((e_step * pairs_per_chunk + j) * pitch, SUBLANES), tn), :]
        gates.append(pltpu.bitcast(words, bf16).reshape(tn, 2 * LANES))
    gates = jnp.concatenate(gates, axis=1).astype(f32)
    act = 0.5 * s * (1.0 + lax.erf(s * math.sqrt(0.5)))
    acc_ref[...] += _dot((gates * act).astype(bf16), v_ref[...])

    @pl.when(e_step == pl.num_programs(1) - 1)
    def _finish():
        out_ref[...] = x_ref[...] + acc_ref[...]


def _peer_expert(h2, a_idx, b_idx, gate_w, x2, u_tab, v_tab):
    n, d = h2.shape
    n_exp = u_tab.shape[0]
    tn = PEER_TILE
    pitch = tn + G_PITCH_PAD
    assert n % tn == 0 and n_exp == PEER_NKEYS * PEER_NKEYS and EXPERT_CHUNK % (2 * PEER_NKEYS) == 0
    row = lambda w: pl.BlockSpec((tn, w), lambda i, e: (i, 0))
    tab = pl.BlockSpec((EXPERT_CHUNK, d), lambda i, e: (e, 0))
    return pl.pallas_call(
        functools.partial(_peer_expert_kernel, pitch=pitch),
        grid=(n // tn, n_exp // EXPERT_CHUNK),
        in_specs=[row(d), row(LANES), row(LANES), row(LANES), row(d), tab, tab],
        out_specs=row(d),
        out_shape=jax.ShapeDtypeStruct((n, d), f32),
        scratch_shapes=[pltpu.VMEM((PEER_NKEYS // 2 * pitch, LANES), jnp.uint32), pltpu.VMEM((tn, d), f32)],
        compiler_params=_cparams(("parallel", "arbitrary")),
        name="peer_expert",
    )(h2, a_idx, b_idx, gate_w, x2, u_tab.astype(bf16), v_tab.astype(bf16))


def _ple_kernel(x_ref, p_ref, pg_ref, wg_ref, wp_ref, fg_ref, y_ref):
    x = x_ref[...]
    gate = jax.nn.sigmoid(_dot(_rmsnorm(x, pg_ref[...]).astype(bf16), wg_ref[...]))
    x = x + gate * _dot(p_ref[...].astype(bf16), wp_ref[...])
    y_ref[...] = _rmsnorm(x, fg_ref[...])


def _ple(x3, row0, p, ple_g, w_ple_gate, w_ple_proj, final_g):
    n, d = p.shape[0], x3.shape[1]
    tm = ROW_TILE
    assert n % tm == 0 and row0 % tm == 0
    row = lambda w: pl.BlockSpec((tm, w), lambda i: (i, 0))
    return pl.pallas_call(
        _ple_kernel,
        grid=(n // tm,),
        in_specs=[pl.BlockSpec((tm, d), lambda i: (i + row0 // tm, 0)), row(p.shape[1]), _resident((1, d)),
                  _resident(w_ple_gate.shape), _resident(w_ple_proj.shape), _resident((1, d))],
        out_specs=row(d),
        out_shape=jax.ShapeDtypeStruct((n, d), f32),
        compiler_params=_cparams(("parallel",)),
        name="ple_final_rows_%d" % row0,
    )(x3, p, ple_g.reshape(1, d), w_ple_gate.astype(bf16), w_ple_proj.astype(bf16), final_g.reshape(1, d))


def kernel(x_prompt, x_sample, p_prompt, p_sample, cache_cmp_kv, cache_slc_kv, page_table, state_win_kv, state_conv, norm1_g, w_in, w_cmp, cmp_pos, conv_w, conv_b, w_attn_up, w_conv_up, w_out, norm2_g, peer_wq, peer_k1, peer_k2, peer_u, peer_v, ple_g, w_ple_gate, w_ple_proj, rel_bias, final_g):
    depth = norm1_g.shape[0]
    assert depth == 1, "single-layer trunk"
    b, t, d = x_prompt.shape
    bd, t_new, _ = x_sample.shape
    n_pages = page_table.shape[1]
    past_len = n_pages * PAGE_SIZE
    n_p, n_s = b * t, bd * t_new
    kv_shape = lambda lead: lead + (2, N_KV, HEAD_DIM)

    x_p, x_s = x_prompt.reshape(n_p, d), x_sample.reshape(n_s, d)
    w_pack = _pack_w_in(w_in[0], d)
    q_pk, kvc, kvs, kvw, kvs_bf, kvw_bf, gates, cin, cbg, mg = _in_proj(x_p, x_s, norm1_g[0], w_pack)
    prompt = lambda a: a[:n_p].reshape(b, t, -1)
    sample = lambda a: a[n_p:].reshape(bd, t_new, -1)

    w4, pos_rows = _pack_w_cmp(w_cmp[0], cmp_pos[0])
    kcp_p = _compress_prompt(kvc, w4, pos_rows, b, t)
    kcp_s = _compress_sample(_pages_t(cache_cmp_kv[0]), page_table, w4, pos_rows)
    n_cmp_s = past_len // CMP_STRIDE - 1
    wt, ct, *sample_tabs = _tables(rel_bias, past_len, kcp_s.shape[1], n_cmp_s)

    kvs_s, kvw_s = sample(kvs), sample(kvw)
    attn_p = _attn_prompt(q_pk, gates, kcp_p, kvs_bf, kvw_bf, wt, ct, b, t)
    win_t = jnp.transpose(state_win_kv[0], (0, 2, 3, 4, 1)).reshape(bd, KV_W, -1)
    attn_s = _attn_sample(sample(q_pk), sample(gates), kcp_s, _pages_t(cache_slc_kv[0]), page_table, kvs_s, win_t,
                          kvw_s, sample_tabs)

    cin_s = sample(cin)
    buf_s = state_conv[0]
    back = lambda k: jnp.concatenate([buf_s[:, CONV_W - 1 - k:], cin_s[:, :t_new - k]], axis=1).reshape(n_s, CONV_DIM)
    x2, h2 = _merge(x_p, x_s, attn_p, attn_s.reshape(n_s, ATTN_DIM), cin, back(1), back(2), cbg, mg, conv_w[0],
                    conv_b[0], w_attn_up[0], w_conv_up[0], w_out[0], norm2_g[0], t)

    a_idx, b_idx, gate_w = _peer_route(h2, peer_wq[0], peer_k1[0], peer_k2[0])
    x3 = _peer_expert(h2, a_idx, b_idx, gate_w, x2, peer_u[0], peer_v[0])

    ple_w = (ple_g[0], w_ple_gate[0], w_ple_proj[0], final_g)
    y_p = _ple(x3, 0, p_prompt[0].reshape(n_p, -1), *ple_w)
    y_s = _ple(x3, n_p, p_sample[0].reshape(n_s, -1), *ple_w)

    n_win = min(WINDOW, t)
    assert t >= CONV_W - 1 and t_new >= CONV_W - 1
    new_win_s = jnp.concatenate([state_win_kv[:, :, t_new:], kvw_s.reshape(kv_shape((1, bd, t_new)))], axis=2)
    return (
        y_p.reshape(b, t, d),
        y_s.reshape(bd, t_new, d),
        prompt(kvc).reshape(kv_shape((1, b, t))),
        sample(kvc).reshape(kv_shape((1, bd, t_new))),
        prompt(kvs).reshape(kv_shape((1, b, t))),
        kvs_s.reshape(kv_shape((1, bd, t_new))),
        prompt(kvw)[:, t - n_win:].reshape(kv_shape((1, b, n_win))),
        new_win_s,
        prompt(cin)[None, :, t - (CONV_W - 1):],
        cin_s[None, :, t_new - (CONV_W - 1):],
    )
```

```python
import functools
import math

import jax
import jax.numpy as jnp
import numpy as np
from jax import lax
from jax.experimental import pallas as pl
from jax.experimental.pallas import tpu as pltpu

f32 = jnp.float32
bf16 = jnp.bfloat16
i32 = jnp.int32

N_HEADS = 8
N_KV = 2
GROUP = N_HEADS // N_KV
HEAD_DIM = 64
ATTN_DIM = N_HEADS * HEAD_DIM
KV_W = 2 * N_KV * HEAD_DIM
CMP_BLOCK = 32
CMP_STRIDE = 16
SEL_BLOCK = 64
SEL_SHIFT = 6
SEL_TOP = 16
WINDOW = 512
N_BUCKETS = 32
MAX_EXACT = N_BUCKETS // 2
MAX_DISTANCE = 128
CONV_DIM = 512
CONV_W = 3
PEER_HEADS = 8
PEER_NKEYS = 128
PEER_HALF = 128
PEER_TOPK = 16
PAGE_SIZE = 128
EPS = 1e-6
NEG = -1e30
FORCE = 1e4

LANES = 128
SUBLANES = 8
VMEM_LIMIT_BYTES = 56 * 1024 * 1024

Q_TILE = 128
SAMPLE_COLS = 32
FAR_CHUNK = 512
CMP_PAD = 16
CMP_WIN = 24
ONES_ROWS = 16
ROW_TILE = 256
PEER_TILE = 512
EXPERT_CHUNK = 2048
TOKEN_UNROLL = 32
G_PITCH_PAD = 8


def _cparams(sem):
    return pltpu.CompilerParams(dimension_semantics=sem, vmem_limit_bytes=VMEM_LIMIT_BYTES)


def _dot(a, b):
    return jnp.dot(a, b, preferred_element_type=f32)


def _dot_nt(a, b):
    return lax.dot_general(a, b, (((1,), (1,)), ((), ())), preferred_element_type=f32)


def _dot_tn(a, b):
    return lax.dot_general(a, b, (((0,), (0,)), ((), ())), preferred_element_type=f32)


def _split_bf16(x):
    hi = x.astype(bf16)
    lo = (x - hi.astype(f32)).astype(bf16)
    return hi, lo


def _resident(shape):
    nd = len(shape)
    return pl.BlockSpec(shape, lambda *_: (0,) * nd)


def _rmsnorm(x, g):
    ms = jnp.mean(x * x, axis=-1, keepdims=True)
    return x * lax.rsqrt(ms + EPS) * g


def _two_sources(first_ref, second_ref, first_tiles):
    return jnp.where(pl.program_id(0) < first_tiles, first_ref[...], second_ref[...])


def _two_specs(tile, width, first_tiles):
    return [pl.BlockSpec((tile, width), lambda i: (jnp.minimum(i, first_tiles - 1), 0)),
            pl.BlockSpec((tile, width), lambda i: (jnp.maximum(i - first_tiles, 0), 0))]


def _in_proj_kernel(xp_ref, xs_ref, g_ref, w_ref, q_ref, kc_ref, ks_ref, kw_ref, ksb_ref, kwb_ref,
                    gn_ref, cin_ref, cb_ref, mg_ref, *, d_model, prompt_tiles):
    qw = N_HEADS * LANES
    h = _rmsnorm(_two_sources(xp_ref, xs_ref, prompt_tiles), g_ref[...]).astype(bf16)
    o = 0
    q_ref[...] = (_dot(h, w_ref[:, o:o + qw]) * (HEAD_DIM ** -0.5)).astype(bf16)
    o += qw
    kc_ref[...] = _dot(h, w_ref[:, o:o + KV_W])
    o += KV_W
    ks = _dot(h, w_ref[:, o:o + KV_W])
    ks_ref[...] = ks
    ksb_ref[...] = ks.astype(bf16)
    o += KV_W
    kw = _dot(h, w_ref[:, o:o + KV_W])
    kw_ref[...] = kw
    kwb_ref[...] = kw.astype(bf16)
    o += KV_W
    gn_ref[...] = jax.nn.sigmoid(_dot(h, w_ref[:, o:o + LANES]))
    o += LANES
    cv = _dot(h, w_ref[:, o:o + CONV_DIM])
    o += CONV_DIM
    cb_ref[...] = _dot(h, w_ref[:, o:o + CONV_DIM])
    o += CONV_DIM
    cin_ref[...] = _dot(h, w_ref[:, o:o + CONV_DIM]) * cv
    o += CONV_DIM
    mg_ref[...] = jax.nn.sigmoid(_dot(h, w_ref[:, o:o + 2 * d_model]))


def _pack_w_in(w_in, d_model):
    sizes = (ATTN_DIM, KV_W, KV_W, KV_W, 3 * N_HEADS, CONV_DIM, CONV_DIM, CONV_DIM, 2 * d_model)
    parts, s = [], 0
    for n in sizes:
        parts.append(w_in[:, s:s + n])
        s += n
    wq = parts[0].reshape(d_model, N_KV, GROUP, HEAD_DIM)
    z = jnp.zeros_like(wq)
    slabs = [jnp.concatenate([wq[:, 0], z[:, 0]], axis=-1), jnp.concatenate([z[:, 1], wq[:, 1]], axis=-1)]
    wq = jnp.stack(slabs, axis=1).reshape(d_model, N_HEADS * LANES)
    gn = jnp.pad(parts[4], ((0, 0), (0, LANES - 3 * N_HEADS)))
    return jnp.concatenate([wq, parts[1], parts[2], parts[3], gn] + parts[5:], axis=1).astype(bf16)


def _in_proj(x_prompt, x_sample, norm_g, w_pack):
    d = x_prompt.shape[1]
    n = x_prompt.shape[0] + x_sample.shape[0]
    tm = ROW_TILE
    assert x_prompt.shape[0] % tm == 0 and x_sample.shape[0] % tm == 0
    prompt_tiles = x_prompt.shape[0] // tm
    row = lambda w: pl.BlockSpec((tm, w), lambda i: (i, 0))
    widths = (N_HEADS * LANES, KV_W, KV_W, KV_W, KV_W, KV_W, LANES, CONV_DIM, CONV_DIM, 2 * d)
    dtypes = (bf16, f32, f32, f32, bf16, bf16, f32, f32, f32, f32)
    return pl.pallas_call(
        functools.partial(_in_proj_kernel, d_model=d, prompt_tiles=prompt_tiles),
        grid=(n // tm,),
        in_specs=_two_specs(tm, d, prompt_tiles) + [_resident((1, d)), _resident(w_pack.shape)],
        out_specs=[row(w) for w in widths],
        out_shape=[jax.ShapeDtypeStruct((n, w), t) for w, t in zip(widths, dtypes)],
        compiler_params=_cparams(("parallel",)),
        name="in_proj",
    )(x_prompt, x_sample, norm_g.reshape(1, d), w_pack)


def _bias_of_dist(dist, valid, rb_ref, head, shift=0.0):
    n = jnp.maximum(dist, 0)
    nf = jnp.maximum(n, 1).astype(f32)
    large = MAX_EXACT + jnp.floor(jnp.log(nf / MAX_EXACT) / math.log(MAX_DISTANCE / MAX_EXACT)
                                  * (N_BUCKETS - MAX_EXACT)).astype(i32)
    large = jnp.minimum(large, N_BUCKETS - 1)
    bucket = jnp.where(n < MAX_EXACT, n, large)
    out = jnp.zeros(dist.shape, f32)
    for b in range(N_BUCKETS):
        out = jnp.where(bucket == b, rb_ref[b, head], out)
    return jnp.where(valid, out - shift, NEG)


def _tables_kernel(rb_ref, wt_ref, ct_ref, scmp_ref, sk_ref, snew_ref, swin_ref, c31s_ref,
                   *, past_len, ncp_s, n_cmp_s):
    ik = lax.broadcasted_iota(i32, (Q_TILE, Q_TILE), 0)
    iq = lax.broadcasted_iota(i32, (Q_TILE, Q_TILE), 1)
    n_w = WINDOW // Q_TILE + 1
    for g in range(N_KV):
        wt_ref[g, 0] = jnp.full(wt_ref.shape[2:], NEG, f32)
        for hh in range(GROUP):
            head = g * GROUP + hh
            far = rb_ref[N_BUCKETS - 1, head]
            cols = slice(hh * Q_TILE, (hh + 1) * Q_TILE)
            for w in range(n_w):
                dist = Q_TILE * (n_w - 1 - w) + iq - ik
                wt_ref[g, w + 1, :, cols] = _bias_of_dist(dist, (dist >= 0) & (dist < WINDOW), rb_ref, head, far)
            mm = lax.broadcasted_iota(i32, (CMP_WIN, Q_TILE), 0)
            jq = lax.broadcasted_iota(i32, (CMP_WIN, Q_TILE), 1)
            dist = jq - CMP_STRIDE * (mm - CMP_PAD) - (CMP_BLOCK - 1)
            ct_ref[g, :, cols] = _bias_of_dist(dist, dist >= 0, rb_ref, head, far)
    def sample_table(shape, q_axis, dist_fn, valid_fn, out_ref):
        c = lax.broadcasted_iota(i32, shape, q_axis)
        r = lax.broadcasted_iota(i32, shape, 1 - q_axis)
        t = c & 3
        hd = jnp.minimum(lax.shift_right_logical(c, 2), N_HEADS - 1)
        dist = dist_fn(r, t)
        valid = valid_fn(r, t, dist)
        acc = jnp.zeros(shape, f32)
        for head in range(N_HEADS):
            acc = jnp.where(hd == head, _bias_of_dist(dist, valid, rb_ref, head), acc)
        out_ref[...] = acc

    sample_table((ncp_s, LANES), 1, lambda r, t: past_len + t - CMP_STRIDE * (r - CMP_PAD) - (CMP_BLOCK - 1),
                 lambda r, t, d: (r >= CMP_PAD) & (r < CMP_PAD + n_cmp_s) & (d >= 0), scmp_ref)
    sample_table((SAMPLE_COLS, PAGE_SIZE), 0, lambda r, t: PAGE_SIZE + t - r, lambda r, t, d: d >= 0, sk_ref)
    sample_table((SAMPLE_COLS, LANES), 0, lambda r, t: t - r, lambda r, t, d: (d >= 0) & (r < 4), snew_ref)
    sample_table((SAMPLE_COLS, WINDOW), 0, lambda r, t: WINDOW + t - r, lambda r, t, d: d < WINDOW, swin_ref)
    sample_table((SAMPLE_COLS, LANES), 0, lambda r, t: jnp.full_like(r, MAX_DISTANCE), lambda r, t, d: d > 0, c31s_ref)


def _tables(rel_bias, past_len, ncp_s, n_cmp_s):
    n_w = WINDOW // Q_TILE + 1
    cols = GROUP * Q_TILE
    shapes = [
        (N_KV, n_w + 1, Q_TILE, cols), (N_KV, CMP_WIN, cols),
        (ncp_s, LANES), (SAMPLE_COLS, PAGE_SIZE), (SAMPLE_COLS, LANES), (SAMPLE_COLS, WINDOW), (SAMPLE_COLS, LANES),
    ]
    return pl.pallas_call(
        functools.partial(_tables_kernel, past_len=past_len, ncp_s=ncp_s, n_cmp_s=n_cmp_s),
        in_specs=[pl.BlockSpec(memory_space=pltpu.SMEM)],
        out_shape=[jax.ShapeDtypeStruct(s, f32) for s in shapes],
        compiler_params=pltpu.CompilerParams(vmem_limit_bytes=VMEM_LIMIT_BYTES),
        name="bias_tables",
    )(rel_bias)


def _ncp(n_chunks):
    return -(-(n_chunks + CMP_PAD + SUBLANES) // LANES) * LANES


def _compress_core(chunk_rows, w_ref, pos_ref, out_ref, n_chunks):
    half = KV_W // 2
    out_ref[...] = jnp.zeros(out_ref.shape, out_ref.dtype)
    for c in range(2):
        fs = None
        pb = None
        for p in range(CMP_STRIDE // 2):
            x = jnp.concatenate([chunk_rows(2 * p, c), chunk_rows(2 * p + 1, c)], axis=1).astype(bf16)
            lanes = [slice((2 * p + k) * KV_W + c * half, (2 * p + k) * KV_W + (c + 1) * half) for k in range(2)]
            px = jnp.concatenate([pos_ref[:, lanes[0]], pos_ref[:, lanes[1]]], axis=1)
            fs = _dot(x, w_ref[p, c]) if fs is None else fs + _dot(x, w_ref[p, c])
            pb = _dot(px, w_ref[p, c]) if pb is None else pb + _dot(px, w_ref[p, c])
        bias = pb[0:1, :half] + pb[1:2, half:]
        nxt = pltpu.roll(fs[:, half:], n_chunks - 1, axis=0)
        kc = fs[:, :half] + nxt + bias
        r = lax.broadcasted_iota(i32, kc.shape, 0)
        kc = jnp.where(r < n_chunks - 1, kc, 0.0)
        out_ref[0, CMP_PAD:CMP_PAD + n_chunks, c * half:(c + 1) * half] = kc.astype(out_ref.dtype)


def _compress_prompt_kernel(x_ref, w_ref, pos_ref, out_ref, *, n_chunks):
    half = KV_W // 2
    rows = lambda l, c: x_ref[:, l * KV_W + c * half:l * KV_W + (c + 1) * half]
    _compress_core(rows, w_ref, pos_ref, out_ref, n_chunks)


def _gather_pages(pt_ref, pages_hbm, buf_ref, sem_ref, n_pages):
    b = pl.program_id(0)
    slot = b % 2

    def copies(seq, into):
        return [pltpu.make_async_copy(pages_hbm.at[pt_ref[seq, k]], buf_ref.at[into, k], sem_ref.at[into])
                for k in range(n_pages)]

    @pl.when(b == 0)
    def _first():
        for c in copies(0, 0):
            c.start()

    @pl.when(b + 1 < pl.num_programs(0))
    def _next():
        for c in copies(b + 1, 1 - slot):
            c.start()

    for c in copies(b, slot):
        c.wait()
    return slot


def _compress_paged_kernel(pt_ref, pages_hbm, perm_ref, w_ref, pos_ref, out_ref, buf_ref, xp_ref, sem_ref,
                           *, n_pages, n_chunks):
    half = KV_W // 2
    per_page = PAGE_SIZE // CMP_STRIDE
    slot = _gather_pages(pt_ref, pages_hbm, buf_ref, sem_ref, n_pages)
    for k in range(n_pages):
        xp_ref[k] = _dot_nt(perm_ref[...], buf_ref[slot, k].astype(bf16))

    def rows(l, c):
        x = xp_ref[:, l * per_page:(l + 1) * per_page, c * half:(c + 1) * half]
        return x.reshape(n_pages * per_page, half)

    _compress_core(rows, w_ref, pos_ref, out_ref, n_chunks)


def _pack_w_cmp(w_cmp, cmp_pos):
    w = w_cmp.reshape(2, CMP_STRIDE // 2, 2, 2, HEAD_DIM, HEAD_DIM)
    w4 = jnp.einsum("hplcde,gG->pclgdhGe", w, jnp.eye(N_KV, dtype=f32))
    w4 = w4.reshape(CMP_STRIDE // 2, 2, KV_W, KV_W)

    def pos_row(p):
        return jnp.broadcast_to(p[:, :, None, :], (CMP_STRIDE, 2, N_KV, HEAD_DIM)).reshape(1, -1)

    pos = jnp.concatenate([pos_row(cmp_pos[:CMP_STRIDE]), pos_row(cmp_pos[CMP_STRIDE:]),
                           jnp.zeros((SUBLANES - 2, CMP_STRIDE * KV_W), f32)], axis=0)
    return w4.astype(bf16), pos.astype(bf16)


def _compress_prompt(kv_cmp, w4, pos, b, t):
    n_chunks = t // CMP_STRIDE
    cw = CMP_STRIDE * KV_W
    chunks = kv_cmp.reshape(-1, cw)
    ncp = _ncp(n_chunks)
    return pl.pallas_call(
        functools.partial(_compress_prompt_kernel, n_chunks=n_chunks),
        grid=(b,),
        in_specs=[pl.BlockSpec((n_chunks, cw), lambda i: (i, 0)), _resident(w4.shape), _resident(pos.shape)],
        out_specs=pl.BlockSpec((1, ncp, KV_W), lambda i: (i, 0, 0)),
        out_shape=jax.ShapeDtypeStruct((b, ncp, KV_W), bf16),
        compiler_params=_cparams(("parallel",)),
        name="compress_prompt",
    )(chunks, w4, pos)


def _pages_t(cache):
    n_pool = cache.shape[0]
    return jnp.transpose(cache, (0, 2, 3, 4, 1)).reshape(n_pool, KV_W, cache.shape[1])


def _compress_sample(pages_t, page_table, w4, pos):
    bd, n_pages = page_table.shape
    per_page = PAGE_SIZE // CMP_STRIDE
    n_chunks = n_pages * per_page
    ncp = _ncp(n_chunks)
    tok = np.arange(PAGE_SIZE)
    perm = jnp.asarray(tok[None, :] == (tok[:, None] % per_page) * CMP_STRIDE + tok[:, None] // per_page, dtype=bf16)

    res = lambda a: pl.BlockSpec(a.shape, lambda i, pt: (0,) * a.ndim)
    grid_spec = pltpu.PrefetchScalarGridSpec(
        num_scalar_prefetch=1,
        grid=(bd,),
        in_specs=[pl.BlockSpec(memory_space=pl.ANY), res(perm), res(w4), res(pos)],
        out_specs=pl.BlockSpec((1, ncp, KV_W), lambda i, pt: (i, 0, 0)),
        scratch_shapes=[pltpu.VMEM((2, n_pages, KV_W, PAGE_SIZE), f32), pltpu.VMEM((n_pages, PAGE_SIZE, KV_W), f32),
                        pltpu.SemaphoreType.DMA((2,))],
    )
    return pl.pallas_call(
        functools.partial(_compress_paged_kernel, n_pages=n_pages, n_chunks=n_chunks),
        grid_spec=grid_spec,
        out_shape=jax.ShapeDtypeStruct((bd, ncp, KV_W), bf16),
        compiler_params=_cparams(("arbitrary",)),
        name="compress_sample",
    )(page_table, pages_t, perm, w4, pos)


def _overlap_t(n_sel_rows, n_sel, ncp, n_cmp):
    j = np.arange(n_sel_rows)[:, None]
    n = np.arange(ncp)[None, :] - CMP_PAD
    hit = (n * CMP_STRIDE < (j + 1) * SEL_BLOCK) & (n * CMP_STRIDE + CMP_BLOCK > j * SEL_BLOCK)
    hit &= (n >= 0) & (n < n_cmp) & (j < n_sel)
    return jnp.asarray(hit, dtype=bf16)


def _masked_softmax_cols(s, maybe_empty=True):
    m = jnp.max(s, axis=0, keepdims=True)
    e = jnp.exp(s - m)
    inv = 1.0 / jnp.sum(e, axis=0, keepdims=True)
    if maybe_empty:
        inv = jnp.where(m > 0.5 * NEG, inv, 0.0)
    return e * inv


def _top_rows(imp, k):
    rows = lax.broadcasted_iota(i32, imp.shape, 0).astype(f32)
    sel = jnp.zeros(imp.shape, jnp.bool_)
    v = imp
    for _ in range(k):
        m = jnp.max(v, axis=0, keepdims=True)
        first = jnp.min(jnp.where(v == m, rows, 1e9), axis=0, keepdims=True)
        pick = rows == first
        sel = sel | pick
        v = jnp.where(pick, -jnp.inf, v)
    return sel


def _flash_steps(carry, scores, v_ext):
    m_new = [jnp.maximum(carry[2 * i], jnp.max(s, axis=0, keepdims=True)) for i, s in enumerate(scores)]
    p = [jnp.exp(s - m).astype(bf16) for s, m in zip(scores, m_new)]
    out = []
    for i in range(len(scores)):
        alpha = jnp.exp(carry[2 * i] - m_new[i])
        out.extend((m_new[i], alpha * carry[2 * i + 1] + _dot(v_ext, p[i])))
    return tuple(out)


def _attn_prompt_kernel(q_ref, gn_ref, kc_ref, vce_ref, ks_ref, vse_ref, kw_ref, vwe_ref, emat_ref, ove_ref,
                        wt_ref, ct_ref, out_ref, cadd_ref, *, n_sel, ncp):
    qt = pl.program_id(1)
    cols = GROUP * Q_TILE
    n_w = WINDOW // Q_TILE + 1
    vrows = KV_W // 2
    gates_t = gn_ref[...].T
    n_far = jnp.maximum(qt - 1, 0) // (FAR_CHUNK // Q_TILE)
    groups = range(N_KV)
    qp = [jnp.concatenate(
        [q_ref[:, (g * GROUP + hh) * LANES:(g * GROUP + hh + 1) * LANES] for hh in range(GROUP)], axis=0)
        for g in groups]

    wk = n_w * Q_TILE
    start = pl.multiple_of(qt * Q_TILE, Q_TILE)
    lc, sw = [], []
    for g in groups:
        ri = lax.broadcasted_iota(i32, (ncp, cols), 0)
        cadd_ref[g] = jnp.where((ri >= CMP_PAD) & (ri < SUBLANES * qt), 0.0, NEG)
        wr = SUBLANES * qt + lax.broadcasted_iota(i32, (CMP_WIN, cols), 0)
        cadd_ref[g, pl.ds(pl.multiple_of(SUBLANES * qt, SUBLANES), CMP_WIN), :] = jnp.where(wr >= CMP_PAD, ct_ref[g], NEG)
        lc.append(_dot_nt(kc_ref[0], qp[g]) + cadd_ref[g])
    for g in groups:
        tiles = [wt_ref[g, jnp.where(qt + w >= n_w - 1, w + 1, 0)] for w in range(n_w)]
        sw.append(_dot_nt(kw_ref[0, pl.ds(start, wk), :], qp[g]) + jnp.concatenate(tiles, axis=0))
    live, e_c, e_w = [], [], []
    for g in groups:
        m = jnp.max(lc[g], axis=0, keepdims=True)
        live.append(m > 0.5 * NEG)
        e_c.append(jnp.exp(lc[g] - m))
    for g in groups:
        e_w.append(jnp.exp(sw[g] - jnp.max(sw[g], axis=0, keepdims=True)).astype(bf16))

    o_c, o_w, qx = [], [], []
    for g in groups:
        oc = _dot(vce_ref[0], e_c[g].astype(bf16))
        o_c.append(oc[:vrows] * jnp.where(live[g], 1.0 / oc[vrows:vrows + 1], 0.0))
        e_hi, e_lo = _split_bf16(e_c[g])
        raw = _dot(ove_ref[...], e_hi) + _dot(ove_ref[...], e_lo)
        raw = raw[:n_sel] * jnp.where(live[g], 1.0 / raw[n_sel:n_sel + 1], 0.0)
        imp = raw[:, 0:Q_TILE]
        for hh in range(1, GROUP):
            imp = imp + raw[:, hh * Q_TILE:(hh + 1) * Q_TILE]

        blk = lax.broadcasted_iota(i32, (n_sel, Q_TILE), 0)
        qpos = qt * Q_TILE + lax.broadcasted_iota(i32, (n_sel, Q_TILE), 1)
        cur = lax.shift_right_logical(qpos, SEL_SHIFT)
        forced = (blk == 0) | (blk == cur) | (blk == cur - 1)
        future = blk * SEL_BLOCK > qpos
        imp = jnp.where(forced, FORCE, jnp.where(future, -FORCE, imp))
        sel = _top_rows(imp, min(SEL_TOP, n_sel))
        sb = jnp.where(sel, 0.0, NEG)
        if n_sel < LANES:
            sb = jnp.concatenate([sb, jnp.zeros((LANES - n_sel, Q_TILE), f32)], axis=0)
        sbq = sb.T.astype(bf16)
        qx.append(jnp.concatenate([qp[g], jnp.concatenate([sbq] * GROUP, axis=0)], axis=1))
    for g in groups:
        ow = _dot(vwe_ref[0, :, pl.ds(start, wk)], e_w[g])
        o_w.append(ow[:vrows] * (1.0 / ow[vrows:vrows + 1]))

    def scores(start, size, g):
        kx = jnp.concatenate([ks_ref[pl.ds(start, size), :], emat_ref[pl.ds(start, size), :]], axis=1)
        return _dot_nt(kx, qx[g])

    def far_step(i, carry):
        ks = pl.multiple_of(i * FAR_CHUNK, FAR_CHUNK)
        v_ext = vse_ref[0, :, pl.ds(ks, FAR_CHUNK)]
        return _flash_steps(carry, [scores(ks, FAR_CHUNK, g) for g in groups], v_ext)

    def near_step(kt, carry):
        ks = pl.multiple_of(kt * Q_TILE, Q_TILE)
        v_ext = vse_ref[0, :, pl.ds(ks, Q_TILE)]
        tile = jnp.where(kt >= qt - 1, kt - qt + n_w, 2)
        return _flash_steps(carry, [scores(ks, Q_TILE, g) + wt_ref[g, tile] for g in groups], v_ext)

    init = (jnp.full((1, cols), NEG, f32), jnp.zeros((vse_ref.shape[1], cols), f32)) * N_KV
    carry = lax.fori_loop(0, n_far, far_step, init)
    carry = lax.fori_loop(n_far * (FAR_CHUNK // Q_TILE), qt + 1, near_step, carry)

    heads_t = []
    for g in groups:
        acc = carry[2 * g + 1]
        o_s = acc[:vrows] * (1.0 / acc[vrows:vrows + 1])
        for hh in range(GROUP):
            head = g * GROUP + hh
            cs = slice(hh * Q_TILE, (hh + 1) * Q_TILE)
            ds = slice(g * HEAD_DIM, (g + 1) * HEAD_DIM)
            heads_t.append(gates_t[3 * head:3 * head + 1, :] * o_c[g][ds, cs]
                           + gates_t[3 * head + 1:3 * head + 2, :] * o_s[ds, cs]
                           + gates_t[3 * head + 2:3 * head + 3, :] * o_w[g][ds, cs])
    for pair in range(N_HEADS // 2):
        both = jnp.concatenate(heads_t[2 * pair:2 * pair + 2], axis=0)
        out_ref[:, pair * LANES:(pair + 1) * LANES] = both.T


def _attn_prompt(q_pk, gates, kcp, ks_bf, kw_bf, wt, ct, b, t):
    ncp = kcp.shape[1]
    n_sel = t // SEL_BLOCK
    n_cmp = t // CMP_STRIDE - 1
    n_qt = t // Q_TILE
    assert t % FAR_CHUNK == 0 and n_sel % SUBLANES == 0 and n_sel <= LANES
    cols = GROUP * Q_TILE
    half = KV_W // 2

    def values_t(v):
        v_t = jnp.swapaxes(v, 1, 2)
        return jnp.concatenate([v_t, jnp.ones((b, ONES_ROWS, v.shape[1]), v.dtype)], axis=1)

    prompt = lambda a: a[:b * t].reshape(b, t, -1)
    kw_pad = jnp.pad(prompt(kw_bf), ((0, 0), (WINDOW, 0), (0, 0)))
    vce, vse, vwe = values_t(kcp[:, :, half:]), values_t(prompt(ks_bf)[:, :, half:]), values_t(kw_pad[:, :, half:])
    ove = jnp.concatenate([_overlap_t(n_sel, n_sel, ncp, n_cmp), jnp.ones((ONES_ROWS, ncp), bf16)], axis=0)
    emat = jnp.asarray(np.arange(t)[:, None] // SEL_BLOCK == np.arange(LANES)[None, :], dtype=bf16)
    tp = t + WINDOW
    vr = half + ONES_ROWS
    per_b = lambda shape: pl.BlockSpec((1,) + shape, lambda i, j: (i, 0, 0))
    q_rows = lambda w: pl.BlockSpec((Q_TILE, w), lambda i, j: (i * n_qt + j, 0))
    return pl.pallas_call(
        functools.partial(_attn_prompt_kernel, n_sel=n_sel, ncp=ncp),
        grid=(b, n_qt),
        in_specs=[
            q_rows(N_HEADS * LANES), q_rows(LANES),
            per_b((ncp, half)), per_b((vr, ncp)),
            pl.BlockSpec((t, half), lambda i, j: (i, 0)), per_b((vr, t)),
            per_b((tp, half)), per_b((vr, tp)),
            _resident(emat.shape), _resident(ove.shape), _resident(wt.shape), _resident(ct.shape),
        ],
        out_specs=q_rows(ATTN_DIM),
        out_shape=jax.ShapeDtypeStruct((b * t, ATTN_DIM), f32),
        scratch_shapes=[pltpu.VMEM((N_KV, ncp, cols), f32)],
        compiler_params=_cparams(("parallel", "arbitrary")),
        name="attn_prompt",
    )(q_pk, gates, kcp, vce, ks_bf, vse, kw_pad, vwe, emat, ove, wt, ct)


def _softmax_rows2(s_main, s_new):
    m = jnp.maximum(jnp.max(s_main, axis=1, keepdims=True), jnp.max(s_new, axis=1, keepdims=True))
    e_main, e_new = jnp.exp(s_main - m), jnp.exp(s_new - m)
    inv = 1.0 / (jnp.sum(e_main, axis=1, keepdims=True) + jnp.sum(e_new, axis=1, keepdims=True))
    return (e_main * inv).astype(bf16), (e_new * inv).astype(bf16)


def _attn_sample_kernel(pt_ref, pages_hbm, q_ref, g_ref, kcp_ref, knew_ref, win_ref, wnew_ref, ovt_ref, rmat_ref,
                        emat_ref, scmp_ref, sk_ref, snew_ref, swin_ref, c31_ref, out_ref, buf_ref, ke_ref, vt_ref,
                        sem_ref, *, n_pages, n_sel, past_len):
    half = KV_W // 2
    slot = _gather_pages(pt_ref, pages_hbm, buf_ref, sem_ref, n_pages)
    q = q_ref[0]
    q_rows = q[0:SAMPLE_COLS]

    kcp = kcp_ref[0]
    pc = _masked_softmax_cols(_dot_nt(kcp[:, :half], q) + scmp_ref[...])
    o_c = _dot_tn(pc.astype(bf16), kcp[:, half:])[0:SAMPLE_COLS]
    p_hi, p_lo = _split_bf16(pc)
    imp = _dot(ovt_ref[...], p_hi) + _dot(ovt_ref[...], p_lo)
    i_hi, i_lo = _split_bf16(imp)
    imp = _dot(i_hi, rmat_ref[...]) + _dot(i_lo, rmat_ref[...])
    rows = imp.shape[0]
    blk = lax.broadcasted_iota(i32, (rows, LANES), 0)
    qpos = past_len + (lax.broadcasted_iota(i32, (rows, LANES), 1) & 3)
    cur = lax.shift_right_logical(qpos, SEL_SHIFT)
    forced = (blk == 0) | (blk == cur) | (blk == cur - 1)
    future = blk * SEL_BLOCK > qpos
    imp = jnp.where(forced, FORCE, jnp.where(future, -FORCE, imp))
    imp = jnp.where(blk < n_sel, imp, -jnp.inf)
    sel = _top_rows(imp, min(SEL_TOP, n_sel))
    sb = jnp.where(sel, 0.0, NEG)
    past_blocks = past_len // SEL_BLOCK
    sb_rows = sb[0:past_blocks].T[0:SAMPLE_COLS]

    for k in range(n_pages):
        lanes = slice(k * PAGE_SIZE, (k + 1) * PAGE_SIZE)
        ke_ref[0:half, lanes] = buf_ref[slot, k, 0:half, :].astype(bf16)
        vt_ref[:, lanes] = buf_ref[slot, k, half:, :].astype(bf16)

    @pl.when(pl.program_id(0) == 0)
    def _membership_rows():
        ke_ref[half:, :] = emat_ref[...]

    n_far = past_len - PAGE_SIZE
    s = _dot(jnp.concatenate([q_rows, sb_rows.astype(bf16)], axis=1), ke_ref[...])
    s = jnp.concatenate([s[:, :n_far] + c31_ref[:, 0:1], s[:, n_far:] + sk_ref[...]], axis=1)
    knew = knew_ref[0].astype(bf16)
    s_n = _dot(q_rows, knew[0:half]) + snew_ref[...]
    p, p_n = _softmax_rows2(s, s_n)
    o_s = _dot_nt(p, vt_ref[...]) + _dot_nt(p_n, knew[half:])

    win = win_ref[0].astype(bf16)
    wnew = wnew_ref[0].astype(bf16)
    p, p_n = _softmax_rows2(_dot(q_rows, win[0:half]) + swin_ref[...], _dot(q_rows, wnew[0:half]) + snew_ref[...])
    o_w = _dot_nt(p, win[half:]) + _dot_nt(p_n, wnew[half:])

    gt = g_ref[0]
    out_ref[0] = gt[:, 0:1] * o_c + gt[:, 1:2] * o_s + gt[:, 2:3] * o_w


def _attn_sample(q_pk, gates, kcp, pages_t, page_table, ks_new, win_t, kw_new, tabs):
    scmp, sk, snew, swin, c31s = tabs
    bd, t_new, _ = q_pk.shape
    n_pages = page_table.shape[1]
    past_len = n_pages * PAGE_SIZE
    ncp = kcp.shape[1]
    n_cmp = past_len // CMP_STRIDE - 1
    n_sel = -(-(past_len + t_new) // SEL_BLOCK)
    n_sel_rows = -(-n_sel // SUBLANES) * SUBLANES
    past_blocks = past_len // SEL_BLOCK
    n_cols = N_KV * GROUP * t_new
    assert n_cols == SAMPLE_COLS and win_t.shape[2] == WINDOW and past_blocks == LANES
    qc = q_pk.reshape(bd, t_new, N_HEADS, LANES).transpose(0, 2, 1, 3).reshape(bd, n_cols, LANES)
    qc = jnp.pad(qc, ((0, 0), (0, LANES - n_cols), (0, 0)))
    gc = gates[:, :, :3 * N_HEADS].reshape(bd, t_new, N_HEADS, 3).transpose(0, 2, 1, 3).reshape(bd, n_cols, 3)
    gc = jnp.pad(gc, ((0, 0), (0, 0), (0, SUBLANES - 3)))
    new_t = lambda a: jnp.pad(jnp.swapaxes(a, 1, 2), ((0, 0), (0, 0), (0, LANES - t_new)))
    ovt = _overlap_t(n_sel_rows, n_sel, ncp, n_cmp)
    c = np.arange(LANES)
    same = (c[:, None] // (GROUP * t_new) == c[None, :] // (GROUP * t_new)) & (c[:, None] % t_new == c[None, :] % t_new)
    rmat = jnp.asarray(same & (c[:, None] < n_cols) & (c[None, :] < n_cols), dtype=bf16)
    emat = jnp.asarray(np.arange(past_blocks)[:, None] == np.arange(past_len)[None, :] // SEL_BLOCK, dtype=bf16)

    per_b = lambda shape: pl.BlockSpec((1,) + shape, lambda i, pt: (i, 0, 0))
    res = lambda a: pl.BlockSpec(a.shape, lambda i, pt: (0,) * a.ndim)
    half = KV_W // 2
    grid_spec = pltpu.PrefetchScalarGridSpec(
        num_scalar_prefetch=1,
        grid=(bd,),
        in_specs=[pl.BlockSpec(memory_space=pl.ANY),
                  per_b((LANES, LANES)), per_b((n_cols, SUBLANES)), per_b((ncp, KV_W)), per_b((KV_W, LANES)),
                  per_b((KV_W, WINDOW)), per_b((KV_W, LANES)),
                  res(ovt), res(rmat), res(emat), res(scmp), res(sk), res(snew), res(swin), res(c31s)],
        out_specs=per_b((n_cols, LANES)),
        scratch_shapes=[pltpu.VMEM((2, n_pages, KV_W, PAGE_SIZE), f32),
                        pltpu.VMEM((half + past_blocks, past_len), bf16), pltpu.VMEM((half, past_len), bf16),
                        pltpu.SemaphoreType.DMA((2,))],
    )
    o = pl.pallas_call(
        functools.partial(_attn_sample_kernel, n_pages=n_pages, n_sel=n_sel, past_len=past_len),
        grid_spec=grid_spec,
        out_shape=jax.ShapeDtypeStruct((bd, n_cols, LANES), f32),
        compiler_params=_cparams(("arbitrary",)),
        name="attn_sample",
    )(page_table, pages_t, qc, gc, kcp, new_t(ks_new), win_t, new_t(kw_new),
      ovt, rmat, emat, scmp, sk, snew, swin, c31s)
    o = o.reshape(bd, N_KV, GROUP, t_new, N_KV, HEAD_DIM)
    o = jnp.stack([o[:, g, :, :, g, :] for g in range(N_KV)], axis=1)
    return o.transpose(0, 3, 1, 2, 4).reshape(bd, t_new, ATTN_DIM)


def _merge_kernel(xp_ref, xs_ref, ap_ref, as_ref, u0_ref, halo_ref, u1s_ref, u2s_ref, cb_ref, mg_ref, cw_ref, cbias_ref,
                  wau_ref, wcu_ref, wo_ref, g2_ref, x2_ref, h2_ref, *, d_model, prompt_tiles, seq_tiles):
    i = pl.program_id(0)
    u0 = u0_ref[...]
    r = lax.broadcasted_iota(i32, u0.shape, 0)
    keep = jnp.where(i % seq_tiles == 0, 0.0, 1.0)
    p1 = halo_ref[SUBLANES - 1:SUBLANES, :] * keep
    p2 = halo_ref[SUBLANES - 2:SUBLANES - 1, :] * keep
    u1 = jnp.where(r == 0, p1, pltpu.roll(u0, 1, axis=0))
    u2 = jnp.where(r == 0, p2, jnp.where(r == 1, p1, pltpu.roll(u0, 2, axis=0)))
    decode = i >= prompt_tiles
    u1 = jnp.where(decode, u1s_ref[...], u1)
    u2 = jnp.where(decode, u2s_ref[...], u2)
    y = cbias_ref[...] + cw_ref[0:1, :] * u2
    y = y + cw_ref[1:2, :] * u1
    y = y + cw_ref[2:3, :] * u0
    up_a = _dot(_two_sources(ap_ref, as_ref, prompt_tiles).astype(bf16), wau_ref[...])
    up_c = _dot((cb_ref[...] * y).astype(bf16), wcu_ref[...])
    mixed = mg_ref[:, :d_model] * up_a + mg_ref[:, d_model:] * up_c
    x2 = _two_sources(xp_ref, xs_ref, prompt_tiles) + _dot(mixed.astype(bf16), wo_ref[...])
    x2_ref[...] = x2
    h2_ref[...] = _rmsnorm(x2, g2_ref[...]).astype(bf16)


def _merge(x_p, x_s, attn_p, attn_s, cin, u1s, u2s, cb, mg, conv_w, conv_b, w_attn_up, w_conv_up, w_out, norm2_g, t):
    n_prompt, d = x_p.shape
    n = n_prompt + x_s.shape[0]
    tm = ROW_TILE
    assert CONV_W == 3 and t % tm == 0 and n_prompt % tm == 0 and (n - n_prompt) % tm == 0
    prompt_tiles = n_prompt // tm
    row = lambda w: pl.BlockSpec((tm, w), lambda i: (i, 0))
    halo = pl.BlockSpec((SUBLANES, CONV_DIM), lambda i: (jnp.maximum(i * (tm // SUBLANES) - 1, 0), 0))
    dec = pl.BlockSpec((tm, CONV_DIM), lambda i: (jnp.maximum(i - prompt_tiles, 0), 0))
    cw = jnp.pad(conv_w, ((0, SUBLANES - CONV_W), (0, 0)))
    return pl.pallas_call(
        functools.partial(_merge_kernel, d_model=d, prompt_tiles=prompt_tiles, seq_tiles=t // tm),
        grid=(n // tm,),
        in_specs=_two_specs(tm, d, prompt_tiles) + _two_specs(tm, ATTN_DIM, prompt_tiles)
        + [row(CONV_DIM), halo, dec, dec, row(CONV_DIM), row(2 * d),
           _resident(cw.shape), _resident((1, CONV_DIM)), _resident(w_attn_up.shape),
           _resident(w_conv_up.shape), _resident(w_out.shape), _resident((1, d))],
        out_specs=[row(d), row(d)],
        out_shape=[jax.ShapeDtypeStruct((n, d), f32), jax.ShapeDtypeStruct((n, d), bf16)],
        compiler_params=_cparams(("parallel",)),
        name="merge",
    )(x_p, x_s, attn_p, attn_s, cin, cin, u1s, u2s, cb, mg, cw, conv_b.reshape(1, CONV_DIM), w_attn_up.astype(bf16),
      w_conv_up.astype(bf16), w_out.astype(bf16), norm2_g.reshape(1, d))


def _top_rows_sorted(s, k):
    rows = lax.broadcasted_iota(i32, s.shape, 0).astype(f32)
    vals, idxs = [], []
    for _ in range(k):
        m = jnp.max(s, axis=0, keepdims=True)
        first = jnp.min(jnp.where(s == m, rows, 1e9), axis=0, keepdims=True)
        vals.append(m)
        idxs.append(first)
        s = jnp.where(rows == first, -jnp.inf, s)
    return jnp.concatenate(vals, axis=0), jnp.concatenate(idxs, axis=0)


def _peer_route_kernel(h_ref, wq_ref, k1_ref, k2_ref, a_ref, b_ref, g_ref):
    qp = _dot(h_ref[...], wq_ref[...]).astype(bf16)
    key_dim = 2 * PEER_HALF
    a_rows, b_rows, g_rows = [], [], []
    for h in range(PEER_HEADS):
        q1 = qp[:, h * key_dim:h * key_dim + PEER_HALF]
        q2 = qp[:, h * key_dim + PEER_HALF:(h + 1) * key_dim]
        v1, i1 = _top_rows_sorted(_dot_nt(k1_ref[...], q1), PEER_TOPK)
        v2, i2 = _top_rows_sorted(_dot_nt(k2_ref[...], q2), PEER_TOPK)
        counts = [PEER_TOPK // (r + 1) for r in range(PEER_TOPK)]
        cand = jnp.concatenate([v1[r:r + 1, :] + v2[0:n, :] for r, n in enumerate(counts)], axis=0)
        code = jnp.concatenate([i1[r:r + 1, :] * PEER_NKEYS + i2[0:n, :] for r, n in enumerate(counts)], axis=0)
        pad = -sum(counts) % SUBLANES
        cand = jnp.concatenate([cand, jnp.full((pad, cand.shape[1]), -jnp.inf, f32)], axis=0)
        code = jnp.concatenate([code, jnp.zeros((pad, code.shape[1]), f32)], axis=0)
        rows = lax.broadcasted_iota(i32, cand.shape, 0).astype(f32)
        sc, ex = [], []
        for _ in range(PEER_TOPK):
            m = jnp.max(cand, axis=0, keepdims=True)
            first = jnp.min(jnp.where(cand == m, rows, 1e9), axis=0, keepdims=True)
            pick = rows == first
            sc.append(m)
            ex.append(jnp.max(jnp.where(pick, code, -1.0), axis=0, keepdims=True))
            cand = jnp.where(pick, -jnp.inf, cand)
        sc = jnp.concatenate(sc, axis=0)
        ex = jnp.concatenate(ex, axis=0)
        e1 = jnp.floor(ex * (1.0 / PEER_NKEYS))
        e = jnp.exp(sc - sc[0:1, :])
        a_rows.append(e1)
        b_rows.append(ex - e1 * PEER_NKEYS)
        g_rows.append(e / jnp.sum(e, axis=0, keepdims=True))
    tn = h_ref.shape[0]
    for src, dst in ((a_rows, a_ref), (b_rows, b_ref), (g_rows, g_ref)):
        full = jnp.concatenate(src, axis=0)
        for c in range(tn // LANES):
            dst[c * LANES:(c + 1) * LANES, :] = full[:, c * LANES:(c + 1) * LANES].T


def _peer_route(h2, wq, k1, k2):
    n, d = h2.shape
    tn = ROW_TILE
    assert PEER_HEADS * PEER_TOPK == LANES
    row = lambda w: pl.BlockSpec((tn, w), lambda i: (i, 0))
    return pl.pallas_call(
        _peer_route_kernel,
        grid=(n // tn,),
        in_specs=[row(d), _resident(wq.shape), _resident(k1.shape), _resident(k2.shape)],
        out_specs=[row(LANES)] * 3,
        out_shape=[jax.ShapeDtypeStruct((n, LANES), f32)] * 3,
        compiler_params=_cparams(("parallel",)),
        name="peer_route",
    )(h2, wq.astype(bf16), k1.astype(bf16), k2.astype(bf16))


def _peer_expert_kernel(h_ref, a_ref, b_ref, w_ref, x_ref, u_ref, v_ref, out_ref, g_ref, acc_ref, *, pitch):
    e_step = pl.program_id(1)
    tn = h_ref.shape[0]
    pairs_per_chunk = EXPERT_CHUNK // PEER_NKEYS // 2
    half_keys = PEER_NKEYS // 2

    @pl.when(e_step == 0)
    def _build_gate_matrix():
        key_row = lax.broadcasted_iota(i32, (PEER_NKEYS, LANES), 0).astype(f32)

        def token(n, c):
            a = jnp.broadcast_to(a_ref[pl.ds(n, 1), :], (PEER_NKEYS, LANES))
            b = jnp.broadcast_to(b_ref[pl.ds(n, 1), :], (PEER_NKEYS, LANES))
            w = jnp.broadcast_to(w_ref[pl.ds(n, 1), :], (PEER_NKEYS, LANES))
            pa = (a == key_row).astype(bf16)
            wb = jnp.where(b == key_row, w, 0.0).astype(bf16)
            g = _dot_nt(pa, wb).astype(bf16)
            g_ref[pl.ds(n, half_keys, stride=pitch), :] = pltpu.bitcast(g, jnp.uint32)
            return c

        lax.fori_loop(0, tn, token, 0, unroll=TOKEN_UNROLL)
        acc_ref[...] = jnp.zeros_like(acc_ref)

    s = _dot_nt(h_ref[...], u_ref[...])
    gates = []
    for j in range(pairs_per_chunk):
        words = g_ref[pl.ds(pl.multiple_of((e_step * pairs_per_chunk + j) * pitch, SUBLANES), tn), :]
        gates.append(pltpu.bitcast(words, bf16).reshape(tn, 2 * LANES))
    gates = jnp.concatenate(gates, axis=1).astype(f32)
    act = 0.5 * s * (1.0 + lax.erf(s * math.sqrt(0.5)))
    acc_ref[...] += _dot((gates * act).astype(bf16), v_ref[...])

    @pl.when(e_step == pl.num_programs(1) - 1)
    def _finish():
        out_ref[...] = x_ref[...] + acc_ref[...]


def _peer_expert(h2, a_idx, b_idx, gate_w, x2, u_tab, v_tab):
    n, d = h2.shape
    n_exp = u_tab.shape[0]
    tn = PEER_TILE
    pitch = tn + G_PITCH_PAD
    assert n % tn == 0 and n_exp == PEER_NKEYS * PEER_NKEYS and EXPERT_CHUNK % (2 * PEER_NKEYS) == 0
    row = lambda w: pl.BlockSpec((tn, w), lambda i, e: (i, 0))
    tab = pl.BlockSpec((EXPERT_CHUNK, d), lambda i, e: (e, 0))
    return pl.pallas_call(
        functools.partial(_peer_expert_kernel, pitch=pitch),
        grid=(n // tn, n_exp // EXPERT_CHUNK),
        in_specs=[row(d), row(LANES), row(LANES), row(LANES), row(d), tab, tab],
        out_specs=row(d),
        out_shape=jax.ShapeDtypeStruct((n, d), f32),
        scratch_shapes=[pltpu.VMEM((PEER_NKEYS // 2 * pitch, LANES), jnp.uint32), pltpu.VMEM((tn, d), f32)],
        compiler_params=_cparams(("parallel", "arbitrary")),
        name="peer_expert",
    )(h2, a_idx, b_idx, gate_w, x2, u_tab.astype(bf16), v_tab.astype(bf16))


def _ple_kernel(x_ref, p_ref, pg_ref, wg_ref, wp_ref, fg_ref, y_ref):
    x = x_ref[...]
    gate = jax.nn.sigmoid(_dot(_rmsnorm(x, pg_ref[...]).astype(bf16), wg_ref[...]))
    x = x + gate * _dot(p_ref[...].astype(bf16), wp_ref[...])
    y_ref[...] = _rmsnorm(x, fg_ref[...])


def _ple(x3, row0, p, ple_g, w_ple_gate, w_ple_proj, final_g):
    n, d = p.shape[0], x3.shape[1]
    tm = ROW_TILE
    assert n % tm == 0 and row0 % tm == 0
    row = lambda w: pl.BlockSpec((tm, w), lambda i: (i, 0))
    return pl.pallas_call(
        _ple_kernel,
        grid=(n // tm,),
        in_specs=[pl.BlockSpec((tm, d), lambda i: (i + row0 // tm, 0)), row(p.shape[1]), _resident((1, d)),
                  _resident(w_ple_gate.shape), _resident(w_ple_proj.shape), _resident((1, d))],
        out_specs=row(d),
        out_shape=jax.ShapeDtypeStruct((n, d), f32),
        compiler_params=_cparams(("parallel",)),
        name="ple_final_rows_%d" % row0,
    )(x3, p, ple_g.reshape(1, d), w_ple_gate.astype(bf16), w_ple_proj.astype(bf16), final_g.reshape(1, d))


def kernel(x_prompt, x_sample, p_prompt, p_sample, cache_cmp_kv, cache_slc_kv, page_table, state_win_kv, state_conv, norm1_g, w_in, w_cmp, cmp_pos, conv_w, conv_b, w_attn_up, w_conv_up, w_out, norm2_g, peer_wq, peer_k1, peer_k2, peer_u, peer_v, ple_g, w_ple_gate, w_ple_proj, rel_bias, final_g):
    depth = norm1_g.shape[0]
    assert depth == 1, "single-layer trunk"
    b, t, d = x_prompt.shape
    bd, t_new, _ = x_sample.shape
    n_pages = page_table.shape[1]
    past_len = n_pages * PAGE_SIZE
    n_p, n_s = b * t, bd * t_new
    kv_shape = lambda lead: lead + (2, N_KV, HEAD_DIM)

    x_p, x_s = x_prompt.reshape(n_p, d), x_sample.reshape(n_s, d)
    w_pack = _pack_w_in(w_in[0], d)
    q_pk, kvc, kvs, kvw, kvs_bf, kvw_bf, gates, cin, cbg, mg = _in_proj(x_p, x_s, norm1_g[0], w_pack)
    prompt = lambda a: a[:n_p].reshape(b, t, -1)
    sample = lambda a: a[n_p:].reshape(bd, t_new, -1)

    w4, pos_rows = _pack_w_cmp(w_cmp[0], cmp_pos[0])
    kcp_p = _compress_prompt(kvc, w4, pos_rows, b, t)
    kcp_s = _compress_sample(_pages_t(cache_cmp_kv[0]), page_table, w4, pos_rows)
    n_cmp_s = past_len // CMP_STRIDE - 1
    wt, ct, *sample_tabs = _tables(rel_bias, past_len, kcp_s.shape[1], n_cmp_s)

    kvs_s, kvw_s = sample(kvs), sample(kvw)
    attn_p = _attn_prompt(q_pk, gates, kcp_p, kvs_bf, kvw_bf, wt, ct, b, t)
    win_t = jnp.transpose(state_win_kv[0], (0, 2, 3, 4, 1)).reshape(bd, KV_W, -1)
    attn_s = _attn_sample(sample(q_pk), sample(gates), kcp_s, _pages_t(cache_slc_kv[0]), page_table, kvs_s, win_t,
                          kvw_s, sample_tabs)

    cin_s = sample(cin)
    buf_s = state_conv[0]
    back = lambda k: jnp.concatenate([buf_s[:, CONV_W - 1 - k:], cin_s[:, :t_new - k]], axis=1).reshape(n_s, CONV_DIM)
    x2, h2 = _merge(x_p, x_s, attn_p, attn_s.reshape(n_s, ATTN_DIM), cin, back(1), back(2), cbg, mg, conv_w[0],
                    conv_b[0], w_attn_up[0], w_conv_up[0], w_out[0], norm2_g[0], t)

    a_idx, b_idx, gate_w = _peer_route(h2, peer_wq[0], peer_k1[0], peer_k2[0])
    x3 = _peer_expert(h2, a_idx, b_idx, gate_w, x2, peer_u[0], peer_v[0])

    ple_w = (ple_g[0], w_ple_gate[0], w_ple_proj[0], final_g)
    y_p = _ple(x3, 0, p_prompt[0].reshape(n_p, -1), *ple_w)
    y_s = _ple(x3, n_p, p_sample[0].reshape(n_s, -1), *ple_w)

    n_win = min(WINDOW, t)
    assert t >= CONV_W - 1 and t_new >= CONV_W - 1
    new_win_s = jnp.concatenate([state_win_kv[:, :, t_new:], kvw_s.reshape(kv_shape((1, bd, t_new)))], axis=2)
    return (
        y_p.reshape(b, t, d),
        y_s.reshape(bd, t_new, d),
        prompt(kvc).reshape(kv_shape((1, b, t))),
        sample(kvc).reshape(kv_shape((1, bd, t_new))),
        prompt(kvs).reshape(kv_shape((1, b, t))),
        kvs_s.reshape(kv_shape((1, bd, t_new))),
        prompt(kvw)[:, t - n_win:].reshape(kv_shape((1, b, n_win))),
        new_win_s,
        prompt(cin)[None, :, t - (CONV_W - 1):],
        cin_s[None, :, t_new - (CONV_W - 1):],
    )
```

```python
import functools
import math

import jax
import jax.numpy as jnp
import numpy as np
from jax import lax
from jax.experimental import pallas as pl
from jax.experimental.pallas import tpu as pltpu

f32 = jnp.float32
bf16 = jnp.bfloat16
i32 = jnp.int32

N_HEADS = 8
N_KV = 2
GROUP = N_HEADS // N_KV
HEAD_DIM = 64
ATTN_DIM = N_HEADS * HEAD_DIM
KV_W = 2 * N_KV * HEAD_DIM
CMP_BLOCK = 32
CMP_STRIDE = 16
SEL_BLOCK = 64
SEL_SHIFT = 6
SEL_TOP = 16
WINDOW = 512
N_BUCKETS = 32
MAX_EXACT = N_BUCKETS // 2
MAX_DISTANCE = 128
CONV_DIM = 512
CONV_W = 3
PEER_HEADS = 8
PEER_NKEYS = 128
PEER_HALF = 128
PEER_TOPK = 16
PAGE_SIZE = 128
EPS = 1e-6
NEG = -1e30
FORCE = 1e4

LANES = 128
SUBLANES = 8
VMEM_LIMIT_BYTES = 56 * 1024 * 1024

Q_TILE = 128
SAMPLE_COLS = 32
FAR_CHUNK = 512
CMP_PAD = 16
CMP_WIN = 24
ONES_ROWS = 16
ROW_TILE = 256
PEER_TILE = 512
EXPERT_CHUNK = 2048
TOKEN_UNROLL = 32
G_PITCH_PAD = 8


def _cparams(sem):
    return pltpu.CompilerParams(dimension_semantics=sem, vmem_limit_bytes=VMEM_LIMIT_BYTES)


def _dot(a, b):
    return jnp.dot(a, b, preferred_element_type=f32)


def _dot_nt(a, b):
    return lax.dot_general(a, b, (((1,), (1,)), ((), ())), preferred_element_type=f32)


def _dot_tn(a, b):
    return lax.dot_general(a, b, (((0,), (0,)), ((), ())), preferred_element_type=f32)


def _split_bf16(x):
    hi = x.astype(bf16)
    lo = (x - hi.astype(f32)).astype(bf16)
    return hi, lo


def _resident(shape):
    nd = len(shape)
    return pl.BlockSpec(shape, lambda *_: (0,) * nd)


def _rmsnorm(x, g):
    ms = jnp.mean(x * x, axis=-1, keepdims=True)
    return x * lax.rsqrt(ms + EPS) * g


def _two_sources(first_ref, second_ref, first_tiles):
    return jnp.where(pl.program_id(0) < first_tiles, first_ref[...], second_ref[...])


def _two_specs(tile, width, first_tiles):
    return [pl.BlockSpec((tile, width), lambda i: (jnp.minimum(i, first_tiles - 1), 0)),
            pl.BlockSpec((tile, width), lambda i: (jnp.maximum(i - first_tiles, 0), 0))]


def _in_proj_kernel(xp_ref, xs_ref, g_ref, w_ref, q_ref, kc_ref, ks_ref, kw_ref, ksb_ref, kwb_ref,
                    gn_ref, cin_ref, cb_ref, mg_ref, *, d_model, prompt_tiles):
    qw = N_HEADS * LANES
    h = _rmsnorm(_two_sources(xp_ref, xs_ref, prompt_tiles), g_ref[...]).astype(bf16)
    o = 0
    q_ref[...] = (_dot(h, w_ref[:, o:o + qw]) * (HEAD_DIM ** -0.5)).astype(bf16)
    o += qw
    kc_ref[...] = _dot(h, w_ref[:, o:o + KV_W])
    o += KV_W
    ks = _dot(h, w_ref[:, o:o + KV_W])
    ks_ref[...] = ks
    ksb_ref[...] = ks.astype(bf16)
    o += KV_W
    kw = _dot(h, w_ref[:, o:o + KV_W])
    kw_ref[...] = kw
    kwb_ref[...] = kw.astype(bf16)
    o += KV_W
    gn_ref[...] = jax.nn.sigmoid(_dot(h, w_ref[:, o:o + LANES]))
    o += LANES
    cv = _dot(h, w_ref[:, o:o + CONV_DIM])
    o += CONV_DIM
    cb_ref[...] = _dot(h, w_ref[:, o:o + CONV_DIM])
    o += CONV_DIM
    cin_ref[...] = _dot(h, w_ref[:, o:o + CONV_DIM]) * cv
    o += CONV_DIM
    mg_ref[...] = jax.nn.sigmoid(_dot(h, w_ref[:, o:o + 2 * d_model]))


def _pack_w_in(w_in, d_model):
    sizes = (ATTN_DIM, KV_W, KV_W, KV_W, 3 * N_HEADS, CONV_DIM, CONV_DIM, CONV_DIM, 2 * d_model)
    parts, s = [], 0
    for n in sizes:
        parts.append(w_in[:, s:s + n])
        s += n
    wq = parts[0].reshape(d_model, N_KV, GROUP, HEAD_DIM)
    z = jnp.zeros_like(wq)
    slabs = [jnp.concatenate([wq[:, 0], z[:, 0]], axis=-1), jnp.concatenate([z[:, 1], wq[:, 1]], axis=-1)]
    wq = jnp.stack(slabs, axis=1).reshape(d_model, N_HEADS * LANES)
    gn = jnp.pad(parts[4], ((0, 0), (0, LANES - 3 * N_HEADS)))
    return jnp.concatenate([wq, parts[1], parts[2], parts[3], gn] + parts[5:], axis=1).astype(bf16)


def _in_proj(x_prompt, x_sample, norm_g, w_pack):
    d = x_prompt.shape[1]
    n = x_prompt.shape[0] + x_sample.shape[0]
    tm = ROW_TILE
    assert x_prompt.shape[0] % tm == 0 and x_sample.shape[0] % tm == 0
    prompt_tiles = x_prompt.shape[0] // tm
    row = lambda w: pl.BlockSpec((tm, w), lambda i: (i, 0))
    widths = (N_HEADS * LANES, KV_W, KV_W, KV_W, KV_W, KV_W, LANES, CONV_DIM, CONV_DIM, 2 * d)
    dtypes = (bf16, f32, f32, f32, bf16, bf16, f32, f32, f32, f32)
    return pl.pallas_call(
        functools.partial(_in_proj_kernel, d_model=d, prompt_tiles=prompt_tiles),
        grid=(n // tm,),
        in_specs=_two_specs(tm, d, prompt_tiles) + [_resident((1, d)), _resident(w_pack.shape)],
        out_specs=[row(w) for w in widths],
        out_shape=[jax.ShapeDtypeStruct((n, w), t) for w, t in zip(widths, dtypes)],
        compiler_params=_cparams(("parallel",)),
        name="in_proj",
    )(x_prompt, x_sample, norm_g.reshape(1, d), w_pack)


def _bias_of_dist(dist, valid, rb_ref, head, shift=0.0):
    n = jnp.maximum(dist, 0)
    nf = jnp.maximum(n, 1).astype(f32)
    large = MAX_EXACT + jnp.floor(jnp.log(nf / MAX_EXACT) / math.log(MAX_DISTANCE / MAX_EXACT)
                                  * (N_BUCKETS - MAX_EXACT)).astype(i32)
    large = jnp.minimum(large, N_BUCKETS - 1)
    bucket = jnp.where(n < MAX_EXACT, n, large)
    out = jnp.zeros(dist.shape, f32)
    for b in range(N_BUCKETS):
        out = jnp.where(bucket == b, rb_ref[b, head], out)
    return jnp.where(valid, out - shift, NEG)


def _tables_kernel(rb_ref, wt_ref, ct_ref, scmp_ref, sk_ref, snew_ref, swin_ref, c31s_ref,
                   *, past_len, ncp_s, n_cmp_s):
    ik = lax.broadcasted_iota(i32, (Q_TILE, Q_TILE), 0)
    iq = lax.broadcasted_iota(i32, (Q_TILE, Q_TILE), 1)
    n_w = WINDOW // Q_TILE + 1
    for g in range(N_KV):
        wt_ref[g, 0] = jnp.full(wt_ref.shape[2:], NEG, f32)
        for hh in range(GROUP):
            head = g * GROUP + hh
            far = rb_ref[N_BUCKETS - 1, head]
            cols = slice(hh * Q_TILE, (hh + 1) * Q_TILE)
            for w in range(n_w):
                dist = Q_TILE * (n_w - 1 - w) + iq - ik
                wt_ref[g, w + 1, :, cols] = _bias_of_dist(dist, (dist >= 0) & (dist < WINDOW), rb_ref, head, far)
            mm = lax.broadcasted_iota(i32, (CMP_WIN, Q_TILE), 0)
            jq = lax.broadcasted_iota(i32, (CMP_WIN, Q_TILE), 1)
            dist = jq - CMP_STRIDE * (mm - CMP_PAD) - (CMP_BLOCK - 1)
            ct_ref[g, :, cols] = _bias_of_dist(dist, dist >= 0, rb_ref, head, far)
    def sample_table(shape, q_axis, dist_fn, valid_fn, out_ref):
        c = lax.broadcasted_iota(i32, shape, q_axis)
        r = lax.broadcasted_iota(i32, shape, 1 - q_axis)
        t = c & 3
        hd = jnp.minimum(lax.shift_right_logical(c, 2), N_HEADS - 1)
        dist = dist_fn(r, t)
        valid = valid_fn(r, t, dist)
        acc = jnp.zeros(shape, f32)
        for head in range(N_HEADS):
            acc = jnp.where(hd == head, _bias_of_dist(dist, valid, rb_ref, head), acc)
        out_ref[...] = acc

    sample_table((ncp_s, LANES), 1, lambda r, t: past_len + t - CMP_STRIDE * (r - CMP_PAD) - (CMP_BLOCK - 1),
                 lambda r, t, d: (r >= CMP_PAD) & (r < CMP_PAD + n_cmp_s) & (d >= 0), scmp_ref)
    sample_table((SAMPLE_COLS, PAGE_SIZE), 0, lambda r, t: PAGE_SIZE + t - r, lambda r, t, d: d >= 0, sk_ref)
    sample_table((SAMPLE_COLS, LANES), 0, lambda r, t: t - r, lambda r, t, d: (d >= 0) & (r < 4), snew_ref)
    sample_table((SAMPLE_COLS, WINDOW), 0, lambda r, t: WINDOW + t - r, lambda r, t, d: d < WINDOW, swin_ref)
    sample_table((SAMPLE_COLS, LANES), 0, lambda r, t: jnp.full_like(r, MAX_DISTANCE), lambda r, t, d: d > 0, c31s_ref)


def _tables(rel_bias, past_len, ncp_s, n_cmp_s):
    n_w = WINDOW // Q_TILE + 1
    cols = GROUP * Q_TILE
    shapes = [
        (N_KV, n_w + 1, Q_TILE, cols), (N_KV, CMP_WIN, cols),
        (ncp_s, LANES), (SAMPLE_COLS, PAGE_SIZE), (SAMPLE_COLS, LANES), (SAMPLE_COLS, WINDOW), (SAMPLE_COLS, LANES),
    ]
    return pl.pallas_call(
        functools.partial(_tables_kernel, past_len=past_len, ncp_s=ncp_s, n_cmp_s=n_cmp_s),
        in_specs=[pl.BlockSpec(memory_space=pltpu.SMEM)],
        out_shape=[jax.ShapeDtypeStruct(s, f32) for s in shapes],
        compiler_params=pltpu.CompilerParams(vmem_limit_bytes=VMEM_LIMIT_BYTES),
        name="bias_tables",
    )(rel_bias)


def _ncp(n_chunks):
    return -(-(n_chunks + CMP_PAD + SUBLANES) // LANES) * LANES


def _compress_core(chunk_rows, w_ref, pos_ref, out_ref, n_chunks):
    half = KV_W // 2
    out_ref[...] = jnp.zeros(out_ref.shape, out_ref.dtype)
    for c in range(2):
        fs = None
        pb = None
        for p in range(CMP_STRIDE // 2):
            x = jnp.concatenate([chunk_rows(2 * p, c), chunk_rows(2 * p + 1, c)], axis=1).astype(bf16)
            lanes = [slice((2 * p + k) * KV_W + c * half, (2 * p + k) * KV_W + (c + 1) * half) for k in range(2)]
            px = jnp.concatenate([pos_ref[:, lanes[0]], pos_ref[:, lanes[1]]], axis=1)
            fs = _dot(x, w_ref[p, c]) if fs is None else fs + _dot(x, w_ref[p, c])
            pb = _dot(px, w_ref[p, c]) if pb is None else pb + _dot(px, w_ref[p, c])
        bias = pb[0:1, :half] + pb[1:2, half:]
        nxt = pltpu.roll(fs[:, half:], n_chunks - 1, axis=0)
        kc = fs[:, :half] + nxt + bias
        r = lax.broadcasted_iota(i32, kc.shape, 0)
        kc = jnp.where(r < n_chunks - 1, kc, 0.0)
        out_ref[0, CMP_PAD:CMP_PAD + n_chunks, c * half:(c + 1) * half] = kc.astype(out_ref.dtype)


def _compress_prompt_kernel(x_ref, w_ref, pos_ref, out_ref, *, n_chunks):
    half = KV_W // 2
    rows = lambda l, c: x_ref[:, l * KV_W + c * half:l * KV_W + (c + 1) * half]
    _compress_core(rows, w_ref, pos_ref, out_ref, n_chunks)


def _gather_pages(pt_ref, pages_hbm, buf_ref, sem_ref, n_pages):
    b = pl.program_id(0)
    slot = b % 2

    def copies(seq, into):
        return [pltpu.make_async_copy(pages_hbm.at[pt_ref[seq, k]], buf_ref.at[into, k], sem_ref.at[into])
                for k in range(n_pages)]

    @pl.when(b == 0)
    def _first():
        for c in copies(0, 0):
            c.start()

    @pl.when(b + 1 < pl.num_programs(0))
    def _next():
        for c in copies(b + 1, 1 - slot):
            c.start()

    for c in copies(b, slot):
        c.wait()
    return slot


def _compress_paged_kernel(pt_ref, pages_hbm, perm_ref, w_ref, pos_ref, out_ref, buf_ref, xp_ref, sem_ref,
                           *, n_pages, n_chunks):
    half = KV_W // 2
    per_page = PAGE_SIZE // CMP_STRIDE
    slot = _gather_pages(pt_ref, pages_hbm, buf_ref, sem_ref, n_pages)
    for k in range(n_pages):
        xp_ref[k] = _dot_nt(perm_ref[...], buf_ref[slot, k].astype(bf16))

    def rows(l, c):
        x = xp_ref[:, l * per_page:(l + 1) * per_page, c * half:(c + 1) * half]
        return x.reshape(n_pages * per_page, half)

    _compress_core(rows, w_ref, pos_ref, out_ref, n_chunks)


def _pack_w_cmp(w_cmp, cmp_pos):
    w = w_cmp.reshape(2, CMP_STRIDE // 2, 2, 2, HEAD_DIM, HEAD_DIM)
    w4 = jnp.einsum("hplcde,gG->pclgdhGe", w, jnp.eye(N_KV, dtype=f32))
    w4 = w4.reshape(CMP_STRIDE // 2, 2, KV_W, KV_W)

    def pos_row(p):
        return jnp.broadcast_to(p[:, :, None, :], (CMP_STRIDE, 2, N_KV, HEAD_DIM)).reshape(1, -1)

    pos = jnp.concatenate([pos_row(cmp_pos[:CMP_STRIDE]), pos_row(cmp_pos[CMP_STRIDE:]),
                           jnp.zeros((SUBLANES - 2, CMP_STRIDE * KV_W), f32)], axis=0)
    return w4.astype(bf16), pos.astype(bf16)


def _compress_prompt(kv_cmp, w4, pos, b, t):
    n_chunks = t // CMP_STRIDE
    cw = CMP_STRIDE * KV_W
    chunks = kv_cmp.reshape(-1, cw)
    ncp = _ncp(n_chunks)
    return pl.pallas_call(
        functools.partial(_compress_prompt_kernel, n_chunks=n_chunks),
        grid=(b,),
        in_specs=[pl.BlockSpec((n_chunks, cw), lambda i: (i, 0)), _resident(w4.shape), _resident(pos.shape)],
        out_specs=pl.BlockSpec((1, ncp, KV_W), lambda i: (i, 0, 0)),
        out_shape=jax.ShapeDtypeStruct((b, ncp, KV_W), bf16),
        compiler_params=_cparams(("parallel",)),
        name="compress_prompt",
    )(chunks, w4, pos)


def _pages_t(cache):
    n_pool = cache.shape[0]
    return jnp.transpose(cache, (0, 2, 3, 4, 1)).reshape(n_pool, KV_W, cache.shape[1])


def _compress_sample(pages_t, page_table, w4, pos):
    bd, n_pages = page_table.shape
    per_page = PAGE_SIZE // CMP_STRIDE
    n_chunks = n_pages * per_page
    ncp = _ncp(n_chunks)
    tok = np.arange(PAGE_SIZE)
    perm = jnp.asarray(tok[None, :] == (tok[:, None] % per_page) * CMP_STRIDE + tok[:, None] // per_page, dtype=bf16)

    res = lambda a: pl.BlockSpec(a.shape, lambda i, pt: (0,) * a.ndim)
    grid_spec = pltpu.PrefetchScalarGridSpec(
        num_scalar_prefetch=1,
        grid=(bd,),
        in_specs=[pl.BlockSpec(memory_space=pl.ANY), res(perm), res(w4), res(pos)],
        out_specs=pl.BlockSpec((1, ncp, KV_W), lambda i, pt: (i, 0, 0)),
        scratch_shapes=[pltpu.VMEM((2, n_pages, KV_W, PAGE_SIZE), f32), pltpu.VMEM((n_pages, PAGE_SIZE, KV_W), f32),
                        pltpu.SemaphoreType.DMA((2,))],
    )
    return pl.pallas_call(
        functools.partial(_compress_paged_kernel, n_pages=n_pages, n_chunks=n_chunks),
        grid_spec=grid_spec,
        out_shape=jax.ShapeDtypeStruct((bd, ncp, KV_W), bf16),
        compiler_params=_cparams(("arbitrary",)),
        name="compress_sample",
    )(page_table, pages_t, perm, w4, pos)


def _overlap_t(n_sel_rows, n_sel, ncp, n_cmp):
    j = np.arange(n_sel_rows)[:, None]
    n = np.arange(ncp)[None, :] - CMP_PAD
    hit = (n * CMP_STRIDE < (j + 1) * SEL_BLOCK) & (n * CMP_STRIDE + CMP_BLOCK > j * SEL_BLOCK)
    hit &= (n >= 0) & (n < n_cmp) & (j < n_sel)
    return jnp.asarray(hit, dtype=bf16)


def _masked_softmax_cols(s, maybe_empty=True):
    m = jnp.max(s, axis=0, keepdims=True)
    e = jnp.exp(s - m)
    inv = 1.0 / jnp.sum(e, axis=0, keepdims=True)
    if maybe_empty:
        inv = jnp.where(m > 0.5 * NEG, inv, 0.0)
    return e * inv


def _top_rows(imp, k):
    rows = lax.broadcasted_iota(i32, imp.shape, 0).astype(f32)
    sel = jnp.zeros(imp.shape, jnp.bool_)
    v = imp
    for _ in range(k):
        m = jnp.max(v, axis=0, keepdims=True)
        first = jnp.min(jnp.where(v == m, rows, 1e9), axis=0, keepdims=True)
        pick = rows == first
        sel = sel | pick
        v = jnp.where(pick, -jnp.inf, v)
    return sel


def _flash_steps(carry, scores, v_ext):
    m_new = [jnp.maximum(carry[2 * i], jnp.max(s, axis=0, keepdims=True)) for i, s in enumerate(scores)]
    p = [jnp.exp(s - m).astype(bf16) for s, m in zip(scores, m_new)]
    out = []
    for i in range(len(scores)):
        alpha = jnp.exp(carry[2 * i] - m_new[i])
        out.extend((m_new[i], alpha * carry[2 * i + 1] + _dot(v_ext, p[i])))
    return tuple(out)


def _attn_prompt_kernel(q_ref, gn_ref, kc_ref, vce_ref, ks_ref, vse_ref, kw_ref, vwe_ref, emat_ref, ove_ref,
                        wt_ref, ct_ref, out_ref, cadd_ref, *, n_sel, ncp):
    qt = pl.program_id(1)
    cols = GROUP * Q_TILE
    n_w = WINDOW // Q_TILE + 1
    vrows = KV_W // 2
    gates_t = gn_ref[...].T
    n_far = jnp.maximum(qt - 1, 0) // (FAR_CHUNK // Q_TILE)
    groups = range(N_KV)
    qp = [jnp.concatenate(
        [q_ref[:, (g * GROUP + hh) * LANES:(g * GROUP + hh + 1) * LANES] for hh in range(GROUP)], axis=0)
        for g in groups]

    wk = n_w * Q_TILE
    start = pl.multiple_of(qt * Q_TILE, Q_TILE)
    lc, sw = [], []
    for g in groups:
        ri = lax.broadcasted_iota(i32, (ncp, cols), 0)
        cadd_ref[g] = jnp.where((ri >= CMP_PAD) & (ri < SUBLANES * qt), 0.0, NEG)
        wr = SUBLANES * qt + lax.broadcasted_iota(i32, (CMP_WIN, cols), 0)
        cadd_ref[g, pl.ds(pl.multiple_of(SUBLANES * qt, SUBLANES), CMP_WIN), :] = jnp.where(wr >= CMP_PAD, ct_ref[g], NEG)
        lc.append(_dot_nt(kc_ref[0], qp[g]) + cadd_ref[g])
    for g in groups:
        tiles = [wt_ref[g, jnp.where(qt + w >= n_w - 1, w + 1, 0)] for w in range(n_w)]
        sw.append(_dot_nt(kw_ref[0, pl.ds(start, wk), :], qp[g]) + jnp.concatenate(tiles, axis=0))
    live, e_c, e_w = [], [], []
    for g in groups:
        m = jnp.max(lc[g], axis=0, keepdims=True)
        live.append(m > 0.5 * NEG)
        e_c.append(jnp.exp(lc[g] - m))
    for g in groups:
        e_w.append(jnp.exp(sw[g] - jnp.max(sw[g], axis=0, keepdims=True)).astype(bf16))

    o_c, o_w, qx = [], [], []
    for g in groups:
        oc = _dot(vce_ref[0], e_c[g].astype(bf16))
        o_c.append(oc[:vrows] * jnp.where(live[g], 1.0 / oc[vrows:vrows + 1], 0.0))
        e_hi, e_lo = _split_bf16(e_c[g])
        raw = _dot(ove_ref[...], e_hi) + _dot(ove_ref[...], e_lo)
        raw = raw[:n_sel] * jnp.where(live[g], 1.0 / raw[n_sel:n_sel + 1], 0.0)
        imp = raw[:, 0:Q_TILE]
        for hh in range(1, GROUP):
            imp = imp + raw[:, hh * Q_TILE:(hh + 1) * Q_TILE]

        blk = lax.broadcasted_iota(i32, (n_sel, Q_TILE), 0)
        qpos = qt * Q_TILE + lax.broadcasted_iota(i32, (n_sel, Q_TILE), 1)
        cur = lax.shift_right_logical(qpos, SEL_SHIFT)
        forced = (blk == 0) | (blk == cur) | (blk == cur - 1)
        future = blk * SEL_BLOCK > qpos
        imp = jnp.where(forced, FORCE, jnp.where(future, -FORCE, imp))
        sel = _top_rows(imp, min(SEL_TOP, n_sel))
        sb = jnp.where(sel, 0.0, NEG)
        if n_sel < LANES:
            sb = jnp.concatenate([sb, jnp.zeros((LANES - n_sel, Q_TILE), f32)], axis=0)
        sbq = sb.T.astype(bf16)
        qx.append(jnp.concatenate([qp[g], jnp.concatenate([sbq] * GROUP, axis=0)], axis=1))
    for g in groups:
        ow = _dot(vwe_ref[0, :, pl.ds(start, wk)], e_w[g])
        o_w.append(ow[:vrows] * (1.0 / ow[vrows:vrows + 1]))

    def scores(start, size, g):
        kx = jnp.concatenate([ks_ref[pl.ds(start, size), :], emat_ref[pl.ds(start, size), :]], axis=1)
        return _dot_nt(kx, qx[g])

    def far_step(i, carry):
        ks = pl.multiple_of(i * FAR_CHUNK, FAR_CHUNK)
        v_ext = vse_ref[0, :, pl.ds(ks, FAR_CHUNK)]
        return _flash_steps(carry, [scores(ks, FAR_CHUNK, g) for g in groups], v_ext)

    def near_step(kt, carry):
        ks = pl.multiple_of(kt * Q_TILE, Q_TILE)
        v_ext = vse_ref[0, :, pl.ds(ks, Q_TILE)]
        tile = jnp.where(kt >= qt - 1, kt - qt + n_w, 2)
        return _flash_steps(carry, [scores(ks, Q_TILE, g) + wt_ref[g, tile] for g in groups], v_ext)

    init = (jnp.full((1, cols), NEG, f32), jnp.zeros((vse_ref.shape[1], cols), f32)) * N_KV
    carry = lax.fori_loop(0, n_far, far_step, init)
    carry = lax.fori_loop(n_far * (FAR_CHUNK // Q_TILE), qt + 1, near_step, carry)

    heads_t = []
    for g in groups:
        acc = carry[2 * g + 1]
        o_s = acc[:vrows] * (1.0 / acc[vrows:vrows + 1])
        for hh in range(GROUP):
            head = g * GROUP + hh
            cs = slice(hh * Q_TILE, (hh + 1) * Q_TILE)
            ds = slice(g * HEAD_DIM, (g + 1) * HEAD_DIM)
            heads_t.append(gates_t[3 * head:3 * head + 1, :] * o_c[g][ds, cs]
                           + gates_t[3 * head + 1:3 * head + 2, :] * o_s[ds, cs]
                           + gates_t[3 * head + 2:3 * head + 3, :] * o_w[g][ds, cs])
    for pair in range(N_HEADS // 2):
        both = jnp.concatenate(heads_t[2 * pair:2 * pair + 2], axis=0)
        out_ref[:, pair * LANES:(pair + 1) * LANES] = both.T


def _attn_prompt(q_pk, gates, kcp, ks_bf, kw_bf, wt, ct, b, t):
    ncp = kcp.shape[1]
    n_sel = t // SEL_BLOCK
    n_cmp = t // CMP_STRIDE - 1
    n_qt = t // Q_TILE
    assert t % FAR_CHUNK == 0 and n_sel % SUBLANES == 0 and n_sel <= LANES
    cols = GROUP * Q_TILE
    half = KV_W // 2

    def values_t(v):
        v_t = jnp.swapaxes(v, 1, 2)
        return jnp.concatenate([v_t, jnp.ones((b, ONES_ROWS, v.shape[1]), v.dtype)], axis=1)

    prompt = lambda a: a[:b * t].reshape(b, t, -1)
    kw_pad = jnp.pad(prompt(kw_bf), ((0, 0), (WINDOW, 0), (0, 0)))
    vce, vse, vwe = values_t(kcp[:, :, half:]), values_t(prompt(ks_bf)[:, :, half:]), values_t(kw_pad[:, :, half:])
    ove = jnp.concatenate([_overlap_t(n_sel, n_sel, ncp, n_cmp), jnp.ones((ONES_ROWS, ncp), bf16)], axis=0)
    emat = jnp.asarray(np.arange(t)[:, None] // SEL_BLOCK == np.arange(LANES)[None, :], dtype=bf16)
    tp = t + WINDOW
    vr = half + ONES_ROWS
    per_b = lambda shape: pl.BlockSpec((1,) + shape, lambda i, j: (i, 0, 0))
    q_rows = lambda w: pl.BlockSpec((Q_TILE, w), lambda i, j: (i * n_qt + j, 0))
    return pl.pallas_call(
        functools.partial(_attn_prompt_kernel, n_sel=n_sel, ncp=ncp),
        grid=(b, n_qt),
        in_specs=[
            q_rows(N_HEADS * LANES), q_rows(LANES),
            per_b((ncp, half)), per_b((vr, ncp)),
            pl.BlockSpec((t, half), lambda i, j: (i, 0)), per_b((vr, t)),
            per_b((tp, half)), per_b((vr, tp)),
            _resident(emat.shape), _resident(ove.shape), _resident(wt.shape), _resident(ct.shape),
        ],
        out_specs=q_rows(ATTN_DIM),
        out_shape=jax.ShapeDtypeStruct((b * t, ATTN_DIM), f32),
        scratch_shapes=[pltpu.VMEM((N_KV, ncp, cols), f32)],
        compiler_params=_cparams(("parallel", "arbitrary")),
        name="attn_prompt",
    )(q_pk, gates, kcp, vce, ks_bf, vse, kw_pad, vwe, emat, ove, wt, ct)


def _softmax_rows2(s_main, s_new):
    m = jnp.maximum(jnp.max(s_main, axis=1, keepdims=True), jnp.max(s_new, axis=1, keepdims=True))
    e_main, e_new = jnp.exp(s_main - m), jnp.exp(s_new - m)
    inv = 1.0 / (jnp.sum(e_main, axis=1, keepdims=True) + jnp.sum(e_new, axis=1, keepdims=True))
    return (e_main * inv).astype(bf16), (e_new * inv).astype(bf16)


def _attn_sample_kernel(pt_ref, pages_hbm, q_ref, g_ref, kcp_ref, knew_ref, win_ref, wnew_ref, ovt_ref, rmat_ref,
                        emat_ref, scmp_ref, sk_ref, snew_ref, swin_ref, c31_ref, out_ref, buf_ref, ke_ref, vt_ref,
                        sem_ref, *, n_pages, n_sel, past_len):
    half = KV_W // 2
    slot = _gather_pages(pt_ref, pages_hbm, buf_ref, sem_ref, n_pages)
    q = q_ref[0]
    q_rows = q[0:SAMPLE_COLS]

    kcp = kcp_ref[0]
    pc = _masked_softmax_cols(_dot_nt(kcp[:, :half], q) + scmp_ref[...])
    o_c = _dot_tn(pc.astype(bf16), kcp[:, half:])[0:SAMPLE_COLS]
    p_hi, p_lo = _split_bf16(pc)
    imp = _dot(ovt_ref[...], p_hi) + _dot(ovt_ref[...], p_lo)
    i_hi, i_lo = _split_bf16(imp)
    imp = _dot(i_hi, rmat_ref[...]) + _dot(i_lo, rmat_ref[...])
    rows = imp.shape[0]
    blk = lax.broadcasted_iota(i32, (rows, LANES), 0)
    qpos = past_len + (lax.broadcasted_iota(i32, (rows, LANES), 1) & 3)
    cur = lax.shift_right_logical(qpos, SEL_SHIFT)
    forced = (blk == 0) | (blk == cur) | (blk == cur - 1)
    future = blk * SEL_BLOCK > qpos
    imp = jnp.where(forced, FORCE, jnp.where(future, -FORCE, imp))
    imp = jnp.where(blk < n_sel, imp, -jnp.inf)
    sel = _top_rows(imp, min(SEL_TOP, n_sel))
    sb = jnp.where(sel, 0.0, NEG)
    past_blocks = past_len // SEL_BLOCK
    sb_rows = sb[0:past_blocks].T[0:SAMPLE_COLS]

    for k in range(n_pages):
        lanes = slice(k * PAGE_SIZE, (k + 1) * PAGE_SIZE)
        ke_ref[0:half, lanes] = buf_ref[slot, k, 0:half, :].astype(bf16)
        vt_ref[:, lanes] = buf_ref[slot, k, half:, :].astype(bf16)
    ke_ref[half:, :] = emat_ref[...]
    n_far = past_len - PAGE_SIZE
    s = _dot(jnp.concatenate([q_rows, sb_rows.astype(bf16)], axis=1), ke_ref[...])
    s = jnp.concatenate([s[:, :n_far] + c31_ref[:, 0:1], s[:, n_far:] + sk_ref[...]], axis=1)
    knew = knew_ref[0].astype(bf16)
    s_n = _dot(q_rows, knew[0:half]) + snew_ref[...]
    p, p_n = _softmax_rows2(s, s_n)
    o_s = _dot_nt(p, vt_ref[...]) + _dot_nt(p_n, knew[half:])

    win = win_ref[0].astype(bf16)
    wnew = wnew_ref[0].astype(bf16)
    p, p_n = _softmax_rows2(_dot(q_rows, win[0:half]) + swin_ref[...], _dot(q_rows, wnew[0:half]) + snew_ref[...])
    o_w = _dot_nt(p, win[half:]) + _dot_nt(p_n, wnew[half:])

    gt = g_ref[0]
    out_ref[0] = gt[:, 0:1] * o_c + gt[:, 1:2] * o_s + gt[:, 2:3] * o_w


def _attn_sample(q_pk, gates, kcp, pages_t, page_table, ks_new, win_t, kw_new, tabs):
    scmp, sk, snew, swin, c31s = tabs
    bd, t_new, _ = q_pk.shape
    n_pages = page_table.shape[1]
    past_len = n_pages * PAGE_SIZE
    ncp = kcp.shape[1]
    n_cmp = past_len // CMP_STRIDE - 1
    n_sel = -(-(past_len + t_new) // SEL_BLOCK)
    n_sel_rows = -(-n_sel // SUBLANES) * SUBLANES
    past_blocks = past_len // SEL_BLOCK
    n_cols = N_KV * GROUP * t_new
    assert n_cols == SAMPLE_COLS and win_t.shape[2] == WINDOW and past_blocks == LANES
    qc = q_pk.reshape(bd, t_new, N_HEADS, LANES).transpose(0, 2, 1, 3).reshape(bd, n_cols, LANES)
    qc = jnp.pad(qc, ((0, 0), (0, LANES - n_cols), (0, 0)))
    gc = gates[:, :, :3 * N_HEADS].reshape(bd, t_new, N_HEADS, 3).transpose(0, 2, 1, 3).reshape(bd, n_cols, 3)
    gc = jnp.pad(gc, ((0, 0), (0, 0), (0, SUBLANES - 3)))
    new_t = lambda a: jnp.pad(jnp.swapaxes(a, 1, 2), ((0, 0), (0, 0), (0, LANES - t_new)))
    ovt = _overlap_t(n_sel_rows, n_sel, ncp, n_cmp)
    c = np.arange(LANES)
    same = (c[:, None] // (GROUP * t_new) == c[None, :] // (GROUP * t_new)) & (c[:, None] % t_new == c[None, :] % t_new)
    rmat = jnp.asarray(same & (c[:, None] < n_cols) & (c[None, :] < n_cols), dtype=bf16)
    emat = jnp.asarray(np.arange(past_blocks)[:, None] == np.arange(past_len)[None, :] // SEL_BLOCK, dtype=bf16)

    per_b = lambda shape: pl.BlockSpec((1,) + shape, lambda i, pt: (i, 0, 0))
    res = lambda a: pl.BlockSpec(a.shape, lambda i, pt: (0,) * a.ndim)
    half = KV_W // 2
    grid_spec = pltpu.PrefetchScalarGridSpec(
        num_scalar_prefetch=1,
        grid=(bd,),
        in_specs=[pl.BlockSpec(memory_space=pl.ANY),
                  per_b((LANES, LANES)), per_b((n_cols, SUBLANES)), per_b((ncp, KV_W)), per_b((KV_W, LANES)),
                  per_b((KV_W, WINDOW)), per_b((KV_W, LANES)),
                  res(ovt), res(rmat), res(emat), res(scmp), res(sk), res(snew), res(swin), res(c31s)],
        out_specs=per_b((n_cols, LANES)),
        scratch_shapes=[pltpu.VMEM((2, n_pages, KV_W, PAGE_SIZE), f32),
                        pltpu.VMEM((half + past_blocks, past_len), bf16), pltpu.VMEM((half, past_len), bf16),
                        pltpu.SemaphoreType.DMA((2,))],
    )
    o = pl.pallas_call(
        functools.partial(_attn_sample_kernel, n_pages=n_pages, n_sel=n_sel, past_len=past_len),
        grid_spec=grid_spec,
        out_shape=jax.ShapeDtypeStruct((bd, n_cols, LANES), f32),
        compiler_params=_cparams(("arbitrary",)),
        name="attn_sample",
    )(page_table, pages_t, qc, gc, kcp, new_t(ks_new), win_t, new_t(kw_new),
      ovt, rmat, emat, scmp, sk, snew, swin, c31s)
    o = o.reshape(bd, N_KV, GROUP, t_new, N_KV, HEAD_DIM)
    o = jnp.stack([o[:, g, :, :, g, :] for g in range(N_KV)], axis=1)
    return o.transpose(0, 3, 1, 2, 4).reshape(bd, t_new, ATTN_DIM)


def _merge_kernel(xp_ref, xs_ref, ap_ref, as_ref, u0_ref, halo_ref, u1s_ref, u2s_ref, cb_ref, mg_ref, cw_ref, cbias_ref,
                  wau_ref, wcu_ref, wo_ref, g2_ref, x2_ref, h2_ref, *, d_model, prompt_tiles, seq_tiles):
    i = pl.program_id(0)
    u0 = u0_ref[...]
    r = lax.broadcasted_iota(i32, u0.shape, 0)
    keep = jnp.where(i % seq_tiles == 0, 0.0, 1.0)
    p1 = halo_ref[SUBLANES - 1:SUBLANES, :] * keep
    p2 = halo_ref[SUBLANES - 2:SUBLANES - 1, :] * keep
    u1 = jnp.where(r == 0, p1, pltpu.roll(u0, 1, axis=0))
    u2 = jnp.where(r == 0, p2, jnp.where(r == 1, p1, pltpu.roll(u0, 2, axis=0)))
    decode = i >= prompt_tiles
    u1 = jnp.where(decode, u1s_ref[...], u1)
    u2 = jnp.where(decode, u2s_ref[...], u2)
    y = cbias_ref[...] + cw_ref[0:1, :] * u2
    y = y + cw_ref[1:2, :] * u1
    y = y + cw_ref[2:3, :] * u0
    up_a = _dot(_two_sources(ap_ref, as_ref, prompt_tiles).astype(bf16), wau_ref[...])
    up_c = _dot((cb_ref[...] * y).astype(bf16), wcu_ref[...])
    mixed = mg_ref[:, :d_model] * up_a + mg_ref[:, d_model:] * up_c
    x2 = _two_sources(xp_ref, xs_ref, prompt_tiles) + _dot(mixed.astype(bf16), wo_ref[...])
    x2_ref[...] = x2
    h2_ref[...] = _rmsnorm(x2, g2_ref[...]).astype(bf16)


def _merge(x_p, x_s, attn_p, attn_s, cin, u1s, u2s, cb, mg, conv_w, conv_b, w_attn_up, w_conv_up, w_out, norm2_g, t):
    n_prompt, d = x_p.shape
    n = n_prompt + x_s.shape[0]
    tm = ROW_TILE
    assert CONV_W == 3 and t % tm == 0 and n_prompt % tm == 0 and (n - n_prompt) % tm == 0
    prompt_tiles = n_prompt // tm
    row = lambda w: pl.BlockSpec((tm, w), lambda i: (i, 0))
    halo = pl.BlockSpec((SUBLANES, CONV_DIM), lambda i: (jnp.maximum(i * (tm // SUBLANES) - 1, 0), 0))
    dec = pl.BlockSpec((tm, CONV_DIM), lambda i: (jnp.maximum(i - prompt_tiles, 0), 0))
    cw = jnp.pad(conv_w, ((0, SUBLANES - CONV_W), (0, 0)))
    return pl.pallas_call(
        functools.partial(_merge_kernel, d_model=d, prompt_tiles=prompt_tiles, seq_tiles=t // tm),
        grid=(n // tm,),
        in_specs=_two_specs(tm, d, prompt_tiles) + _two_specs(tm, ATTN_DIM, prompt_tiles)
        + [row(CONV_DIM), halo, dec, dec, row(CONV_DIM), row(2 * d),
           _resident(cw.shape), _resident((1, CONV_DIM)), _resident(w_attn_up.shape),
           _resident(w_conv_up.shape), _resident(w_out.shape), _resident((1, d))],
        out_specs=[row(d), row(d)],
        out_shape=[jax.ShapeDtypeStruct((n, d), f32), jax.ShapeDtypeStruct((n, d), bf16)],
        compiler_params=_cparams(("parallel",)),
        name="merge",
    )(x_p, x_s, attn_p, attn_s, cin, cin, u1s, u2s, cb, mg, cw, conv_b.reshape(1, CONV_DIM), w_attn_up.astype(bf16),
      w_conv_up.astype(bf16), w_out.astype(bf16), norm2_g.reshape(1, d))


def _top_rows_sorted(s, k):
    rows = lax.broadcasted_iota(i32, s.shape, 0).astype(f32)
    vals, idxs = [], []
    for _ in range(k):
        m = jnp.max(s, axis=0, keepdims=True)
        first = jnp.min(jnp.where(s == m, rows, 1e9), axis=0, keepdims=True)
        vals.append(m)
        idxs.append(first)
        s = jnp.where(rows == first, -jnp.inf, s)
    return jnp.concatenate(vals, axis=0), jnp.concatenate(idxs, axis=0)


def _peer_route_kernel(h_ref, wq_ref, k1_ref, k2_ref, a_ref, b_ref, g_ref):
    qp = _dot(h_ref[...], wq_ref[...]).astype(bf16)
    key_dim = 2 * PEER_HALF
    a_rows, b_rows, g_rows = [], [], []
    for h in range(PEER_HEADS):
        q1 = qp[:, h * key_dim:h * key_dim + PEER_HALF]
        q2 = qp[:, h * key_dim + PEER_HALF:(h + 1) * key_dim]
        v1, i1 = _top_rows_sorted(_dot_nt(k1_ref[...], q1), PEER_TOPK)
        v2, i2 = _top_rows_sorted(_dot_nt(k2_ref[...], q2), PEER_TOPK)
        counts = [PEER_TOPK // (r + 1) for r in range(PEER_TOPK)]
        cand = jnp.concatenate([v1[r:r + 1, :] + v2[0:n, :] for r, n in enumerate(counts)], axis=0)
        code = jnp.concatenate([i1[r:r + 1, :] * PEER_NKEYS + i2[0:n, :] for r, n in enumerate(counts)], axis=0)
        pad = -sum(counts) % SUBLANES
        cand = jnp.concatenate([cand, jnp.full((pad, cand.shape[1]), -jnp.inf, f32)], axis=0)
        code = jnp.concatenate([code, jnp.zeros((pad, code.shape[1]), f32)], axis=0)
        rows = lax.broadcasted_iota(i32, cand.shape, 0).astype(f32)
        sc, ex = [], []
        for _ in range(PEER_TOPK):
            m = jnp.max(cand, axis=0, keepdims=True)
            first = jnp.min(jnp.where(cand == m, rows, 1e9), axis=0, keepdims=True)
            pick = rows == first
            sc.append(m)
            ex.append(jnp.max(jnp.where(pick, code, -1.0), axis=0, keepdims=True))
            cand = jnp.where(pick, -jnp.inf, cand)
        sc = jnp.concatenate(sc, axis=0)
        ex = jnp.concatenate(ex, axis=0)
        e1 = jnp.floor(ex * (1.0 / PEER_NKEYS))
        e = jnp.exp(sc - sc[0:1, :])
        a_rows.append(e1)
        b_rows.append(ex - e1 * PEER_NKEYS)
        g_rows.append(e / jnp.sum(e, axis=0, keepdims=True))
    tn = h_ref.shape[0]
    for src, dst in ((a_rows, a_ref), (b_rows, b_ref), (g_rows, g_ref)):
        full = jnp.concatenate(src, axis=0)
        for c in range(tn // LANES):
            dst[c * LANES:(c + 1) * LANES, :] = full[:, c * LANES:(c + 1) * LANES].T


def _peer_route(h2, wq, k1, k2):
    n, d = h2.shape
    tn = ROW_TILE
    assert PEER_HEADS * PEER_TOPK == LANES
    row = lambda w: pl.BlockSpec((tn, w), lambda i: (i, 0))
    return pl.pallas_call(
        _peer_route_kernel,
        grid=(n // tn,),
        in_specs=[row(d), _resident(wq.shape), _resident(k1.shape), _resident(k2.shape)],
        out_specs=[row(LANES)] * 3,
        out_shape=[jax.ShapeDtypeStruct((n, LANES), f32)] * 3,
        compiler_params=_cparams(("parallel",)),
        name="peer_route",
    )(h2, wq.astype(bf16), k1.astype(bf16), k2.astype(bf16))


def _peer_expert_kernel(h_ref, a_ref, b_ref, w_ref, x_ref, u_ref, v_ref, out_ref, g_ref, acc_ref, *, pitch):
    e_step = pl.program_id(1)
    tn = h_ref.shape[0]
    pairs_per_chunk = EXPERT_CHUNK // PEER_NKEYS // 2
    half_keys = PEER_NKEYS // 2

    @pl.when(e_step == 0)
    def _build_gate_matrix():
        key_row = lax.broadcasted_iota(i32, (PEER_NKEYS, LANES), 0).astype(f32)

        def token(n, c):
            a = jnp.broadcast_to(a_ref[pl.ds(n, 1), :], (PEER_NKEYS, LANES))
            b = jnp.broadcast_to(b_ref[pl.ds(n, 1), :], (PEER_NKEYS, LANES))
            w = jnp.broadcast_to(w_ref[pl.ds(n, 1), :], (PEER_NKEYS, LANES))
            pa = (a == key_row).astype(bf16)
            wb = jnp.where(b == key_row, w, 0.0).astype(bf16)
            g = _dot_nt(pa, wb).astype(bf16)
            g_ref[pl.ds(n, half_keys, stride=pitch), :] = pltpu.bitcast(g, jnp.uint32)
            return c

        lax.fori_loop(0, tn, token, 0, unroll=TOKEN_UNROLL)
        acc_ref[...] = jnp.zeros_like(acc_ref)

    s = _dot_nt(h_ref[...], u_ref[...])
    gates = []
    for j in range(pairs_per_chunk):
        words = g_ref[pl.ds(pl.multiple_of((e_step * pairs_per_chunk + j) * pitch, SUBLANES), tn), :]
        gates.append(pltpu.bitcast(words, bf16).reshape(tn, 2 * LANES))
    gates = jnp.concatenate(gates, axis=1).astype(f32)
    act = 0.5 * s * (1.0 + lax.erf(s * math.sqrt(0.5)))
    acc_ref[...] += _dot((gates * act).astype(bf16), v_ref[...])

    @pl.when(e_step == pl.num_programs(1) - 1)
    def _finish():
        out_ref[...] = x_ref[...] + acc_ref[...]


def _peer_expert(h2, a_idx, b_idx, gate_w, x2, u_tab, v_tab):
    n, d = h2.shape
    n_exp = u_tab.shape[0]
    tn = PEER_TILE
    pitch = tn + G_PITCH_PAD
    assert n % tn == 0 and n_exp == PEER_NKEYS * PEER_NKEYS and EXPERT_CHUNK % (2 * PEER_NKEYS) == 0
    row = lambda w: pl.BlockSpec((tn, w), lambda i, e: (i, 0))
    tab = pl.BlockSpec((EXPERT_CHUNK, d), lambda i, e: (e, 0))
    return pl.pallas_call(
        functools.partial(_peer_expert_kernel, pitch=pitch),
        grid=(n // tn, n_exp // EXPERT_CHUNK),
        in_specs=[row(d), row(LANES), row(LANES), row(LANES), row(d), tab, tab],
        out_specs=row(d),
        out_shape=jax.ShapeDtypeStruct((n, d), f32),
        scratch_shapes=[pltpu.VMEM((PEER_NKEYS // 2 * pitch, LANES), jnp.uint32), pltpu.VMEM((tn, d), f32)],
        compiler_params=_cparams(("parallel", "arbitrary")),
        name="peer_expert",
    )(h2, a_idx, b_idx, gate_w, x2, u_tab.astype(bf16), v_tab.astype(bf16))


def _ple_kernel(x_ref, p_ref, pg_ref, wg_ref, wp_ref, fg_ref, y_ref):
    x = x_ref[...]
    gate = jax.nn.sigmoid(_dot(_rmsnorm(x, pg_ref[...]).astype(bf16), wg_ref[...]))
    x = x + gate * _dot(p_ref[...].astype(bf16), wp_ref[...])
    y_ref[...] = _rmsnorm(x, fg_ref[...])


def _ple(x3, row0, p, ple_g, w_ple_gate, w_ple_proj, final_g):
    n, d = p.shape[0], x3.shape[1]
    tm = ROW_TILE
    assert n % tm == 0 and row0 % tm == 0
    row = lambda w: pl.BlockSpec((tm, w), lambda i: (i, 0))
    return pl.pallas_call(
        _ple_kernel,
        grid=(n // tm,),
        in_specs=[pl.BlockSpec((tm, d), lambda i: (i + row0 // tm, 0)), row(p.shape[1]), _resident((1, d)),
                  _resident(w_ple_gate.shape), _resident(w_ple_proj.shape), _resident((1, d))],
        out_specs=row(d),
        out_shape=jax.ShapeDtypeStruct((n, d), f32),
        compiler_params=_cparams(("parallel",)),
        name="ple_final_rows_%d" % row0,
    )(x3, p, ple_g.reshape(1, d), w_ple_gate.astype(bf16), w_ple_proj.astype(bf16), final_g.reshape(1, d))


def kernel(x_prompt, x_sample, p_prompt, p_sample, cache_cmp_kv, cache_slc_kv, page_table, state_win_kv, state_conv, norm1_g, w_in, w_cmp, cmp_pos, conv_w, conv_b, w_attn_up, w_conv_up, w_out, norm2_g, peer_wq, peer_k1, peer_k2, peer_u, peer_v, ple_g, w_ple_gate, w_ple_proj, rel_bias, final_g):
    depth = norm1_g.shape[0]
    assert depth == 1, "single-layer trunk"
    b, t, d = x_prompt.shape
    bd, t_new, _ = x_sample.shape
    n_pages = page_table.shape[1]
    past_len = n_pages * PAGE_SIZE
    n_p, n_s = b * t, bd * t_new
    kv_shape = lambda lead: lead + (2, N_KV, HEAD_DIM)

    x_p, x_s = x_prompt.reshape(n_p, d), x_sample.reshape(n_s, d)
    w_pack = _pack_w_in(w_in[0], d)
    q_pk, kvc, kvs, kvw, kvs_bf, kvw_bf, gates, cin, cbg, mg = _in_proj(x_p, x_s, norm1_g[0], w_pack)
    prompt = lambda a: a[:n_p].reshape(b, t, -1)
    sample = lambda a: a[n_p:].reshape(bd, t_new, -1)

    w4, pos_rows = _pack_w_cmp(w_cmp[0], cmp_pos[0])
    kcp_p = _compress_prompt(kvc, w4, pos_rows, b, t)
    kcp_s = _compress_sample(_pages_t(cache_cmp_kv[0]), page_table, w4, pos_rows)
    n_cmp_s = past_len // CMP_STRIDE - 1
    wt, ct, *sample_tabs = _tables(rel_bias, past_len, kcp_s.shape[1], n_cmp_s)

    kvs_s, kvw_s = sample(kvs), sample(kvw)
    attn_p = _attn_prompt(q_pk, gates, kcp_p, kvs_bf, kvw_bf, wt, ct, b, t)
    win_t = jnp.transpose(state_win_kv[0], (0, 2, 3, 4, 1)).reshape(bd, KV_W, -1)
    attn_s = _attn_sample(sample(q_pk), sample(gates), kcp_s, _pages_t(cache_slc_kv[0]), page_table, kvs_s, win_t,
                          kvw_s, sample_tabs)

    cin_s = sample(cin)
    buf_s = state_conv[0]
    back = lambda k: jnp.concatenate([buf_s[:, CONV_W - 1 - k:], cin_s[:, :t_new - k]], axis=1).reshape(n_s, CONV_DIM)
    x2, h2 = _merge(x_p, x_s, attn_p, attn_s.reshape(n_s, ATTN_DIM), cin, back(1), back(2), cbg, mg, conv_w[0],
                    conv_b[0], w_attn_up[0], w_conv_up[0], w_out[0], norm2_g[0], t)

    a_idx, b_idx, gate_w = _peer_route(h2, peer_wq[0], peer_k1[0], peer_k2[0])
    x3 = _peer_expert(h2, a_idx, b_idx, gate_w, x2, peer_u[0], peer_v[0])

    ple_w = (ple_g[0], w_ple_gate[0], w_ple_proj[0], final_g)
    y_p = _ple(x3, 0, p_prompt[0].reshape(n_p, -1), *ple_w)
    y_s = _ple(x3, n_p, p_sample[0].reshape(n_s, -1), *ple_w)

    n_win = min(WINDOW, t)
    assert t >= CONV_W - 1 and t_new >= CONV_W - 1
    new_win_s = jnp.concatenate([state_win_kv[:, :, t_new:], kvw_s.reshape(kv_shape((1, bd, t_new)))], axis=2)
    return (
        y_p.reshape(b, t, d),
        y_s.reshape(bd, t_new, d),
        prompt(kvc).reshape(kv_shape((1, b, t))),
        sample(kvc).reshape(kv_shape((1, bd, t_new))),
        prompt(kvs).reshape(kv_shape((1, b, t))),
        kvs_s.reshape(kv_shape((1, bd, t_new))),
        prompt(kvw)[:, t - n_win:].reshape(kv_shape((1, b, n_win))),
        new_win_s,
        prompt(cin)[None, :, t - (CONV_W - 1):],
        cin_s[None, :, t_new - (CONV_W - 1):],
    )
```
